```python
import math
import jax
import jax.numpy as jnp
from jax import lax
import numpy as np


D_MODEL = 1024
BATCH = 8
SEQ = 4096
DEPTH = 2

HEAD_DIM = 64
ROPE_THETA = 10000.0
BLOCK = 128
LN_EPS = 1e-5

POOL_WINDOWS = (2, 4, 8, 16)
POOL_GROUP = 64
POOL_WIDTH = len(POOL_WINDOWS) * POOL_GROUP
SWA_WINDOW = 128
SWA_Q_HEADS = 12
SWA_KV_HEADS = 4
SWA_Q_DIM = SWA_Q_HEADS * HEAD_DIM
SWA_KV_DIM = SWA_KV_HEADS * HEAD_DIM
DIL_PAIRS = ((128, 1), (512, 4), (2048, 16))
DIL_Q_HEADS = 8
DIL_KV_HEADS = 2
DIL_Q_DIM = DIL_Q_HEADS * HEAD_DIM
DIL_KV_DIM = DIL_KV_HEADS * HEAD_DIM
DIL_GROUP_IN = DIL_Q_DIM + 2 * DIL_KV_DIM
CONV_WIDTH = 512
CONV_K = 3
D_FF = 2816
N_EXPERTS = 8
TOP_K = 2
D_FF_EXPERT = 3584
MOE_BLOCK = 256

ALPHA = (2 * DEPTH) ** 0.25
BETA = (8 * DEPTH) ** -0.25
N_EVEN = (DEPTH + 1) // 2
N_ODD = DEPTH // 2
EVEN_IN = POOL_WIDTH + SWA_Q_DIM + 2 * SWA_KV_DIM
EVEN_MIX = POOL_WIDTH + SWA_Q_DIM
C_IN = len(DIL_PAIRS) * DIL_GROUP_IN
ODD_IN = C_IN + 3 * CONV_WIDTH
ODD_MIX = DIL_Q_DIM + CONV_WIDTH

kernel_name = 'hybrid_pool_swa_dilated_conv_moe'


def layer_norm(x, w, b):
    xf = x.astype(jnp.float32)
    mu = xf.mean(-1, keepdims=True)
    var = jnp.square(xf - mu).mean(-1, keepdims=True)
    return ((xf - mu) * lax.rsqrt(var + LN_EPS) * w + b).astype(x.dtype)


def rope(x, positions):
    half = x.shape[-1] // 2
    inv = ROPE_THETA ** (-jnp.arange(half, dtype=jnp.float32) / half)
    ang = positions.astype(jnp.float32)[..., None] * inv
    cos = jnp.cos(ang)[:, :, None, :]
    sin = jnp.sin(ang)[:, :, None, :]
    xf = x.astype(jnp.float32)
    x1, x2 = xf[..., :half], xf[..., half:]
    return jnp.concatenate([x1 * cos - x2 * sin, x2 * cos + x1 * sin], -1).astype(x.dtype)


def banded_attention(q, k, v, max_dist, sinks=None):
    n, L, H, dh = q.shape
    kvh = k.shape[2]
    g = H // kvh
    nblk = -(-L // BLOCK)
    Lp = nblk * BLOCK
    pad = ((0, 0), (0, Lp - L), (0, 0), (0, 0))
    q, k, v = jnp.pad(q, pad), jnp.pad(k, pad), jnp.pad(v, pad)
    qb = q.reshape(n, nblk, BLOCK, kvh, g, dh).transpose(1, 0, 2, 3, 4, 5)

    def windows(t):
        tb = t.reshape(n, nblk, BLOCK, kvh, dh)
        prev = jnp.concatenate([jnp.zeros_like(tb[:, :1]), tb[:, :-1]], axis=1)
        return jnp.concatenate([prev, tb], axis=2).transpose(1, 0, 2, 3, 4)

    kw, vw = windows(k), windows(v)
    qi = jnp.arange(BLOCK)[:, None]
    sj = jnp.arange(2 * BLOCK)[None, :]
    dist = BLOCK + qi - sj
    band = (dist >= 0) & (dist <= max_dist)
    scale = 1.0 / math.sqrt(dh)

    def block(args):
        qblk, kblk, vblk, bi = args
        s = jnp.einsum('nqkgd,nskd->nkgqs', qblk, kblk, preferred_element_type=jnp.float32) * scale
        valid = band & ((bi * BLOCK + sj - BLOCK) >= 0)
        s = jnp.where(valid, s, -jnp.inf)
        m = s.max(-1, keepdims=True)
        if sinks is not None:
            sk = sinks.astype(jnp.float32).reshape(kvh, g, 1, 1)
            m = jnp.maximum(m, sk)
        p = jnp.exp(s - m)
        den = p.sum(-1, keepdims=True)
        if sinks is not None:
            den = den + jnp.exp(sk - m)
        o = jnp.einsum('nkgqs,nskd->nqkgd', (p / den).astype(vblk.dtype), vblk)
        return o, (m + jnp.log(den))[..., 0]

    o, lse = lax.map(block, (qb, kw, vw, jnp.arange(nblk)))
    o = o.transpose(1, 0, 2, 3, 4, 5).reshape(n, Lp, H, dh)[:, :L]
    lse = lse.transpose(1, 0, 4, 2, 3).reshape(n, Lp, H)[:, :L]
    return o, lse


def multiscale_pool(u, w_group, scale):
    b, s, _ = u.shape
    uf = u.astype(jnp.float32)
    csum = jnp.cumsum(uf, axis=1)
    count = jnp.arange(1, s + 1, dtype=jnp.float32)[None, :, None]
    diffs = []
    for gi, w in enumerate(POOL_WINDOWS):
        sl = slice(gi * POOL_GROUP, (gi + 1) * POOL_GROUP)
        cg = csum[..., sl]
        lagged = jnp.pad(cg, ((0, 0), (w, 0), (0, 0)))[:, :s]
        diffs.append((cg - lagged) / jnp.minimum(count, w) - uf[..., sl])
    d = jnp.stack(diffs, axis=2).astype(u.dtype)
    y = jnp.einsum('bsgc,gcd->bsgd', d, w_group).reshape(b, s, POOL_WIDTH)
    return y * scale


def even_mixer(x, positions, w_in, pool_w, pool_scale, sinks, w_out):
    b, s, _ = x.shape
    h = x @ w_in
    u, q, k, v = jnp.split(h, [POOL_WIDTH, POOL_WIDTH + SWA_Q_DIM, POOL_WIDTH + SWA_Q_DIM + SWA_KV_DIM], axis=-1)
    a_out = multiscale_pool(u, pool_w, pool_scale)
    q = rope(q.reshape(b, s, SWA_Q_HEADS, HEAD_DIM), positions)
    k = rope(k.reshape(b, s, SWA_KV_HEADS, HEAD_DIM), positions)
    v = v.reshape(b, s, SWA_KV_HEADS, HEAD_DIM)
    b_out, _ = banded_attention(q, k, v, SWA_WINDOW - 1, sinks)
    return jnp.concatenate([a_out, b_out.reshape(b, s, SWA_Q_DIM)], -1) @ w_out


def dilated_attention(q, k, v, window, dil):
    b, s = q.shape[:2]

    def to_sub(t):
        return t.reshape(b, s // dil, dil, *t.shape[2:]).swapaxes(1, 2).reshape(b * dil, s // dil, *t.shape[2:])

    def from_sub(t):
        return t.reshape(b, dil, s // dil, *t.shape[2:]).swapaxes(1, 2).reshape(b, s, *t.shape[2:])

    o, lse = banded_attention(to_sub(q), to_sub(k), to_sub(v), window // dil)
    return from_sub(o), from_sub(lse)


def short_conv(h, gate_b, gate_c, conv_w):
    z = gate_c * h
    y = lax.conv_general_dilated(z, conv_w[:, None, :], window_strides=(1,), padding=[(CONV_K - 1, 0)],
                                 dimension_numbers=('NWC', 'WIO', 'NWC'), feature_group_count=CONV_WIDTH)
    return gate_b * y


def odd_mixer(x, positions, w_in, conv_w, w_out):
    b, s, _ = x.shape
    h = x @ w_in
    outs, lses = [], []
    for gi, (window, dil) in enumerate(DIL_PAIRS):
        hg = h[..., gi * DIL_GROUP_IN:(gi + 1) * DIL_GROUP_IN]
        q, k, v = jnp.split(hg, [DIL_Q_DIM, DIL_Q_DIM + DIL_KV_DIM], axis=-1)
        q = rope(q.reshape(b, s, DIL_Q_HEADS, HEAD_DIM), positions)
        k = rope(k.reshape(b, s, DIL_KV_HEADS, HEAD_DIM), positions)
        v = v.reshape(b, s, DIL_KV_HEADS, HEAD_DIM)
        o, lse = dilated_attention(q, k, v, window, dil)
        outs.append(o)
        lses.append(lse)
    wts = jax.nn.softmax(jnp.stack(lses), axis=0)
    c_out = jnp.sum(wts[..., None] * jnp.stack(outs).astype(jnp.float32), axis=0).astype(x.dtype)
    hd, gate_b, gate_c = jnp.split(h[..., C_IN:], 3, axis=-1)
    d_out = short_conv(hd, gate_b, gate_c, conv_w)
    return jnp.concatenate([c_out.reshape(b, s, DIL_Q_DIM), d_out], -1) @ w_out


def swiglu(x, w_gate, w_up, w_down):
    return (jax.nn.silu(x @ w_gate) * (x @ w_up)) @ w_down


def moe_swiglu(x, w_router, w_gate, w_up, w_down):
    b, s, d = x.shape
    xt = x.reshape(-1, d)
    T = xt.shape[0]
    logits = (xt @ w_router).astype(jnp.float32)
    top_logit, top_e = lax.top_k(logits, TOP_K)
    gates = jax.nn.softmax(top_logit, axis=-1)
    e_flat = top_e.reshape(-1)
    g_flat = gates.reshape(-1)
    tok_flat = jnp.arange(T * TOP_K, dtype=jnp.int32) // TOP_K
    order = jnp.argsort(e_flat)
    e_sorted = e_flat[order]
    counts = jnp.bincount(e_flat, length=N_EXPERTS)
    padded = (counts + MOE_BLOCK - 1) // MOE_BLOCK * MOE_BLOCK
    start = jnp.cumsum(counts) - counts
    pend = jnp.cumsum(padded)
    pstart = pend - padded
    dest = pstart[e_sorted] + jnp.arange(T * TOP_K) - start[e_sorted]
    n_rows = -(-(T * TOP_K + N_EXPERTS * (MOE_BLOCK - 1)) // MOE_BLOCK) * MOE_BLOCK
    n_blocks = n_rows // MOE_BLOCK
    row_tok = jnp.full((n_rows,), T, jnp.int32).at[dest].set(tok_flat[order])
    row_gate = jnp.zeros((n_rows,), jnp.float32).at[dest].set(g_flat[order])
    block_e = jnp.minimum(jnp.searchsorted(pend, jnp.arange(n_blocks) * MOE_BLOCK, side='right'), N_EXPERTS - 1)
    xrows = jnp.concatenate([xt, jnp.zeros((1, d), xt.dtype)], 0)[row_tok].reshape(n_blocks, MOE_BLOCK, d)

    def expert_block(args):
        xb, e = args
        return (jax.nn.silu(xb @ w_gate[e]) * (xb @ w_up[e])) @ w_down[e]

    y = lax.map(expert_block, (xrows, block_e)).reshape(n_rows, d)
    out = jnp.zeros((T + 1, d), jnp.float32).at[row_tok].add(y.astype(jnp.float32) * row_gate[:, None])
    return out[:T].astype(x.dtype).reshape(b, s, d)


def setup_inputs(seed: int = 0) -> dict:
    key = jax.random.key(seed)
    ks = jax.random.split(key, 20)
    f32 = jnp.float32

    def nrm(k, shape, fan_in, gain=1.0):
        return jax.random.normal(k, shape, f32) * (gain * fan_in ** -0.5)

    x = jax.random.normal(ks[0], (BATCH, SEQ, D_MODEL), f32)
    offsets = jax.random.randint(ks[1], (BATCH, 1), 0, SEQ, jnp.int32)
    positions = offsets + jnp.arange(SEQ, dtype=jnp.int32)[None, :]
    ln_w = 1.0 + 0.02 * jax.random.normal(ks[2], (DEPTH, 2, D_MODEL), f32)
    ln_b = 0.02 * jax.random.normal(ks[3], (DEPTH, 2, D_MODEL), f32)
    even_w_in = nrm(ks[4], (N_EVEN, D_MODEL, EVEN_IN), D_MODEL)
    pool_w = nrm(ks[5], (N_EVEN, len(POOL_WINDOWS), POOL_GROUP, POOL_GROUP), POOL_GROUP)
    pool_scale = 1.0 + 0.02 * jax.random.normal(ks[6], (N_EVEN, POOL_WIDTH), f32)
    swa_sinks = 0.5 * jax.random.normal(ks[7], (N_EVEN, SWA_Q_HEADS), f32)
    even_w_out = nrm(ks[8], (N_EVEN, EVEN_MIX, D_MODEL), EVEN_MIX, BETA)
    ffn_w_gate = nrm(ks[9], (N_EVEN, D_MODEL, D_FF), D_MODEL)
    ffn_w_up = nrm(ks[10], (N_EVEN, D_MODEL, D_FF), D_MODEL)
    ffn_w_down = nrm(ks[11], (N_EVEN, D_FF, D_MODEL), D_FF, BETA)
    odd_w_in = nrm(ks[12], (N_ODD, D_MODEL, ODD_IN), D_MODEL)
    conv_w = nrm(ks[13], (N_ODD, CONV_K, CONV_WIDTH), CONV_K)
    odd_w_out = nrm(ks[14], (N_ODD, ODD_MIX, D_MODEL), ODD_MIX, BETA)
    router_w = nrm(ks[15], (N_ODD, D_MODEL, N_EXPERTS), D_MODEL)
    moe_w_gate = nrm(ks[16], (N_ODD, N_EXPERTS, D_MODEL, D_FF_EXPERT), D_MODEL)
    moe_w_up = nrm(ks[17], (N_ODD, N_EXPERTS, D_MODEL, D_FF_EXPERT), D_MODEL)
    moe_w_down = nrm(ks[18], (N_ODD, N_EXPERTS, D_FF_EXPERT, D_MODEL), D_FF_EXPERT, BETA)
    return {'x': x, 'positions': positions, 'ln_w': ln_w, 'ln_b': ln_b,
            'even_w_in': even_w_in, 'pool_w': pool_w, 'pool_scale': pool_scale, 'swa_sinks': swa_sinks,
            'even_w_out': even_w_out, 'ffn_w_gate': ffn_w_gate, 'ffn_w_up': ffn_w_up, 'ffn_w_down': ffn_w_down,
            'odd_w_in': odd_w_in, 'conv_w': conv_w, 'odd_w_out': odd_w_out, 'router_w': router_w,
            'moe_w_gate': moe_w_gate, 'moe_w_up': moe_w_up, 'moe_w_down': moe_w_down}


def reference(x, positions, ln_w, ln_b, even_w_in, pool_w, pool_scale, swa_sinks, even_w_out,
              ffn_w_gate, ffn_w_up, ffn_w_down, odd_w_in, conv_w, odd_w_out, router_w,
              moe_w_gate, moe_w_up, moe_w_down):
    for layer in range(DEPTH):
        i = layer // 2
        if layer % 2 == 0:
            mix = even_mixer(x, positions, even_w_in[i], pool_w[i], pool_scale[i], swa_sinks[i], even_w_out[i])
        else:
            mix = odd_mixer(x, positions, odd_w_in[i], conv_w[i], odd_w_out[i])
        x = layer_norm(ALPHA * x + mix, ln_w[layer, 0], ln_b[layer, 0])
        if layer % 2 == 0:
            ffn = swiglu(x, ffn_w_gate[i], ffn_w_up[i], ffn_w_down[i])
        else:
            ffn = moe_swiglu(x, router_w[i], moe_w_gate[i], moe_w_up[i], moe_w_down[i])
        x = layer_norm(ALPHA * x + ffn, ln_w[layer, 1], ln_b[layer, 1])
    return x
```

```python
import functools
import math

import jax
import jax.numpy as jnp
from jax import lax
from jax.experimental import pallas as pl
from jax.experimental.pallas import tpu as pltpu

F32 = jnp.float32
BF16 = jnp.bfloat16
I32 = jnp.int32

HEAD_DIM = 64
ROPE_THETA = 10000.0
ATT_BLOCK = 128
LN_EPS = 1e-5
POOL_WINDOWS = (2, 4, 8, 16)
POOL_GROUP = 64
POOL_WIDTH = 256
POOL_HALO = 16
SWA_WINDOW = 128
SWA_Q_HEADS = 12
SWA_KV_HEADS = 4
SWA_Q_DIM = SWA_Q_HEADS * HEAD_DIM
SWA_KV_DIM = SWA_KV_HEADS * HEAD_DIM
DIL_PAIRS = ((128, 1), (512, 4), (2048, 16))
DIL_Q_HEADS = 8
DIL_KV_HEADS = 2
DIL_Q_DIM = DIL_Q_HEADS * HEAD_DIM
DIL_KV_DIM = DIL_KV_HEADS * HEAD_DIM
DIL_GROUP_IN = DIL_Q_DIM + 2 * DIL_KV_DIM
DIL_Q_PER_KV = DIL_Q_HEADS // DIL_KV_HEADS
CONV_WIDTH = 512
CONV_K = 3
CONV_HALO = 8
N_EXPERTS = 8
DEPTH = 2
ALPHA = (2 * DEPTH) ** 0.25
QK_SCALE = 1.0 / math.sqrt(HEAD_DIM)
NEG_BIG = -1e30

LANES = 128
VMEM_LIMIT = 48 * 1024 * 1024


def _params(sem, vmem=VMEM_LIMIT):
    return pltpu.CompilerParams(dimension_semantics=sem, vmem_limit_bytes=vmem)


def _layer_norm(y, w, b):
    mu = jnp.mean(y, axis=-1, keepdims=True)
    yc = y - mu
    var = jnp.mean(yc * yc, axis=-1, keepdims=True)
    return yc * lax.rsqrt(var + LN_EPS) * w + b


def _rope_chunk(xc, cos, sin_signed, first_half):
    rot = jnp.where(first_half, pltpu.roll(xc, 96, 1), pltpu.roll(xc, 32, 1))
    return xc * cos + rot * sin_signed


def _rope(x, cos, sin_signed):
    tm, c = x.shape
    lane = lax.broadcasted_iota(I32, (tm, LANES), 1)
    first_half = (lane & 32) == 0
    chunks = [_rope_chunk(x[:, i * LANES:(i + 1) * LANES], cos, sin_signed, first_half)
              for i in range(c // LANES)]
    return chunks[0] if len(chunks) == 1 else jnp.concatenate(chunks, axis=1)


def _trig_kernel(pos_ref, inv_ref, cos_ref, sin_ref):
    pos = pos_ref[...].astype(F32)
    ang = inv_ref[...] * pos
    c = jnp.cos(ang)
    s = jnp.sin(ang)
    c4 = jnp.concatenate([c, c, c, c], axis=0)
    s4 = jnp.concatenate([-s, s, -s, s], axis=0)
    cos_ref[...] = c4.T
    sin_ref[...] = s4.T


def _rope_tables(positions, tm=512):
    t = positions.size
    half = HEAD_DIM // 2
    inv = ROPE_THETA ** (-jnp.arange(half, dtype=F32) / half)
    return pl.pallas_call(
        _trig_kernel,
        grid=(t // tm,),
        in_specs=[pl.BlockSpec((1, tm), lambda i: (0, i)),
                  pl.BlockSpec((half, 1), lambda i: (0, 0))],
        out_specs=[pl.BlockSpec((tm, LANES), lambda i: (i, 0)),
                   pl.BlockSpec((tm, LANES), lambda i: (i, 0))],
        out_shape=[jax.ShapeDtypeStruct((t, LANES), F32)] * 2,
        compiler_params=_params(("parallel",)),
        name="rope_tables",
    )(positions.reshape(1, t), inv.reshape(half, 1))


def _inproj0_kernel(x_ref, w_ref, cos_ref, sin_ref, u_ref, q_ref, k_ref, v_ref):
    xb = x_ref[...].astype(BF16)
    cos = cos_ref[...]
    sin = sin_ref[...]
    q0 = POOL_WIDTH
    k0 = q0 + SWA_Q_DIM
    v0 = k0 + SWA_KV_DIM
    u_ref[...] = jnp.dot(xb, w_ref[:, :q0], preferred_element_type=F32)
    q = jnp.dot(xb, w_ref[:, q0:k0], preferred_element_type=F32)
    q_ref[...] = (_rope(q, cos, sin) * QK_SCALE).astype(BF16)
    k = jnp.dot(xb, w_ref[:, k0:v0], preferred_element_type=F32)
    k_ref[...] = _rope(k, cos, sin).astype(BF16)
    v_ref[...] = jnp.dot(xb, w_ref[:, v0:], preferred_element_type=F32).astype(BF16)


def _inproj0(x2d, w_bf, cos_t, sin_t, tm=512):
    t, d = x2d.shape
    n_in = w_bf.shape[1]
    row = lambda i: (i, 0)
    return pl.pallas_call(
        _inproj0_kernel,
        grid=(t // tm,),
        in_specs=[pl.BlockSpec((tm, d), row),
                  pl.BlockSpec((d, n_in), lambda i: (0, 0)),
                  pl.BlockSpec((tm, LANES), row),
                  pl.BlockSpec((tm, LANES), row)],
        out_specs=[pl.BlockSpec((tm, POOL_WIDTH), row),
                   pl.BlockSpec((tm, SWA_Q_DIM), row),
                   pl.BlockSpec((tm, SWA_KV_DIM), row),
                   pl.BlockSpec((tm, SWA_KV_DIM), row)],
        out_shape=[jax.ShapeDtypeStruct((t, POOL_WIDTH), F32),
                   jax.ShapeDtypeStruct((t, SWA_Q_DIM), BF16),
                   jax.ShapeDtypeStruct((t, SWA_KV_DIM), BF16),
                   jax.ShapeDtypeStruct((t, SWA_KV_DIM), BF16)],
        compiler_params=_params(("parallel",)),
        name="inproj0",
    )(x2d, w_bf, cos_t, sin_t)


def _band_mask(max_dist, key_lo):
    qi = lax.broadcasted_iota(I32, (ATT_BLOCK, 2 * ATT_BLOCK), 0)
    sj = lax.broadcasted_iota(I32, (ATT_BLOCK, 2 * ATT_BLOCK), 1)
    dist = ATT_BLOCK + qi - sj
    return (dist >= 0) & (dist <= max_dist) & (sj >= key_lo)


def _attn_block(q, kwin, vwin, valid, sink=None):
    s = lax.dot_general(q, kwin, (((1,), (1,)), ((), ())), preferred_element_type=F32)
    s = jnp.where(valid, s, NEG_BIG)
    m = jnp.max(s, axis=1, keepdims=True)
    if sink is not None:
        m = jnp.maximum(m, sink)
    p = jnp.exp(s - m)
    den = jnp.sum(p, axis=1, keepdims=True)
    if sink is not None:
        den = den + jnp.exp(sink - m)
    o = jnp.dot(p.astype(BF16), vwin, preferred_element_type=F32)
    return o * (1.0 / den), m + jnp.log(den)


def _swa_kernel(sink_ref, q_ref, kp_ref, kc_ref, vp_ref, vc_ref, o_ref, *, tq):
    i = pl.program_id(1)
    kfull = jnp.concatenate([kp_ref[...], kc_ref[...]], axis=0)
    vfull = jnp.concatenate([vp_ref[...], vc_ref[...]], axis=0)
    g = SWA_Q_HEADS // SWA_KV_HEADS
    for j in range(tq // ATT_BLOCK):
        key_lo = jnp.where(i == 0, ATT_BLOCK, 0) if j == 0 else 0
        valid = _band_mask(SWA_WINDOW - 1, key_lo)
        r0 = j * ATT_BLOCK
        outs = []
        for h in range(SWA_Q_HEADS):
            kv = h // g
            q = q_ref[r0:r0 + ATT_BLOCK, h * HEAD_DIM:(h + 1) * HEAD_DIM]
            kwin = kfull[r0:r0 + 2 * ATT_BLOCK, kv * HEAD_DIM:(kv + 1) * HEAD_DIM]
            vwin = vfull[r0:r0 + 2 * ATT_BLOCK, kv * HEAD_DIM:(kv + 1) * HEAD_DIM]
            o, _ = _attn_block(q, kwin, vwin, valid, sink_ref[h])
            outs.append(o)
        o_ref[r0:r0 + ATT_BLOCK, :] = jnp.concatenate(outs, axis=1).astype(BF16)


def _swa(q, k, v, sinks, b, s, tq=256):
    per = tq // ATT_BLOCK
    cur = lambda bi, i: (bi, i, 0)
    prev = lambda bi, i: (bi, jnp.maximum(i * per - 1, 0), 0)
    q3 = q.reshape(b, s, SWA_Q_DIM)
    k3 = k.reshape(b, s, SWA_KV_DIM)
    v3 = v.reshape(b, s, SWA_KV_DIM)
    out = pl.pallas_call(
        functools.partial(_swa_kernel, tq=tq),
        grid=(b, s // tq),
        in_specs=[pl.BlockSpec(memory_space=pltpu.SMEM),
                  pl.BlockSpec((None, tq, SWA_Q_DIM), cur),
                  pl.BlockSpec((None, ATT_BLOCK, SWA_KV_DIM), prev),
                  pl.BlockSpec((None, tq, SWA_KV_DIM), cur),
                  pl.BlockSpec((None, ATT_BLOCK, SWA_KV_DIM), prev),
                  pl.BlockSpec((None, tq, SWA_KV_DIM), cur)],
        out_specs=pl.BlockSpec((None, tq, SWA_Q_DIM), cur),
        out_shape=jax.ShapeDtypeStruct((b, s, SWA_Q_DIM), BF16),
        compiler_params=_params(("parallel", "parallel")),
        name="swa_attention",
    )(sinks, q3, k3, k3, v3, v3)
    return out.reshape(b * s, SWA_Q_DIM)


def _pool_mixer(u, halo, seq_row0):
    tm = u.shape[0]
    full = jnp.concatenate([halo, u], axis=0)
    sums = [full]
    for shift in (1, 2, 4, 8):
        prev = sums[-1]
        sums.append(prev + pltpu.roll(prev, shift, 0))
    lane = lax.broadcasted_iota(I32, (tm, POOL_WIDTH), 1)
    row = lax.broadcasted_iota(I32, (tm, POOL_WIDTH), 0)
    grp = lane // POOL_GROUP
    win = sums[4][POOL_HALO:]
    width = jnp.full((tm, POOL_WIDTH), POOL_WINDOWS[3], I32)
    for gi in (2, 1, 0):
        win = jnp.where(grp == gi, sums[gi + 1][POOL_HALO:], win)
        width = jnp.where(grp == gi, POOL_WINDOWS[gi], width)
    count = jnp.minimum(seq_row0 + row + 1, width).astype(F32)
    return win / count - u


def _outproj0_kernel(u_ref, uh_ref, o_ref, x_ref, pw_ref, ps_ref, wa_ref, wb_ref, lw_ref, lb_ref, out_ref, *, tm):
    i = pl.program_id(1)
    halo = jnp.where(i == 0, 0.0, uh_ref[...])
    d = _pool_mixer(u_ref[...], halo, i * tm)
    a = jnp.dot(d.astype(BF16), pw_ref[...], preferred_element_type=F32) * ps_ref[...]
    mix = jnp.dot(a.astype(BF16), wa_ref[...], preferred_element_type=F32)
    mix = mix + jnp.dot(o_ref[...], wb_ref[...], preferred_element_type=F32)
    y = ALPHA * x_ref[...] + mix
    out_ref[...] = _layer_norm(y, lw_ref[...], lb_ref[...])


def _outproj0(u, attn, x2d, pool_bd, pool_scale, wa, wb, ln_w, ln_b, b, s, tm=512):
    d = x2d.shape[1]
    per = tm // POOL_HALO
    cur = lambda bi, i: (bi, i, 0)
    prev = lambda bi, i: (bi, jnp.maximum(i * per - 1, 0), 0)
    const = lambda bi, i: (0, 0)
    out = pl.pallas_call(
        functools.partial(_outproj0_kernel, tm=tm),
        grid=(b, s // tm),
        in_specs=[pl.BlockSpec((None, tm, POOL_WIDTH), cur),
                  pl.BlockSpec((None, POOL_HALO, POOL_WIDTH), prev),
                  pl.BlockSpec((None, tm, SWA_Q_DIM), cur),
                  pl.BlockSpec((None, tm, d), cur),
                  pl.BlockSpec((POOL_WIDTH, POOL_WIDTH), const),
                  pl.BlockSpec((1, POOL_WIDTH), const),
                  pl.BlockSpec((POOL_WIDTH, d), const),
                  pl.BlockSpec((SWA_Q_DIM, d), const),
                  pl.BlockSpec((1, d), const),
                  pl.BlockSpec((1, d), const)],
        out_specs=pl.BlockSpec((None, tm, d), cur),
        out_shape=jax.ShapeDtypeStruct((b, s, d), F32),
        compiler_params=_params(("parallel", "parallel")),
        name="outproj0",
    )(u.reshape(b, s, POOL_WIDTH), u.reshape(b, s, POOL_WIDTH), attn.reshape(b, s, SWA_Q_DIM),
      x2d.reshape(b, s, d), pool_bd, pool_scale, wa, wb, ln_w, ln_b)
    return out.reshape(b * s, d)


def _swiglu_tile(xb, wg, wu, wd):
    g = jnp.dot(xb, wg, preferred_element_type=F32)
    u = jnp.dot(xb, wu, preferred_element_type=F32)
    h = (g * jax.nn.sigmoid(g)) * u
    return jnp.dot(h.astype(BF16), wd, preferred_element_type=F32)


def _ffn_kernel(x_ref, wg_ref, wu_ref, wd_ref, lw_ref, lb_ref, out_ref, acc_ref):
    j = pl.program_id(1)
    part = _swiglu_tile(x_ref[...].astype(BF16), wg_ref[...], wu_ref[...], wd_ref[...])

    @pl.when(j == 0)
    def _():
        acc_ref[...] = part

    @pl.when(j > 0)
    def _():
        acc_ref[...] += part

    @pl.when(j == pl.num_programs(1) - 1)
    def _():
        y = ALPHA * x_ref[...] + acc_ref[...]
        out_ref[...] = _layer_norm(y, lw_ref[...], lb_ref[...])


def _ffn_dense(x2d, wg, wu, wd, ln_w, ln_b, tm=512, tf=1408):
    t, d = x2d.shape
    f = wg.shape[1]
    return pl.pallas_call(
        _ffn_kernel,
        grid=(t // tm, f // tf),
        in_specs=[pl.BlockSpec((tm, d), lambda i, j: (i, 0)),
                  pl.BlockSpec((d, tf), lambda i, j: (0, j)),
                  pl.BlockSpec((d, tf), lambda i, j: (0, j)),
                  pl.BlockSpec((tf, d), lambda i, j: (j, 0)),
                  pl.BlockSpec((1, d), lambda i, j: (0, 0)),
                  pl.BlockSpec((1, d), lambda i, j: (0, 0))],
        out_specs=pl.BlockSpec((tm, d), lambda i, j: (i, 0)),
        out_shape=jax.ShapeDtypeStruct((t, d), F32),
        scratch_shapes=[pltpu.VMEM((tm, d), F32)],
        compiler_params=_params(("parallel", "arbitrary")),
        name="ffn_dense",
    )(x2d, wg, wu, wd, ln_w, ln_b)


def _inproj1_kernel(x_ref, wq_ref, wkv_ref, wc_ref, cw_ref, cos_ref, sin_ref,
                    q0_ref, q1_ref, q2_ref, kv0_ref, kv1_ref, kv2_ref, d_ref,
                    zc_ref, sq_ref, skv_ref, *, tm):
    i = pl.program_id(1)
    xb = x_ref[...].astype(BF16)
    cos = cos_ref[...]
    sin = sin_ref[...]
    lane = lax.broadcasted_iota(I32, (tm, LANES), 1)
    cos_k = jnp.where(lane < HEAD_DIM, cos, 1.0)
    sin_k = jnp.where(lane < HEAD_DIM, sin, 0.0)
    q_refs = (q0_ref, q1_ref, q2_ref)
    kv_refs = (kv0_ref, kv1_ref, kv2_ref)
    for gi, (_, dil) in enumerate(DIL_PAIRS):
        q = jnp.dot(xb, wq_ref[gi], preferred_element_type=F32)
        q = _rope(q, cos, sin) * QK_SCALE
        kv = jnp.dot(xb, wkv_ref[gi], preferred_element_type=F32)
        kv = _rope(kv, cos_k, sin_k)
        if dil == 1:
            q_refs[gi][0] = q.astype(BF16)
            for h in range(DIL_KV_HEADS):
                kv_refs[gi][h, 0] = kv[:, h * LANES:(h + 1) * LANES].astype(BF16)
        else:
            n = tm // dil
            for c in range(DIL_Q_DIM // LANES):
                sq_ref[c] = q[:, c * LANES:(c + 1) * LANES]
            for h in range(DIL_KV_HEADS):
                skv_ref[h] = kv[:, h * LANES:(h + 1) * LANES]
            for r in range(dil):
                rows = pl.ds(r, n, stride=dil)
                q_refs[gi][r] = jnp.concatenate(
                    [sq_ref[c, rows, :] for c in range(DIL_Q_DIM // LANES)], axis=1).astype(BF16)
                for h in range(DIL_KV_HEADS):
                    kv_refs[gi][h, r] = skv_ref[h, rows, :].astype(BF16)

    hc = jnp.dot(xb, wc_ref[...], preferred_element_type=F32)
    z = hc[:, 2 * CONV_WIDTH:] * hc[:, :CONV_WIDTH]
    zprev = jnp.where(i == 0, 0.0, zc_ref[...])
    zfull = jnp.concatenate([zprev, z], axis=0)
    z1 = pltpu.roll(zfull, 1, 0)[CONV_HALO:]
    z2 = pltpu.roll(zfull, 2, 0)[CONV_HALO:]
    cw = cw_ref[...]
    y = cw[0:1] * z2 + cw[1:2] * z1 + cw[2:3] * z
    d_ref[...] = (hc[:, CONV_WIDTH:2 * CONV_WIDTH] * y).astype(BF16)
    zc_ref[...] = z[tm - CONV_HALO:]


def _inproj1(x2d, wq, wkv, wc, conv_w, cos_t, sin_t, b, s, tm=512):
    d = x2d.shape[1]
    cur = lambda bi, i: (bi, i, 0)
    row = lambda bi, i: (bi * (s // tm) + i, 0)
    in_specs = [pl.BlockSpec((None, tm, d), cur),
                pl.BlockSpec(wq.shape, lambda bi, i: (0, 0, 0)),
                pl.BlockSpec(wkv.shape, lambda bi, i: (0, 0, 0)),
                pl.BlockSpec(wc.shape, lambda bi, i: (0, 0)),
                pl.BlockSpec(conv_w.shape, lambda bi, i: (0, 0)),
                pl.BlockSpec((tm, LANES), row),
                pl.BlockSpec((tm, LANES), row)]
    out_specs, out_shape = [], []
    for _, dil in DIL_PAIRS:
        out_specs.append(pl.BlockSpec((None, dil, tm // dil, DIL_Q_DIM), lambda bi, i: (bi, 0, i, 0)))
        out_shape.append(jax.ShapeDtypeStruct((b, dil, s // dil, DIL_Q_DIM), BF16))
    for _, dil in DIL_PAIRS:
        out_specs.append(pl.BlockSpec((None, DIL_KV_HEADS, dil, tm // dil, LANES), lambda bi, i: (bi, 0, 0, i, 0)))
        out_shape.append(jax.ShapeDtypeStruct((b, DIL_KV_HEADS, dil, s // dil, LANES), BF16))
    out_specs.append(pl.BlockSpec((None, tm, CONV_WIDTH), cur))
    out_shape.append(jax.ShapeDtypeStruct((b, s, CONV_WIDTH), BF16))
    return pl.pallas_call(
        functools.partial(_inproj1_kernel, tm=tm),
        grid=(b, s // tm),
        in_specs=in_specs,
        out_specs=out_specs,
        out_shape=out_shape,
        scratch_shapes=[pltpu.VMEM((CONV_HALO, CONV_WIDTH), F32),
                        pltpu.VMEM((DIL_Q_DIM // LANES, tm, LANES), F32),
                        pltpu.VMEM((DIL_KV_HEADS, tm, LANES), F32)],
        compiler_params=_params(("parallel", "arbitrary")),
        name="inproj1",
    )(x2d.reshape(b, s, d), wq, wkv, wc, conv_w, cos_t, sin_t)


def _dil_kernel(q0_ref, q1_ref, q2_ref, kv0_ref, kv1_ref, kv2_ref, out_ref, o_run, l_run, *, s):
    q_refs = (q0_ref, q1_ref, q2_ref)
    kv_refs = (kv0_ref, kv1_ref, kv2_ref)
    n_blocks = s // ATT_BLOCK
    width = DIL_Q_PER_KV * HEAD_DIM
    for gi, (window, dil) in enumerate(DIL_PAIRS):
        q_ref, kv_ref = q_refs[gi], kv_refs[gi]
        blocks_per_sub = (s // dil) // ATT_BLOCK
        shift = blocks_per_sub.bit_length() - 1

        def body(n, carry, q_ref=q_ref, kv_ref=kv_ref, gi=gi, window=window, dil=dil,
                 blocks_per_sub=blocks_per_sub, shift=shift):
            res = lax.shift_right_logical(n, shift)
            bi = n & (blocks_per_sub - 1)
            rows = pl.ds(pl.multiple_of(n * ATT_BLOCK, ATT_BLOCK), ATT_BLOCK)
            prow = pl.ds(pl.multiple_of(jnp.maximum(n - 1, 0) * ATT_BLOCK, ATT_BLOCK), ATT_BLOCK)
            kv_cur = kv_ref[rows, :]
            kv_prev = kv_ref[prow, :]
            kwin = jnp.concatenate([kv_prev[:, :HEAD_DIM], kv_cur[:, :HEAD_DIM]], axis=0)
            vwin = jnp.concatenate([kv_prev[:, HEAD_DIM:], kv_cur[:, HEAD_DIM:]], axis=0)
            valid = _band_mask(window // dil, jnp.where(bi == 0, ATT_BLOCK, 0))
            q4 = q_ref[rows, :]
            outs, lses = [], []
            for h in range(DIL_Q_PER_KV):
                o, lse = _attn_block(q4[:, h * HEAD_DIM:(h + 1) * HEAD_DIM], kwin, vwin, valid)
                outs.append(o)
                lses.append(jnp.broadcast_to(lse, (ATT_BLOCK, HEAD_DIM)))
            o4 = jnp.concatenate(outs, axis=1)
            l4 = jnp.concatenate(lses, axis=1)
            start = res + dil * ATT_BLOCK * bi
            if dil == 1:
                tok = pl.ds(pl.multiple_of(start, ATT_BLOCK), ATT_BLOCK)
            else:
                tok = pl.ds(start, ATT_BLOCK, stride=dil)
            for c in range(width // LANES):
                o_c = o4[:, c * LANES:(c + 1) * LANES]
                l_c = l4[:, c * LANES:(c + 1) * LANES]
                if gi == 0:
                    o_run[c, tok, :] = o_c
                    l_run[c, tok, :] = l_c
                else:
                    o_old = o_run[c, tok, :]
                    l_old = l_run[c, tok, :]
                    m = jnp.maximum(l_old, l_c)
                    a = jnp.exp(l_old - m)
                    bb = jnp.exp(l_c - m)
                    tot = a + bb
                    o_run[c, tok, :] = (o_old * a + o_c * bb) * (1.0 / tot)
                    if gi < len(DIL_PAIRS) - 1:
                        l_run[c, tok, :] = m + jnp.log(tot)
            return carry

        lax.fori_loop(0, n_blocks, body, 0)
    for c in range(width // LANES):
        out_ref[:, c * LANES:(c + 1) * LANES] = o_run[c].astype(BF16)


def _dilated(qs, kvs, b, s):
    width = DIL_Q_PER_KV * HEAD_DIM
    in_specs = [pl.BlockSpec((None, s, width), lambda bi, h: (bi, 0, h)) for _ in DIL_PAIRS]
    in_specs += [pl.BlockSpec((None, None, s, LANES), lambda bi, h: (bi, h, 0, 0)) for _ in DIL_PAIRS]
    qs = [q.reshape(b, s, DIL_Q_DIM) for q in qs]
    kvs = [kv.reshape(b, DIL_KV_HEADS, s, LANES) for kv in kvs]
    out = pl.pallas_call(
        functools.partial(_dil_kernel, s=s),
        grid=(b, DIL_KV_HEADS),
        in_specs=in_specs,
        out_specs=pl.BlockSpec((None, s, width), lambda bi, h: (bi, 0, h)),
        out_shape=jax.ShapeDtypeStruct((b, s, DIL_Q_DIM), BF16),
        scratch_shapes=[pltpu.VMEM((width // LANES, s, LANES), F32),
                        pltpu.VMEM((width // LANES, s, LANES), F32)],
        compiler_params=_params(("parallel", "parallel")),
        name="dilated_attention",
    )(*qs, *kvs)
    return out.reshape(b * s, DIL_Q_DIM)


def _to_slabs(ref, val):
    rows, d = val.shape
    per = d // LANES
    for c in range(per):
        ref[pl.ds(c, rows, stride=per), :] = val[:, c * LANES:(c + 1) * LANES]


def _from_slabs(ref, rows):
    per = ref.shape[0] // rows
    return jnp.concatenate([ref[pl.ds(c, rows, stride=per), :] for c in range(per)], axis=1)


def _outproj1_kernel(c_ref, d_ref, x_ref, wa_ref, wb_ref, lw_ref, lb_ref, out_ref, slab_ref):
    mix = jnp.dot(c_ref[...], wa_ref[...], preferred_element_type=F32)
    mix = mix + jnp.dot(d_ref[...], wb_ref[...], preferred_element_type=F32)
    y = _layer_norm(ALPHA * x_ref[...] + mix, lw_ref[...], lb_ref[...])
    out_ref[...] = y
    _to_slabs(slab_ref, y)


def _outproj1(c, dconv, x2d, wa, wb, ln_w, ln_b, tm=512):
    t, d = x2d.shape
    row = lambda i: (i, 0)
    const = lambda i: (0, 0)
    return pl.pallas_call(
        _outproj1_kernel,
        grid=(t // tm,),
        in_specs=[pl.BlockSpec((tm, DIL_Q_DIM), row),
                  pl.BlockSpec((tm, CONV_WIDTH), row),
                  pl.BlockSpec((tm, d), row),
                  pl.BlockSpec((DIL_Q_DIM, d), const),
                  pl.BlockSpec((CONV_WIDTH, d), const),
                  pl.BlockSpec((1, d), const),
                  pl.BlockSpec((1, d), const)],
        out_specs=[pl.BlockSpec((tm, d), row), pl.BlockSpec((tm * (d // LANES), LANES), row)],
        out_shape=[jax.ShapeDtypeStruct((t, d), F32), jax.ShapeDtypeStruct((t * (d // LANES), LANES), F32)],
        compiler_params=_params(("parallel",)),
        name="outproj1",
    )(c, dconv, x2d, wa, wb, ln_w, ln_b)


def _route_kernel(x_ref, wr_ref, e_ref, g_ref, r_ref, cnt_ref, run_ref, *, tm):
    i = pl.program_id(0)

    @pl.when(i == 0)
    def _():
        run_ref[...] = jnp.zeros_like(run_ref)

    logits = lax.dot_general(wr_ref[...], x_ref[...], (((1,), (1,)), ((), ())),
                             precision=lax.Precision.HIGHEST, preferred_element_type=F32)
    eid = lax.broadcasted_iota(I32, (N_EXPERTS, tm), 0)
    m1 = jnp.max(logits, axis=0, keepdims=True)
    i1 = jnp.min(jnp.where(logits == m1, eid, N_EXPERTS), axis=0, keepdims=True)
    rest = jnp.where(eid == i1, -jnp.inf, logits)
    m2 = jnp.max(rest, axis=0, keepdims=True)
    i2 = jnp.min(jnp.where(rest == m2, eid, N_EXPERTS), axis=0, keepdims=True)
    t2 = jnp.exp(m2 - m1)
    g1 = 1.0 / (1.0 + t2)
    g2 = t2 / (1.0 + t2)
    oh1 = eid == i1
    oh2 = eid == i2
    oh = jnp.where(oh1 | oh2, 1.0, 0.0)
    ri = lax.broadcasted_iota(I32, (tm, tm), 0)
    ci = lax.broadcasted_iota(I32, (tm, tm), 1)
    tri = jnp.where(ri <= ci, 1.0, 0.0).astype(BF16)
    incl = jnp.dot(oh.astype(BF16), tri, preferred_element_type=F32)
    rank = run_ref[:, 0:1] + incl - oh
    r1 = jnp.sum(jnp.where(oh1, rank, 0.0), axis=0, keepdims=True)
    r2 = jnp.sum(jnp.where(oh2, rank, 0.0), axis=0, keepdims=True)
    e_ref[0:1, :] = i1
    e_ref[1:2, :] = i2
    g_ref[0:1, :] = g1
    g_ref[1:2, :] = g2
    r_ref[0:1, :] = r1.astype(I32)
    r_ref[1:2, :] = r2.astype(I32)
    run_ref[...] = run_ref[...] + incl[:, tm - 1:tm]
    cnt_ref[...] = run_ref[...]


def _route(x2d, wr_t, tm=512):
    t, d = x2d.shape
    col = lambda i: (0, i)
    return pl.pallas_call(
        functools.partial(_route_kernel, tm=tm),
        grid=(t // tm,),
        in_specs=[pl.BlockSpec((tm, d), lambda i: (i, 0)),
                  pl.BlockSpec((N_EXPERTS, d), lambda i: (0, 0))],
        out_specs=[pl.BlockSpec((2, tm), col), pl.BlockSpec((2, tm), col), pl.BlockSpec((2, tm), col),
                   pl.BlockSpec((N_EXPERTS, LANES), lambda i: (0, 0))],
        out_shape=[jax.ShapeDtypeStruct((2, t), I32), jax.ShapeDtypeStruct((2, t), F32),
                   jax.ShapeDtypeStruct((2, t), I32), jax.ShapeDtypeStruct((N_EXPERTS, LANES), F32)],
        scratch_shapes=[pltpu.VMEM((N_EXPERTS, LANES), F32)],
        compiler_params=_params(("arbitrary",)),
        name="moe_route",
    )(x2d, wr_t)


ROW_SLAB = 8


def _slab_copy(src_ref, dst_ref, sem, src_row, dst_row, n=1):
    src = src_ref.at[pl.ds(pl.multiple_of(src_row * ROW_SLAB, ROW_SLAB), n * ROW_SLAB)]
    dst = dst_ref.at[pl.ds(pl.multiple_of(dst_row * ROW_SLAB, ROW_SLAB), n * ROW_SLAB)]
    return pltpu.make_async_copy(src, dst, sem)


def _gather_kernel(tok_ref, x_ref, out_ref, sem, *, rt):
    base = pl.program_id(0) * rt

    def start(r, c):
        _slab_copy(x_ref, out_ref, sem, tok_ref[0, 0, r], base + r).start()
        return c

    lax.fori_loop(0, rt, start, 0)
    _slab_copy(x_ref, out_ref, sem, 0, base, n=rt).wait()


def _gather_rows(x_slabs, row_tok, rt=512):
    n_rows = row_tok.shape[0]
    return pl.pallas_call(
        functools.partial(_gather_kernel, rt=rt),
        grid=(n_rows // rt,),
        in_specs=[pl.BlockSpec((1, 1, rt), lambda i: (i, 0, 0), memory_space=pltpu.SMEM),
                  pl.BlockSpec(memory_space=pl.ANY)],
        out_specs=pl.BlockSpec(memory_space=pl.ANY),
        out_shape=jax.ShapeDtypeStruct((n_rows * ROW_SLAB, LANES), x_slabs.dtype),
        scratch_shapes=[pltpu.SemaphoreType.DMA(())],
        compiler_params=_params(("arbitrary",)),
        name="moe_gather",
    )(row_tok.reshape(n_rows // rt, 1, rt), x_slabs)


def _moe_kernel(be_ref, nv_ref, x_ref, wg_ref, wu_ref, wd_ref, out_ref, xb_ref, acc_ref, *, tm):
    i = pl.program_id(0)
    j = pl.program_id(1)

    @pl.when(i < nv_ref[0])
    def _():
        @pl.when(j == 0)
        def _():
            xb_ref[...] = _from_slabs(x_ref, tm).astype(BF16)

        part = _swiglu_tile(xb_ref[...], wg_ref[...], wu_ref[...], wd_ref[...])

        @pl.when(j == 0)
        def _():
            acc_ref[...] = part

        @pl.when(j > 0)
        def _():
            acc_ref[...] += part

        @pl.when(j == pl.num_programs(1) - 1)
        def _():
            _to_slabs(out_ref, acc_ref[...])

    @pl.when((i >= nv_ref[0]) & (j == pl.num_programs(1) - 1))
    def _():
        out_ref[...] = jnp.zeros_like(out_ref)


def _moe_experts(xs, block_e, n_valid, wg, wu, wd, tm, tf=896):
    d = wg.shape[1]
    n_rows = xs.shape[0] // ROW_SLAB
    f = wg.shape[2]
    nj = f // tf

    def live(i, j, be, nv):
        ok = i < nv[0]
        return jnp.where(ok, i, nv[0] - 1), jnp.where(ok, j, nj - 1)

    def x_map(i, j, be, nv):
        return live(i, j, be, nv)[0], 0

    def w_map(i, j, be, nv):
        ii, jj = live(i, j, be, nv)
        return be[ii], 0, jj

    def wd_map(i, j, be, nv):
        ii, jj = live(i, j, be, nv)
        return be[ii], jj, 0

    grid_spec = pltpu.PrefetchScalarGridSpec(
        num_scalar_prefetch=2,
        grid=(n_rows // tm, nj),
        in_specs=[pl.BlockSpec((tm * ROW_SLAB, LANES), x_map),
                  pl.BlockSpec((None, d, tf), w_map),
                  pl.BlockSpec((None, d, tf), w_map),
                  pl.BlockSpec((None, tf, d), wd_map)],
        out_specs=pl.BlockSpec((tm * ROW_SLAB, LANES), lambda i, j, be, nv: (i, 0)),
        scratch_shapes=[pltpu.VMEM((tm, d), BF16), pltpu.VMEM((tm, d), F32)],
    )
    return pl.pallas_call(
        functools.partial(_moe_kernel, tm=tm),
        grid_spec=grid_spec,
        out_shape=jax.ShapeDtypeStruct((n_rows * ROW_SLAB, LANES), F32),
        compiler_params=_params(("arbitrary", "arbitrary")),
        name="moe_experts",
    )(block_e, n_valid, xs, wg, wu, wd)


def _combine_kernel(d1_ref, d2_ref, y_ref, x_ref, g1_ref, g2_ref, lw_ref, lb_ref, out_ref, ya, yb, sem, *, tm):
    def start(r, c):
        _slab_copy(y_ref, ya, sem.at[0], d1_ref[0, 0, r], r).start()
        _slab_copy(y_ref, yb, sem.at[1], d2_ref[0, 0, r], r).start()
        return c

    lax.fori_loop(0, tm, start, 0)
    _slab_copy(y_ref, ya, sem.at[0], 0, 0, n=tm).wait()
    _slab_copy(y_ref, yb, sem.at[1], 0, 0, n=tm).wait()
    ffn = _from_slabs(ya, tm) * g1_ref[...] + _from_slabs(yb, tm) * g2_ref[...]
    out_ref[...] = _layer_norm(ALPHA * x_ref[...] + ffn, lw_ref[...], lb_ref[...])


def _combine(y, dest1, dest2, x2d, g1, g2, ln_w, ln_b, tm=256):
    t, d = x2d.shape
    row = lambda i: (i, 0)
    smem = lambda: pl.BlockSpec((1, 1, tm), lambda i: (i, 0, 0), memory_space=pltpu.SMEM)
    return pl.pallas_call(
        functools.partial(_combine_kernel, tm=tm),
        grid=(t // tm,),
        in_specs=[smem(), smem(),
                  pl.BlockSpec(memory_space=pl.ANY),
                  pl.BlockSpec((tm, d), row),
                  pl.BlockSpec((tm, 1), row),
                  pl.BlockSpec((tm, 1), row),
                  pl.BlockSpec((1, d), lambda i: (0, 0)),
                  pl.BlockSpec((1, d), lambda i: (0, 0))],
        out_specs=pl.BlockSpec((tm, d), row),
        out_shape=jax.ShapeDtypeStruct((t, d), F32),
        scratch_shapes=[pltpu.VMEM((tm * ROW_SLAB, LANES), F32), pltpu.VMEM((tm * ROW_SLAB, LANES), F32),
                        pltpu.SemaphoreType.DMA((2,))],
        compiler_params=_params(("arbitrary",)),
        name="moe_combine",
    )(dest1.reshape(t // tm, 1, tm), dest2.reshape(t // tm, 1, tm), y, x2d,
      g1.reshape(t, 1), g2.reshape(t, 1), ln_w, ln_b)


def _moe_layer(x2d, xb, router_w, wg, wu, wd, ln_w, ln_b, tmoe=512):
    t, d = x2d.shape
    e_sel, gates, ranks, counts = _route(x2d, router_w.T)
    counts = counts[:, 0].astype(I32)
    padded = (counts + tmoe - 1) // tmoe * tmoe
    pend = jnp.cumsum(padded)
    pstart = pend - padded
    n_blocks = -(-(2 * t + N_EXPERTS * (tmoe - 1)) // tmoe)
    n_rows = n_blocks * tmoe
    dest = pstart[e_sel] + ranks
    tok = jnp.arange(t, dtype=I32)
    row_tok = jnp.zeros((n_rows,), I32).at[dest[0]].set(tok).at[dest[1]].set(tok)
    block_row0 = jnp.arange(n_blocks, dtype=I32) * tmoe
    block_e = jnp.minimum(jnp.sum((block_row0[:, None] >= pend[None, :]).astype(I32), axis=1), N_EXPERTS - 1)
    n_valid = (pend[-1:] // tmoe).astype(I32)
    xs = _gather_rows(xb, row_tok)
    y = _moe_experts(xs, block_e, n_valid, wg, wu, wd, tmoe)
    return _combine(y, dest[0], dest[1], x2d, gates[0], gates[1], ln_w, ln_b)


def kernel(x, positions, ln_w, ln_b, even_w_in, pool_w, pool_scale, swa_sinks, even_w_out, ffn_w_gate, ffn_w_up, ffn_w_down, odd_w_in, conv_w, odd_w_out, router_w, moe_w_gate, moe_w_up, moe_w_down):
    b, s, d = x.shape
    t = b * s
    x2d = x.reshape(t, d)
    cos_t, sin_t = _rope_tables(positions)
    ln = lambda layer, k: (ln_w[layer, k].reshape(1, d), ln_b[layer, k].reshape(1, d))

    u, q, k, v = _inproj0(x2d, even_w_in[0].astype(BF16), cos_t, sin_t)
    attn = _swa(q, k, v, swa_sinks[0], b, s)
    pool_bd = jax.scipy.linalg.block_diag(*[pool_w[0, gi] for gi in range(len(POOL_WINDOWS))]).astype(BF16)
    w_out0 = even_w_out[0].astype(BF16)
    x2d = _outproj0(u, attn, x2d, pool_bd, pool_scale[0].reshape(1, POOL_WIDTH),
                    w_out0[:POOL_WIDTH], w_out0[POOL_WIDTH:], *ln(0, 0), b, s)
    x2d = _ffn_dense(x2d, ffn_w_gate[0].astype(BF16), ffn_w_up[0].astype(BF16), ffn_w_down[0].astype(BF16),
                     *ln(0, 1))

    w_in1 = odd_w_in[0].astype(BF16)
    c_in = len(DIL_PAIRS) * DIL_GROUP_IN
    wq, wkv = [], []
    for gi in range(len(DIL_PAIRS)):
        g0 = gi * DIL_GROUP_IN
        wq.append(w_in1[:, g0:g0 + DIL_Q_DIM])
        kcol = g0 + DIL_Q_DIM
        vcol = kcol + DIL_KV_DIM
        parts = []
        for h in range(DIL_KV_HEADS):
            parts += [w_in1[:, kcol + h * HEAD_DIM:kcol + (h + 1) * HEAD_DIM],
                      w_in1[:, vcol + h * HEAD_DIM:vcol + (h + 1) * HEAD_DIM]]
        wkv.append(jnp.concatenate(parts, axis=1))
    outs = _inproj1(x2d, jnp.stack(wq), jnp.stack(wkv), w_in1[:, c_in:], conv_w[0], cos_t, sin_t, b, s)
    c_out = _dilated(outs[0:3], outs[3:6], b, s)
    w_out1 = odd_w_out[0].astype(BF16)
    x2d, xb = _outproj1(c_out, outs[6].reshape(t, CONV_WIDTH), x2d, w_out1[:DIL_Q_DIM], w_out1[DIL_Q_DIM:],
                        *ln(1, 0))
    x2d = _moe_layer(x2d, xb, router_w[0], moe_w_gate[0].astype(BF16), moe_w_up[0].astype(BF16),
                     moe_w_down[0].astype(BF16), *ln(1, 1))
    return x2d.reshape(b, s, d)
```

```python
import functools
import math

import jax
import jax.numpy as jnp
from jax import lax
from jax.experimental import pallas as pl
from jax.experimental.pallas import tpu as pltpu

F32 = jnp.float32
BF16 = jnp.bfloat16
I32 = jnp.int32

HEAD_DIM = 64
ROPE_THETA = 10000.0
ATT_BLOCK = 128
LN_EPS = 1e-5
POOL_WINDOWS = (2, 4, 8, 16)
POOL_GROUP = 64
POOL_WIDTH = 256
POOL_HALO = 16
SWA_WINDOW = 128
SWA_Q_HEADS = 12
SWA_KV_HEADS = 4
SWA_Q_DIM = SWA_Q_HEADS * HEAD_DIM
SWA_KV_DIM = SWA_KV_HEADS * HEAD_DIM
DIL_PAIRS = ((128, 1), (512, 4), (2048, 16))
DIL_Q_HEADS = 8
DIL_KV_HEADS = 2
DIL_Q_DIM = DIL_Q_HEADS * HEAD_DIM
DIL_KV_DIM = DIL_KV_HEADS * HEAD_DIM
DIL_GROUP_IN = DIL_Q_DIM + 2 * DIL_KV_DIM
DIL_Q_PER_KV = DIL_Q_HEADS // DIL_KV_HEADS
CONV_WIDTH = 512
CONV_K = 3
CONV_HALO = 8
N_EXPERTS = 8
DEPTH = 2
ALPHA = (2 * DEPTH) ** 0.25
QK_SCALE = 1.0 / math.sqrt(HEAD_DIM)
NEG_BIG = -1e30

LANES = 128
VMEM_LIMIT = 48 * 1024 * 1024


def _params(sem, vmem=VMEM_LIMIT):
    return pltpu.CompilerParams(dimension_semantics=sem, vmem_limit_bytes=vmem)


def _layer_norm(y, w, b):
    mu = jnp.mean(y, axis=-1, keepdims=True)
    yc = y - mu
    var = jnp.mean(yc * yc, axis=-1, keepdims=True)
    return yc * lax.rsqrt(var + LN_EPS) * w + b


def _rope_chunk(xc, cos, sin_signed, first_half):
    rot = jnp.where(first_half, pltpu.roll(xc, 96, 1), pltpu.roll(xc, 32, 1))
    return xc * cos + rot * sin_signed


def _rope(x, cos, sin_signed):
    tm, c = x.shape
    lane = lax.broadcasted_iota(I32, (tm, LANES), 1)
    first_half = (lane & 32) == 0
    chunks = [_rope_chunk(x[:, i * LANES:(i + 1) * LANES], cos, sin_signed, first_half)
              for i in range(c // LANES)]
    return chunks[0] if len(chunks) == 1 else jnp.concatenate(chunks, axis=1)


def _trig_kernel(pos_ref, inv_ref, cos_ref, sin_ref):
    pos = pos_ref[...].astype(F32)
    ang = inv_ref[...] * pos
    c = jnp.cos(ang)
    s = jnp.sin(ang)
    c4 = jnp.concatenate([c, c, c, c], axis=0)
    s4 = jnp.concatenate([-s, s, -s, s], axis=0)
    cos_ref[...] = c4.T
    sin_ref[...] = s4.T


def _rope_tables(positions, tm=512):
    t = positions.size
    half = HEAD_DIM // 2
    inv = ROPE_THETA ** (-jnp.arange(half, dtype=F32) / half)
    return pl.pallas_call(
        _trig_kernel,
        grid=(t // tm,),
        in_specs=[pl.BlockSpec((1, tm), lambda i: (0, i)),
                  pl.BlockSpec((half, 1), lambda i: (0, 0))],
        out_specs=[pl.BlockSpec((tm, LANES), lambda i: (i, 0)),
                   pl.BlockSpec((tm, LANES), lambda i: (i, 0))],
        out_shape=[jax.ShapeDtypeStruct((t, LANES), F32)] * 2,
        compiler_params=_params(("parallel",)),
        name="rope_tables",
    )(positions.reshape(1, t), inv.reshape(half, 1))


def _inproj0_kernel(x_ref, w_ref, cos_ref, sin_ref, u_ref, q_ref, k_ref, v_ref):
    xb = x_ref[...].astype(BF16)
    cos = cos_ref[...]
    sin = sin_ref[...]
    q0 = POOL_WIDTH
    k0 = q0 + SWA_Q_DIM
    v0 = k0 + SWA_KV_DIM
    u_ref[...] = jnp.dot(xb, w_ref[:, :q0], preferred_element_type=F32)
    q = jnp.dot(xb, w_ref[:, q0:k0], preferred_element_type=F32)
    q_ref[...] = (_rope(q, cos, sin) * QK_SCALE).astype(BF16)
    k = jnp.dot(xb, w_ref[:, k0:v0], preferred_element_type=F32)
    k_ref[...] = _rope(k, cos, sin).astype(BF16)
    v_ref[...] = jnp.dot(xb, w_ref[:, v0:], preferred_element_type=F32).astype(BF16)


def _inproj0(x2d, w_bf, cos_t, sin_t, tm=512):
    t, d = x2d.shape
    n_in = w_bf.shape[1]
    row = lambda i: (i, 0)
    return pl.pallas_call(
        _inproj0_kernel,
        grid=(t // tm,),
        in_specs=[pl.BlockSpec((tm, d), row),
                  pl.BlockSpec((d, n_in), lambda i: (0, 0)),
                  pl.BlockSpec((tm, LANES), row),
                  pl.BlockSpec((tm, LANES), row)],
        out_specs=[pl.BlockSpec((tm, POOL_WIDTH), row),
                   pl.BlockSpec((tm, SWA_Q_DIM), row),
                   pl.BlockSpec((tm, SWA_KV_DIM), row),
                   pl.BlockSpec((tm, SWA_KV_DIM), row)],
        out_shape=[jax.ShapeDtypeStruct((t, POOL_WIDTH), F32),
                   jax.ShapeDtypeStruct((t, SWA_Q_DIM), BF16),
                   jax.ShapeDtypeStruct((t, SWA_KV_DIM), BF16),
                   jax.ShapeDtypeStruct((t, SWA_KV_DIM), BF16)],
        compiler_params=_params(("parallel",)),
        name="inproj0",
    )(x2d, w_bf, cos_t, sin_t)


def _band_mask(max_dist, key_lo):
    qi = lax.broadcasted_iota(I32, (ATT_BLOCK, 2 * ATT_BLOCK), 0)
    sj = lax.broadcasted_iota(I32, (ATT_BLOCK, 2 * ATT_BLOCK), 1)
    dist = ATT_BLOCK + qi - sj
    return (dist >= 0) & (dist <= max_dist) & (sj >= key_lo)


def _attn_block(q, kwin, vwin, valid, sink=None):
    s = lax.dot_general(q, kwin, (((1,), (1,)), ((), ())), preferred_element_type=F32)
    s = jnp.where(valid, s, NEG_BIG)
    m = jnp.max(s, axis=1, keepdims=True)
    if sink is not None:
        m = jnp.maximum(m, sink)
    p = jnp.exp(s - m)
    den = jnp.sum(p, axis=1, keepdims=True)
    if sink is not None:
        den = den + jnp.exp(sink - m)
    o = jnp.dot(p.astype(BF16), vwin, preferred_element_type=F32)
    return o * (1.0 / den), m + jnp.log(den)


def _band_bias(rows, max_dist, key_lo):
    qi = lax.broadcasted_iota(I32, (rows, 2 * ATT_BLOCK), 0) & (ATT_BLOCK - 1)
    sj = lax.broadcasted_iota(I32, (rows, 2 * ATT_BLOCK), 1)
    dist = ATT_BLOCK + qi - sj
    return jnp.where((dist >= 0) & (dist <= max_dist) & (sj >= key_lo), 0.0, NEG_BIG)


def _attn_chunks(chunks, kvx, bias):
    nc = len(chunks)
    qs = chunks[0] if nc == 1 else jnp.concatenate(chunks, axis=0)
    kz, zk, vz, zv = [kvx[:, part * LANES:(part + 1) * LANES] for part in range(KV_PARTS)]
    rows = nc * ATT_BLOCK

    def one_head(kpart):
        s = lax.dot_general(qs, kpart, (((1,), (1,)), ((), ())), preferred_element_type=F32) + bias
        m = jnp.max(s, axis=1, keepdims=True)
        p = jnp.exp(s - m)
        return p.astype(BF16), m, jnp.sum(p, axis=1, keepdims=True)

    pa, ma, da = one_head(kz)
    pb, mb, db = one_head(zk)
    o = jnp.dot(pa, vz, preferred_element_type=F32) + jnp.dot(pb, zv, preferred_element_type=F32)
    low = lax.broadcasted_iota(I32, (rows, LANES), 1) < HEAD_DIM
    o = o * jnp.where(low, 1.0 / da, 1.0 / db)
    lse = jnp.where(low, ma + jnp.log(da), mb + jnp.log(db))
    return ([o[c * ATT_BLOCK:(c + 1) * ATT_BLOCK] for c in range(nc)],
            [lse[c * ATT_BLOCK:(c + 1) * ATT_BLOCK] for c in range(nc)])


def _swa_kernel(sink_ref, q_ref, kp_ref, kc_ref, vp_ref, vc_ref, o_ref, *, tq):
    i = pl.program_id(1)
    kfull = jnp.concatenate([kp_ref[...], kc_ref[...]], axis=0)
    vfull = jnp.concatenate([vp_ref[...], vc_ref[...]], axis=0)
    g = SWA_Q_HEADS // SWA_KV_HEADS
    for j in range(tq // ATT_BLOCK):
        key_lo = jnp.where(i == 0, ATT_BLOCK, 0) if j == 0 else 0
        valid = _band_mask(SWA_WINDOW - 1, key_lo)
        r0 = j * ATT_BLOCK
        outs = []
        for h in range(SWA_Q_HEADS):
            kv = h // g
            q = q_ref[r0:r0 + ATT_BLOCK, h * HEAD_DIM:(h + 1) * HEAD_DIM]
            kwin = kfull[r0:r0 + 2 * ATT_BLOCK, kv * HEAD_DIM:(kv + 1) * HEAD_DIM]
            vwin = vfull[r0:r0 + 2 * ATT_BLOCK, kv * HEAD_DIM:(kv + 1) * HEAD_DIM]
            o, _ = _attn_block(q, kwin, vwin, valid, sink_ref[h])
            outs.append(o)
        o_ref[r0:r0 + ATT_BLOCK, :] = jnp.concatenate(outs, axis=1).astype(BF16)


def _swa(q, k, v, sinks, b, s, tq=256):
    per = tq // ATT_BLOCK
    cur = lambda bi, i: (bi, i, 0)
    prev = lambda bi, i: (bi, jnp.maximum(i * per - 1, 0), 0)
    q3 = q.reshape(b, s, SWA_Q_DIM)
    k3 = k.reshape(b, s, SWA_KV_DIM)
    v3 = v.reshape(b, s, SWA_KV_DIM)
    out = pl.pallas_call(
        functools.partial(_swa_kernel, tq=tq),
        grid=(b, s // tq),
        in_specs=[pl.BlockSpec(memory_space=pltpu.SMEM),
                  pl.BlockSpec((None, tq, SWA_Q_DIM), cur),
                  pl.BlockSpec((None, ATT_BLOCK, SWA_KV_DIM), prev),
                  pl.BlockSpec((None, tq, SWA_KV_DIM), cur),
                  pl.BlockSpec((None, ATT_BLOCK, SWA_KV_DIM), prev),
                  pl.BlockSpec((None, tq, SWA_KV_DIM), cur)],
        out_specs=pl.BlockSpec((None, tq, SWA_Q_DIM), cur),
        out_shape=jax.ShapeDtypeStruct((b, s, SWA_Q_DIM), BF16),
        compiler_params=_params(("parallel", "parallel")),
        name="swa_attention",
    )(sinks, q3, k3, k3, v3, v3)
    return out.reshape(b * s, SWA_Q_DIM)


def _pool_mixer(u, halo, seq_row0):
    tm = u.shape[0]
    full = jnp.concatenate([halo, u], axis=0)
    sums = [full]
    for shift in (1, 2, 4, 8):
        prev = sums[-1]
        sums.append(prev + pltpu.roll(prev, shift, 0))
    lane = lax.broadcasted_iota(I32, (tm, POOL_WIDTH), 1)
    row = lax.broadcasted_iota(I32, (tm, POOL_WIDTH), 0)
    grp = lane // POOL_GROUP
    win = sums[4][POOL_HALO:]
    width = jnp.full((tm, POOL_WIDTH), POOL_WINDOWS[3], I32)
    for gi in (2, 1, 0):
        win = jnp.where(grp == gi, sums[gi + 1][POOL_HALO:], win)
        width = jnp.where(grp == gi, POOL_WINDOWS[gi], width)
    count = jnp.minimum(seq_row0 + row + 1, width).astype(F32)
    return win / count - u


def _outproj0_kernel(u_ref, uh_ref, o_ref, x_ref, pw_ref, ps_ref, wa_ref, wb_ref, lw_ref, lb_ref, out_ref, *, tm):
    i = pl.program_id(1)
    halo = jnp.where(i == 0, 0.0, uh_ref[...])
    d = _pool_mixer(u_ref[...], halo, i * tm)
    a = jnp.dot(d.astype(BF16), pw_ref[...], preferred_element_type=F32) * ps_ref[...]
    mix = jnp.dot(a.astype(BF16), wa_ref[...], preferred_element_type=F32)
    mix = mix + jnp.dot(o_ref[...], wb_ref[...], preferred_element_type=F32)
    y = ALPHA * x_ref[...] + mix
    out_ref[...] = _layer_norm(y, lw_ref[...], lb_ref[...])


def _outproj0(u, attn, x2d, pool_bd, pool_scale, wa, wb, ln_w, ln_b, b, s, tm=512):
    d = x2d.shape[1]
    per = tm // POOL_HALO
    cur = lambda bi, i: (bi, i, 0)
    prev = lambda bi, i: (bi, jnp.maximum(i * per - 1, 0), 0)
    const = lambda bi, i: (0, 0)
    out = pl.pallas_call(
        functools.partial(_outproj0_kernel, tm=tm),
        grid=(b, s // tm),
        in_specs=[pl.BlockSpec((None, tm, POOL_WIDTH), cur),
                  pl.BlockSpec((None, POOL_HALO, POOL_WIDTH), prev),
                  pl.BlockSpec((None, tm, SWA_Q_DIM), cur),
                  pl.BlockSpec((None, tm, d), cur),
                  pl.BlockSpec((POOL_WIDTH, POOL_WIDTH), const),
                  pl.BlockSpec((1, POOL_WIDTH), const),
                  pl.BlockSpec((POOL_WIDTH, d), const),
                  pl.BlockSpec((SWA_Q_DIM, d), const),
                  pl.BlockSpec((1, d), const),
                  pl.BlockSpec((1, d), const)],
        out_specs=pl.BlockSpec((None, tm, d), cur),
        out_shape=jax.ShapeDtypeStruct((b, s, d), F32),
        compiler_params=_params(("parallel", "parallel")),
        name="outproj0",
    )(u.reshape(b, s, POOL_WIDTH), u.reshape(b, s, POOL_WIDTH), attn.reshape(b, s, SWA_Q_DIM),
      x2d.reshape(b, s, d), pool_bd, pool_scale, wa, wb, ln_w, ln_b)
    return out.reshape(b * s, d)


def _swiglu_tile(xb, wg, wu, wd):
    g = jnp.dot(xb, wg, preferred_element_type=F32)
    u = jnp.dot(xb, wu, preferred_element_type=F32)
    h = (g * jax.nn.sigmoid(g)) * u
    return jnp.dot(h.astype(BF16), wd, preferred_element_type=F32)


def _ffn_kernel(x_ref, wg_ref, wu_ref, wd_ref, lw_ref, lb_ref, out_ref, acc_ref):
    j = pl.program_id(1)
    part = _swiglu_tile(x_ref[...].astype(BF16), wg_ref[...], wu_ref[...], wd_ref[...])

    @pl.when(j == 0)
    def _():
        acc_ref[...] = part

    @pl.when(j > 0)
    def _():
        acc_ref[...] += part

    @pl.when(j == pl.num_programs(1) - 1)
    def _():
        y = ALPHA * x_ref[...] + acc_ref[...]
        out_ref[...] = _layer_norm(y, lw_ref[...], lb_ref[...])


def _ffn_dense(x2d, wg, wu, wd, ln_w, ln_b, tm=512, tf=1408):
    t, d = x2d.shape
    f = wg.shape[1]
    return pl.pallas_call(
        _ffn_kernel,
        grid=(t // tm, f // tf),
        in_specs=[pl.BlockSpec((tm, d), lambda i, j: (i, 0)),
                  pl.BlockSpec((d, tf), lambda i, j: (0, j)),
                  pl.BlockSpec((d, tf), lambda i, j: (0, j)),
                  pl.BlockSpec((tf, d), lambda i, j: (j, 0)),
                  pl.BlockSpec((1, d), lambda i, j: (0, 0)),
                  pl.BlockSpec((1, d), lambda i, j: (0, 0))],
        out_specs=pl.BlockSpec((tm, d), lambda i, j: (i, 0)),
        out_shape=jax.ShapeDtypeStruct((t, d), F32),
        scratch_shapes=[pltpu.VMEM((tm, d), F32)],
        compiler_params=_params(("parallel", "arbitrary")),
        name="ffn_dense",
    )(x2d, wg, wu, wd, ln_w, ln_b)


KV_PARTS = 4


def _kv_operands(kv, low):
    swapped = pltpu.roll(kv, HEAD_DIM, 1)
    return (jnp.where(low, kv, 0.0), jnp.where(low, 0.0, swapped),
            jnp.where(low, swapped, 0.0), jnp.where(low, 0.0, kv))


def _inproj1_kernel(x_ref, wq_ref, wkv_ref, wc_ref, cw_ref, cos_ref, sin_ref,
                    q0_ref, q1_ref, q2_ref, kv0_ref, kv1_ref, kv2_ref, d_ref,
                    zc_ref, sq_ref, skv_ref, *, tm):
    i = pl.program_id(1)
    xb = x_ref[...].astype(BF16)
    cos = cos_ref[...]
    sin = sin_ref[...]
    lane = lax.broadcasted_iota(I32, (tm, LANES), 1)
    low = lane < HEAD_DIM
    cos_k = jnp.where(low, cos, 1.0)
    sin_k = jnp.where(low, sin, 0.0)
    q_refs = (q0_ref, q1_ref, q2_ref)
    kv_refs = (kv0_ref, kv1_ref, kv2_ref)
    for gi, (_, dil) in enumerate(DIL_PAIRS):
        q = jnp.dot(xb, wq_ref[gi], preferred_element_type=F32)
        q = _rope(q, cos, sin) * QK_SCALE
        kv = jnp.dot(xb, wkv_ref[gi], preferred_element_type=F32)
        kv = _rope(kv, cos_k, sin_k)
        kvx = [_kv_operands(kv[:, h * LANES:(h + 1) * LANES], low) for h in range(DIL_KV_HEADS)]
        if dil == 1:
            q_refs[gi][0] = q.astype(BF16)
            for h in range(DIL_KV_HEADS):
                kv_refs[gi][h, 0] = jnp.concatenate(kvx[h], axis=1).astype(BF16)
        else:
            n = tm // dil
            for c in range(DIL_Q_DIM // LANES):
                sq_ref[c] = q[:, c * LANES:(c + 1) * LANES]
            for h in range(DIL_KV_HEADS):
                for part in range(KV_PARTS):
                    skv_ref[h * KV_PARTS + part] = kvx[h][part]
            for r in range(dil):
                rows = pl.ds(r, n, stride=dil)
                q_refs[gi][r] = jnp.concatenate(
                    [sq_ref[c, rows, :] for c in range(DIL_Q_DIM // LANES)], axis=1).astype(BF16)
                for h in range(DIL_KV_HEADS):
                    kv_refs[gi][h, r] = jnp.concatenate(
                        [skv_ref[h * KV_PARTS + part, rows, :] for part in range(KV_PARTS)], axis=1).astype(BF16)

    hc = jnp.dot(xb, wc_ref[...], preferred_element_type=F32)
    z = hc[:, 2 * CONV_WIDTH:] * hc[:, :CONV_WIDTH]
    zprev = jnp.where(i == 0, 0.0, zc_ref[...])
    zfull = jnp.concatenate([zprev, z], axis=0)
    z1 = pltpu.roll(zfull, 1, 0)[CONV_HALO:]
    z2 = pltpu.roll(zfull, 2, 0)[CONV_HALO:]
    cw = cw_ref[...]
    y = cw[0:1] * z2 + cw[1:2] * z1 + cw[2:3] * z
    d_ref[...] = (hc[:, CONV_WIDTH:2 * CONV_WIDTH] * y).astype(BF16)
    zc_ref[...] = z[tm - CONV_HALO:]


def _inproj1(x2d, wq, wkv, wc, conv_w, cos_t, sin_t, b, s, tm=512):
    d = x2d.shape[1]
    cur = lambda bi, i: (bi, i, 0)
    row = lambda bi, i: (bi * (s // tm) + i, 0)
    in_specs = [pl.BlockSpec((None, tm, d), cur),
                pl.BlockSpec(wq.shape, lambda bi, i: (0, 0, 0)),
                pl.BlockSpec(wkv.shape, lambda bi, i: (0, 0, 0)),
                pl.BlockSpec(wc.shape, lambda bi, i: (0, 0)),
                pl.BlockSpec(conv_w.shape, lambda bi, i: (0, 0)),
                pl.BlockSpec((tm, LANES), row),
                pl.BlockSpec((tm, LANES), row)]
    out_specs, out_shape = [], []
    for _, dil in DIL_PAIRS:
        out_specs.append(pl.BlockSpec((None, dil, tm // dil, DIL_Q_DIM), lambda bi, i: (bi, 0, i, 0)))
        out_shape.append(jax.ShapeDtypeStruct((b, dil, s // dil, DIL_Q_DIM), BF16))
    for _, dil in DIL_PAIRS:
        out_specs.append(pl.BlockSpec((None, DIL_KV_HEADS, dil, tm // dil, KV_PARTS * LANES),
                                      lambda bi, i: (bi, 0, 0, i, 0)))
        out_shape.append(jax.ShapeDtypeStruct((b, DIL_KV_HEADS, dil, s // dil, KV_PARTS * LANES), BF16))
    out_specs.append(pl.BlockSpec((None, tm, CONV_WIDTH), cur))
    out_shape.append(jax.ShapeDtypeStruct((b, s, CONV_WIDTH), BF16))
    return pl.pallas_call(
        functools.partial(_inproj1_kernel, tm=tm),
        grid=(b, s // tm),
        in_specs=in_specs,
        out_specs=out_specs,
        out_shape=out_shape,
        scratch_shapes=[pltpu.VMEM((CONV_HALO, CONV_WIDTH), F32),
                        pltpu.VMEM((DIL_Q_DIM // LANES, tm, LANES), F32),
                        pltpu.VMEM((DIL_KV_HEADS * KV_PARTS, tm, LANES), F32)],
        compiler_params=_params(("parallel", "arbitrary")),
        name="inproj1",
    )(x2d.reshape(b, s, d), wq, wkv, wc, conv_w, cos_t, sin_t)


def _dil_kernel(q0_ref, q1_ref, q2_ref, kv0_ref, kv1_ref, kv2_ref, out_ref, o_run, l_run, bias_ref, *, s):
    q_refs = (q0_ref, q1_ref, q2_ref)
    kv_refs = (kv0_ref, kv1_ref, kv2_ref)
    n_blocks = s // ATT_BLOCK
    width = DIL_Q_PER_KV * HEAD_DIM
    max_dist = DIL_PAIRS[0][0] // DIL_PAIRS[0][1]
    assert all(w // d == max_dist for w, d in DIL_PAIRS)
    stacked = (width // LANES) * ATT_BLOCK
    bias_ref[0] = _band_bias(stacked, max_dist, 0)
    bias_ref[1] = _band_bias(stacked, max_dist, ATT_BLOCK)
    for gi, (window, dil) in enumerate(DIL_PAIRS):
        q_ref, kv_ref = q_refs[gi], kv_refs[gi]
        blocks_per_sub = (s // dil) // ATT_BLOCK
        shift = blocks_per_sub.bit_length() - 1

        def body(n, carry, q_ref=q_ref, kv_ref=kv_ref, gi=gi, window=window, dil=dil,
                 blocks_per_sub=blocks_per_sub, shift=shift):
            res = lax.shift_right_logical(n, shift)
            bi = n & (blocks_per_sub - 1)
            rows = pl.ds(pl.multiple_of(n * ATT_BLOCK, ATT_BLOCK), ATT_BLOCK)
            prow = pl.ds(pl.multiple_of(jnp.maximum(n - 1, 0) * ATT_BLOCK, ATT_BLOCK), ATT_BLOCK)
            kvx = jnp.concatenate([kv_ref[prow, :], kv_ref[rows, :]], axis=0)
            q4 = q_ref[rows, :]
            chunks = [q4[:, c * LANES:(c + 1) * LANES] for c in range(width // LANES)]
            first = jnp.where(bi == 0, 1, 0)
            o_chunks, l_chunks = _attn_chunks(chunks, kvx, bias_ref[first])
            o4 = jnp.concatenate(o_chunks, axis=1)
            l4 = jnp.concatenate(l_chunks, axis=1)
            start = res + dil * ATT_BLOCK * bi
            if dil == 1:
                tok = pl.ds(pl.multiple_of(start, ATT_BLOCK), ATT_BLOCK)
            else:
                tok = pl.ds(start, ATT_BLOCK, stride=dil)
            for c in range(width // LANES):
                o_c = o4[:, c * LANES:(c + 1) * LANES]
                l_c = l4[:, c * LANES:(c + 1) * LANES]
                if gi == 0:
                    o_run[c, tok, :] = o_c
                    l_run[c, tok, :] = l_c
                else:
                    o_old = o_run[c, tok, :]
                    l_old = l_run[c, tok, :]
                    m = jnp.maximum(l_old, l_c)
                    a = jnp.exp(l_old - m)
                    bb = jnp.exp(l_c - m)
                    tot = a + bb
                    o_run[c, tok, :] = (o_old * a + o_c * bb) * (1.0 / tot)
                    if gi < len(DIL_PAIRS) - 1:
                        l_run[c, tok, :] = m + jnp.log(tot)
            return carry

        lax.fori_loop(0, n_blocks, body, 0, unroll=4)
    for c in range(width // LANES):
        out_ref[:, c * LANES:(c + 1) * LANES] = o_run[c].astype(BF16)


def _dilated(qs, kvs, b, s):
    width = DIL_Q_PER_KV * HEAD_DIM
    in_specs = [pl.BlockSpec((None, s, width), lambda bi, h: (bi, 0, h)) for _ in DIL_PAIRS]
    in_specs += [pl.BlockSpec((None, None, s, KV_PARTS * LANES), lambda bi, h: (bi, h, 0, 0)) for _ in DIL_PAIRS]
    qs = [q.reshape(b, s, DIL_Q_DIM) for q in qs]
    kvs = [kv.reshape(b, DIL_KV_HEADS, s, KV_PARTS * LANES) for kv in kvs]
    out = pl.pallas_call(
        functools.partial(_dil_kernel, s=s),
        grid=(b, DIL_KV_HEADS),
        in_specs=in_specs,
        out_specs=pl.BlockSpec((None, s, width), lambda bi, h: (bi, 0, h)),
        out_shape=jax.ShapeDtypeStruct((b, s, DIL_Q_DIM), BF16),
        scratch_shapes=[pltpu.VMEM((width // LANES, s, LANES), F32),
                        pltpu.VMEM((width // LANES, s, LANES), F32),
                        pltpu.VMEM((2, (width // LANES) * ATT_BLOCK, 2 * ATT_BLOCK), F32)],
        compiler_params=_params(("parallel", "parallel"), vmem=56 * 1024 * 1024),
        name="dilated_attention",
    )(*qs, *kvs)
    return out.reshape(b * s, DIL_Q_DIM)


def _to_slabs(ref, val):
    rows, d = val.shape
    per = d // LANES
    for c in range(per):
        ref[pl.ds(c, rows, stride=per), :] = val[:, c * LANES:(c + 1) * LANES]


def _from_slabs(ref, rows):
    per = ref.shape[0] // rows
    return jnp.concatenate([ref[pl.ds(c, rows, stride=per), :] for c in range(per)], axis=1)


def _outproj1_kernel(c_ref, d_ref, x_ref, wa_ref, wb_ref, lw_ref, lb_ref, out_ref, slab_ref):
    mix = jnp.dot(c_ref[...], wa_ref[...], preferred_element_type=F32)
    mix = mix + jnp.dot(d_ref[...], wb_ref[...], preferred_element_type=F32)
    y = _layer_norm(ALPHA * x_ref[...] + mix, lw_ref[...], lb_ref[...])
    out_ref[...] = y
    _to_slabs(slab_ref, y)


def _outproj1(c, dconv, x2d, wa, wb, ln_w, ln_b, tm=512):
    t, d = x2d.shape
    row = lambda i: (i, 0)
    const = lambda i: (0, 0)
    return pl.pallas_call(
        _outproj1_kernel,
        grid=(t // tm,),
        in_specs=[pl.BlockSpec((tm, DIL_Q_DIM), row),
                  pl.BlockSpec((tm, CONV_WIDTH), row),
                  pl.BlockSpec((tm, d), row),
                  pl.BlockSpec((DIL_Q_DIM, d), const),
                  pl.BlockSpec((CONV_WIDTH, d), const),
                  pl.BlockSpec((1, d), const),
                  pl.BlockSpec((1, d), const)],
        out_specs=[pl.BlockSpec((tm, d), row), pl.BlockSpec((tm * (d // LANES), LANES), row)],
        out_shape=[jax.ShapeDtypeStruct((t, d), F32), jax.ShapeDtypeStruct((t * (d // LANES), LANES), F32)],
        compiler_params=_params(("parallel",)),
        name="outproj1",
    )(c, dconv, x2d, wa, wb, ln_w, ln_b)


def _route_kernel(x_ref, wr_ref, e_ref, g_ref, r_ref, cnt_ref, run_ref, *, tm):
    i = pl.program_id(0)

    @pl.when(i == 0)
    def _():
        run_ref[...] = jnp.zeros_like(run_ref)

    logits = lax.dot_general(wr_ref[...], x_ref[...], (((1,), (1,)), ((), ())),
                             precision=lax.Precision.HIGHEST, preferred_element_type=F32)
    eid = lax.broadcasted_iota(I32, (N_EXPERTS, tm), 0)
    m1 = jnp.max(logits, axis=0, keepdims=True)
    i1 = jnp.min(jnp.where(logits == m1, eid, N_EXPERTS), axis=0, keepdims=True)
    rest = jnp.where(eid == i1, -jnp.inf, logits)
    m2 = jnp.max(rest, axis=0, keepdims=True)
    i2 = jnp.min(jnp.where(rest == m2, eid, N_EXPERTS), axis=0, keepdims=True)
    t2 = jnp.exp(m2 - m1)
    g1 = 1.0 / (1.0 + t2)
    g2 = t2 / (1.0 + t2)
    oh1 = eid == i1
    oh2 = eid == i2
    oh = jnp.where(oh1 | oh2, 1.0, 0.0)
    ri = lax.broadcasted_iota(I32, (tm, tm), 0)
    ci = lax.broadcasted_iota(I32, (tm, tm), 1)
    tri = jnp.where(ri <= ci, 1.0, 0.0).astype(BF16)
    incl = jnp.dot(oh.astype(BF16), tri, preferred_element_type=F32)
    rank = run_ref[:, 0:1] + incl - oh
    r1 = jnp.sum(jnp.where(oh1, rank, 0.0), axis=0, keepdims=True)
    r2 = jnp.sum(jnp.where(oh2, rank, 0.0), axis=0, keepdims=True)
    e_ref[0:1, :] = i1
    e_ref[1:2, :] = i2
    g_ref[0:1, :] = g1
    g_ref[1:2, :] = g2
    r_ref[0:1, :] = r1.astype(I32)
    r_ref[1:2, :] = r2.astype(I32)
    run_ref[...] = run_ref[...] + incl[:, tm - 1:tm]
    cnt_ref[...] = run_ref[...]


def _route(x2d, wr_t, tm=512):
    t, d = x2d.shape
    col = lambda i: (0, i)
    return pl.pallas_call(
        functools.partial(_route_kernel, tm=tm),
        grid=(t // tm,),
        in_specs=[pl.BlockSpec((tm, d), lambda i: (i, 0)),
                  pl.BlockSpec((N_EXPERTS, d), lambda i: (0, 0))],
        out_specs=[pl.BlockSpec((2, tm), col), pl.BlockSpec((2, tm), col), pl.BlockSpec((2, tm), col),
                   pl.BlockSpec((N_EXPERTS, LANES), lambda i: (0, 0))],
        out_shape=[jax.ShapeDtypeStruct((2, t), I32), jax.ShapeDtypeStruct((2, t), F32),
                   jax.ShapeDtypeStruct((2, t), I32), jax.ShapeDtypeStruct((N_EXPERTS, LANES), F32)],
        scratch_shapes=[pltpu.VMEM((N_EXPERTS, LANES), F32)],
        compiler_params=_params(("arbitrary",)),
        name="moe_route",
    )(x2d, wr_t)


ROW_SLAB = 8


def _slab_copy(src_ref, dst_ref, sem, src_row, dst_row, n=1):
    src = src_ref.at[pl.ds(pl.multiple_of(src_row * ROW_SLAB, ROW_SLAB), n * ROW_SLAB)]
    dst = dst_ref.at[pl.ds(pl.multiple_of(dst_row * ROW_SLAB, ROW_SLAB), n * ROW_SLAB)]
    return pltpu.make_async_copy(src, dst, sem)


def _moe_kernel(be_ref, nv_ref, tok_ref, tokn_ref, x_hbm, wg_ref, wu_ref, wd_ref, out_ref,
                xs_ref, xb_ref, acc_ref, sem, *, tm, nj):
    i = pl.program_id(0)
    j = pl.program_id(1)
    n_valid = nv_ref[0]
    slot = i & 1
    per_step = tm // nj

    def row_copy(tok, dst_slot, r):
        return _slab_copy(x_hbm, xs_ref, sem.at[dst_slot], tok, dst_slot * tm + r)

    @pl.when(i < n_valid)
    def _():
        @pl.when(j == 0)
        def _():
            @pl.when(i == 0)
            def _():
                def start(r, c):
                    row_copy(tok_ref[0, 0, r], 0, r).start()
                    return c
                lax.fori_loop(0, tm, start, 0)

            _slab_copy(x_hbm, xs_ref, sem.at[slot], 0, slot * tm, n=tm).wait()
            base = slot * (tm * ROW_SLAB)
            xb_ref[...] = jnp.concatenate(
                [xs_ref[pl.ds(base + c, tm, stride=ROW_SLAB), :] for c in range(ROW_SLAB)], axis=1).astype(BF16)
            acc_ref[...] = jnp.zeros_like(acc_ref)

        for r in range(per_step):
            rr = j * per_step + r
            row_copy(tokn_ref[0, 0, rr], 1 - slot, rr).start()

        acc_ref[...] += _swiglu_tile(xb_ref[...], wg_ref[...], wu_ref[...], wd_ref[...])

        @pl.when(j == nj - 1)
        def _():
            _to_slabs(out_ref, acc_ref[...])

            @pl.when(i + 1 >= n_valid)
            def _():
                _slab_copy(x_hbm, xs_ref, sem.at[1 - slot], 0, (1 - slot) * tm, n=tm).wait()

    @pl.when((i >= nv_ref[0]) & (j == pl.num_programs(1) - 1))
    def _():
        out_ref[...] = jnp.zeros_like(out_ref)


def _moe_experts(x_slabs, row_tok, block_e, n_valid, wg, wu, wd, tm, tf=896):
    d = wg.shape[1]
    n_rows = row_tok.shape[0]
    n_blocks = n_rows // tm
    f = wg.shape[2]
    nj = f // tf

    def live(i, j, be, nv):
        ok = i < nv[0]
        return jnp.where(ok, i, nv[0] - 1), jnp.where(ok, j, nj - 1)

    def w_map(i, j, be, nv):
        ii, jj = live(i, j, be, nv)
        return be[ii], 0, jj

    def wd_map(i, j, be, nv):
        ii, jj = live(i, j, be, nv)
        return be[ii], jj, 0

    grid_spec = pltpu.PrefetchScalarGridSpec(
        num_scalar_prefetch=2,
        grid=(n_blocks, nj),
        in_specs=[pl.BlockSpec((1, 1, tm), lambda i, j, be, nv: (i, 0, 0), memory_space=pltpu.SMEM),
                  pl.BlockSpec((1, 1, tm), lambda i, j, be, nv: (jnp.minimum(i + 1, n_blocks - 1), 0, 0),
                               memory_space=pltpu.SMEM),
                  pl.BlockSpec(memory_space=pl.ANY),
                  pl.BlockSpec((None, d, tf), w_map),
                  pl.BlockSpec((None, d, tf), w_map),
                  pl.BlockSpec((None, tf, d), wd_map)],
        out_specs=pl.BlockSpec((tm * ROW_SLAB, LANES), lambda i, j, be, nv: (i, 0)),
        scratch_shapes=[pltpu.VMEM((2 * tm * ROW_SLAB, LANES), F32),
                        pltpu.VMEM((tm, d), BF16),
                        pltpu.VMEM((tm, d), F32),
                        pltpu.SemaphoreType.DMA((2,))],
    )
    tok3 = row_tok.reshape(n_blocks, 1, tm)
    return pl.pallas_call(
        functools.partial(_moe_kernel, tm=tm, nj=nj),
        grid_spec=grid_spec,
        out_shape=jax.ShapeDtypeStruct((n_rows * ROW_SLAB, LANES), F32),
        compiler_params=_params(("arbitrary", "arbitrary")),
        name="moe_experts",
    )(block_e, n_valid, tok3, tok3, x_slabs, wg, wu, wd)


def _combine_kernel(d1_ref, d2_ref, y_ref, x_ref, g1_ref, g2_ref, lw_ref, lb_ref, out_ref, ya, yb, sem, *, tm):
    def start(r, c):
        _slab_copy(y_ref, ya, sem.at[0], d1_ref[0, 0, r], r).start()
        _slab_copy(y_ref, yb, sem.at[1], d2_ref[0, 0, r], r).start()
        return c

    lax.fori_loop(0, tm, start, 0)
    _slab_copy(y_ref, ya, sem.at[0], 0, 0, n=tm).wait()
    _slab_copy(y_ref, yb, sem.at[1], 0, 0, n=tm).wait()
    ffn = _from_slabs(ya, tm) * g1_ref[...] + _from_slabs(yb, tm) * g2_ref[...]
    out_ref[...] = _layer_norm(ALPHA * x_ref[...] + ffn, lw_ref[...], lb_ref[...])


def _combine(y, dest1, dest2, x2d, g1, g2, ln_w, ln_b, tm=256):
    t, d = x2d.shape
    row = lambda i: (i, 0)
    smem = lambda: pl.BlockSpec((1, 1, tm), lambda i: (i, 0, 0), memory_space=pltpu.SMEM)
    return pl.pallas_call(
        functools.partial(_combine_kernel, tm=tm),
        grid=(t // tm,),
        in_specs=[smem(), smem(),
                  pl.BlockSpec(memory_space=pl.ANY),
                  pl.BlockSpec((tm, d), row),
                  pl.BlockSpec((tm, 1), row),
                  pl.BlockSpec((tm, 1), row),
                  pl.BlockSpec((1, d), lambda i: (0, 0)),
                  pl.BlockSpec((1, d), lambda i: (0, 0))],
        out_specs=pl.BlockSpec((tm, d), row),
        out_shape=jax.ShapeDtypeStruct((t, d), F32),
        scratch_shapes=[pltpu.VMEM((tm * ROW_SLAB, LANES), F32), pltpu.VMEM((tm * ROW_SLAB, LANES), F32),
                        pltpu.SemaphoreType.DMA((2,))],
        compiler_params=_params(("arbitrary",)),
        name="moe_combine",
    )(dest1.reshape(t // tm, 1, tm), dest2.reshape(t // tm, 1, tm), y, x2d,
      g1.reshape(t, 1), g2.reshape(t, 1), ln_w, ln_b)


def _moe_layer(x2d, xb, router_w, wg, wu, wd, ln_w, ln_b, tmoe=512):
    t, d = x2d.shape
    e_sel, gates, ranks, counts = _route(x2d, router_w.T)
    counts = counts[:, 0].astype(I32)
    padded = (counts + tmoe - 1) // tmoe * tmoe
    pend = jnp.cumsum(padded)
    pstart = pend - padded
    n_blocks = -(-(2 * t + N_EXPERTS * (tmoe - 1)) // tmoe)
    n_rows = n_blocks * tmoe
    first_row = sum(jnp.where(e_sel == e, pstart[e], 0) for e in range(N_EXPERTS))
    dest = first_row + ranks
    tok = jnp.arange(t, dtype=I32)
    row_tok = jnp.zeros((n_rows,), I32).at[dest[0]].set(tok).at[dest[1]].set(tok)
    block_row0 = jnp.arange(n_blocks, dtype=I32) * tmoe
    block_e = jnp.minimum(jnp.sum((block_row0[:, None] >= pend[None, :]).astype(I32), axis=1), N_EXPERTS - 1)
    n_valid = (pend[-1:] // tmoe).astype(I32)
    y = _moe_experts(xb, row_tok, block_e, n_valid, wg, wu, wd, tmoe)
    return _combine(y, dest[0], dest[1], x2d, gates[0], gates[1], ln_w, ln_b)


def kernel(x, positions, ln_w, ln_b, even_w_in, pool_w, pool_scale, swa_sinks, even_w_out, ffn_w_gate, ffn_w_up, ffn_w_down, odd_w_in, conv_w, odd_w_out, router_w, moe_w_gate, moe_w_up, moe_w_down):
    b, s, d = x.shape
    t = b * s
    x2d = x.reshape(t, d)
    cos_t, sin_t = _rope_tables(positions)
    ln = lambda layer, k: (ln_w[layer, k].reshape(1, d), ln_b[layer, k].reshape(1, d))

    u, q, k, v = _inproj0(x2d, even_w_in[0].astype(BF16), cos_t, sin_t)
    attn = _swa(q, k, v, swa_sinks[0], b, s)
    pool_bd = jax.scipy.linalg.block_diag(*[pool_w[0, gi] for gi in range(len(POOL_WINDOWS))]).astype(BF16)
    w_out0 = even_w_out[0].astype(BF16)
    x2d = _outproj0(u, attn, x2d, pool_bd, pool_scale[0].reshape(1, POOL_WIDTH),
                    w_out0[:POOL_WIDTH], w_out0[POOL_WIDTH:], *ln(0, 0), b, s)
    x2d = _ffn_dense(x2d, ffn_w_gate[0].astype(BF16), ffn_w_up[0].astype(BF16), ffn_w_down[0].astype(BF16),
                     *ln(0, 1))

    w_in1 = odd_w_in[0].astype(BF16)
    c_in = len(DIL_PAIRS) * DIL_GROUP_IN
    wq, wkv = [], []
    for gi in range(len(DIL_PAIRS)):
        g0 = gi * DIL_GROUP_IN
        wq.append(w_in1[:, g0:g0 + DIL_Q_DIM])
        kcol = g0 + DIL_Q_DIM
        vcol = kcol + DIL_KV_DIM
        parts = []
        for h in range(DIL_KV_HEADS):
            parts += [w_in1[:, kcol + h * HEAD_DIM:kcol + (h + 1) * HEAD_DIM],
                      w_in1[:, vcol + h * HEAD_DIM:vcol + (h + 1) * HEAD_DIM]]
        wkv.append(jnp.concatenate(parts, axis=1))
    outs = _inproj1(x2d, jnp.stack(wq), jnp.stack(wkv), w_in1[:, c_in:], conv_w[0], cos_t, sin_t, b, s)
    c_out = _dilated(outs[0:3], outs[3:6], b, s)
    w_out1 = odd_w_out[0].astype(BF16)
    x2d, xb = _outproj1(c_out, outs[6].reshape(t, CONV_WIDTH), x2d, w_out1[:DIL_Q_DIM], w_out1[DIL_Q_DIM:],
                        *ln(1, 0))
    x2d = _moe_layer(x2d, xb, router_w[0], moe_w_gate[0].astype(BF16), moe_w_up[0].astype(BF16),
                     moe_w_down[0].astype(BF16), *ln(1, 1))
    return x2d.reshape(b, s, d)
```

```python
import functools
import math

import jax
import jax.numpy as jnp
from jax import lax
from jax.experimental import pallas as pl
from jax.experimental.pallas import tpu as pltpu

F32 = jnp.float32
BF16 = jnp.bfloat16
I32 = jnp.int32

HEAD_DIM = 64
ROPE_THETA = 10000.0
ATT_BLOCK = 128
LN_EPS = 1e-5
POOL_WINDOWS = (2, 4, 8, 16)
POOL_GROUP = 64
POOL_WIDTH = 256
POOL_HALO = 16
SWA_WINDOW = 128
SWA_Q_HEADS = 12
SWA_KV_HEADS = 4
SWA_Q_DIM = SWA_Q_HEADS * HEAD_DIM
SWA_KV_DIM = SWA_KV_HEADS * HEAD_DIM
DIL_PAIRS = ((128, 1), (512, 4), (2048, 16))
DIL_Q_HEADS = 8
DIL_KV_HEADS = 2
DIL_Q_DIM = DIL_Q_HEADS * HEAD_DIM
DIL_KV_DIM = DIL_KV_HEADS * HEAD_DIM
DIL_GROUP_IN = DIL_Q_DIM + 2 * DIL_KV_DIM
DIL_Q_PER_KV = DIL_Q_HEADS // DIL_KV_HEADS
CONV_WIDTH = 512
CONV_K = 3
CONV_HALO = 8
N_EXPERTS = 8
DEPTH = 2
ALPHA = (2 * DEPTH) ** 0.25
QK_SCALE = 1.0 / math.sqrt(HEAD_DIM)
NEG_BIG = -1e30
KV_PARTS = 4

LANES = 128
VMEM_LIMIT = 48 * 1024 * 1024


def _params(sem, vmem=VMEM_LIMIT):
    return pltpu.CompilerParams(dimension_semantics=sem, vmem_limit_bytes=vmem)


def _layer_norm(y, w, b):
    mu = jnp.mean(y, axis=-1, keepdims=True)
    yc = y - mu
    var = jnp.mean(yc * yc, axis=-1, keepdims=True)
    return yc * lax.rsqrt(var + LN_EPS) * w + b


def _rope_chunk(xc, cos, sin_signed, first_half):
    rot = jnp.where(first_half, pltpu.roll(xc, 96, 1), pltpu.roll(xc, 32, 1))
    return xc * cos + rot * sin_signed


def _rope(x, cos, sin_signed):
    tm, c = x.shape
    lane = lax.broadcasted_iota(I32, (tm, LANES), 1)
    first_half = (lane & 32) == 0
    chunks = [_rope_chunk(x[:, i * LANES:(i + 1) * LANES], cos, sin_signed, first_half)
              for i in range(c // LANES)]
    return chunks[0] if len(chunks) == 1 else jnp.concatenate(chunks, axis=1)


def _trig_kernel(pos_ref, inv_ref, cos_ref, sin_ref):
    pos = pos_ref[...].astype(F32)
    ang = inv_ref[...] * pos
    c = jnp.cos(ang)
    s = jnp.sin(ang)
    c4 = jnp.concatenate([c, c, c, c], axis=0)
    s4 = jnp.concatenate([-s, s, -s, s], axis=0)
    cos_ref[...] = c4.T
    sin_ref[...] = s4.T


def _rope_tables(positions, tm=512):
    t = positions.size
    half = HEAD_DIM // 2
    inv = ROPE_THETA ** (-jnp.arange(half, dtype=F32) / half)
    return pl.pallas_call(
        _trig_kernel,
        grid=(t // tm,),
        in_specs=[pl.BlockSpec((1, tm), lambda i: (0, i)),
                  pl.BlockSpec((half, 1), lambda i: (0, 0))],
        out_specs=[pl.BlockSpec((tm, LANES), lambda i: (i, 0)),
                   pl.BlockSpec((tm, LANES), lambda i: (i, 0))],
        out_shape=[jax.ShapeDtypeStruct((t, LANES), F32)] * 2,
        compiler_params=_params(("parallel",)),
        name="rope_tables",
    )(positions.reshape(1, t), inv.reshape(half, 1))


def _inproj0_kernel(x_ref, w_ref, cos_ref, sin_ref, u_ref, q_ref, k_ref, v_ref):
    xb = x_ref[...].astype(BF16)
    cos = cos_ref[...]
    sin = sin_ref[...]
    q0 = POOL_WIDTH
    k0 = q0 + SWA_Q_DIM
    v0 = k0 + SWA_KV_DIM
    u_ref[...] = jnp.dot(xb, w_ref[:, :q0], preferred_element_type=F32)
    q = jnp.dot(xb, w_ref[:, q0:k0], preferred_element_type=F32)
    q_ref[...] = (_rope(q, cos, sin) * QK_SCALE).astype(BF16)
    k = jnp.dot(xb, w_ref[:, k0:v0], preferred_element_type=F32)
    k_ref[...] = _rope(k, cos, sin).astype(BF16)
    v_ref[...] = jnp.dot(xb, w_ref[:, v0:], preferred_element_type=F32).astype(BF16)


def _inproj0(x2d, w_bf, cos_t, sin_t, tm=512):
    t, d = x2d.shape
    n_in = w_bf.shape[1]
    row = lambda i: (i, 0)
    return pl.pallas_call(
        _inproj0_kernel,
        grid=(t // tm,),
        in_specs=[pl.BlockSpec((tm, d), row),
                  pl.BlockSpec((d, n_in), lambda i: (0, 0)),
                  pl.BlockSpec((tm, LANES), row),
                  pl.BlockSpec((tm, LANES), row)],
        out_specs=[pl.BlockSpec((tm, POOL_WIDTH), row),
                   pl.BlockSpec((tm, SWA_Q_DIM), row),
                   pl.BlockSpec((tm, SWA_KV_DIM), row),
                   pl.BlockSpec((tm, SWA_KV_DIM), row)],
        out_shape=[jax.ShapeDtypeStruct((t, POOL_WIDTH), F32),
                   jax.ShapeDtypeStruct((t, SWA_Q_DIM), BF16),
                   jax.ShapeDtypeStruct((t, SWA_KV_DIM), BF16),
                   jax.ShapeDtypeStruct((t, SWA_KV_DIM), BF16)],
        compiler_params=_params(("parallel",)),
        name="inproj0",
    )(x2d, w_bf, cos_t, sin_t)


def _band_bias(rows, max_dist, key_lo):
    qi = lax.broadcasted_iota(I32, (rows, 2 * ATT_BLOCK), 0) & (ATT_BLOCK - 1)
    sj = lax.broadcasted_iota(I32, (rows, 2 * ATT_BLOCK), 1)
    dist = ATT_BLOCK + qi - sj
    return jnp.where((dist >= 0) & (dist <= max_dist) & (sj >= key_lo), 0.0, NEG_BIG)


def _band_mask(max_dist, key_lo):
    qi = lax.broadcasted_iota(I32, (ATT_BLOCK, 2 * ATT_BLOCK), 0)
    sj = lax.broadcasted_iota(I32, (ATT_BLOCK, 2 * ATT_BLOCK), 1)
    dist = ATT_BLOCK + qi - sj
    return (dist >= 0) & (dist <= max_dist) & (sj >= key_lo)


def _attn_block(q, kwin, vwin, valid, sink):
    s = lax.dot_general(q, kwin, (((1,), (1,)), ((), ())), preferred_element_type=F32)
    s = jnp.where(valid, s, NEG_BIG)
    m = jnp.maximum(jnp.max(s, axis=1, keepdims=True), sink)
    p = jnp.exp(s - m)
    den = jnp.sum(p, axis=1, keepdims=True) + jnp.exp(sink - m)
    o = jnp.dot(p.astype(BF16), vwin, preferred_element_type=F32)
    return o * (1.0 / den)


def _attn_stack(qs, kpart, vpart, bias):
    s = lax.dot_general(qs, kpart, (((1,), (1,)), ((), ())), preferred_element_type=F32) + bias
    m = jnp.max(s, axis=1, keepdims=True)
    p = jnp.exp(s - m)
    den = jnp.sum(p, axis=1, keepdims=True)
    return jnp.dot(p.astype(BF16), vpart, preferred_element_type=F32), m, den


def _kv_parts(kvx, h=0):
    base = h * KV_PARTS * LANES
    return [kvx[:, base + part * LANES:base + (part + 1) * LANES] for part in range(KV_PARTS)]


def _swa_kernel(sink_ref, q_ref, kp_ref, kc_ref, vp_ref, vc_ref, o_ref, *, tq):
    i = pl.program_id(1)
    kfull = jnp.concatenate([kp_ref[...], kc_ref[...]], axis=0)
    vfull = jnp.concatenate([vp_ref[...], vc_ref[...]], axis=0)
    g = SWA_Q_HEADS // SWA_KV_HEADS
    for j in range(tq // ATT_BLOCK):
        key_lo = jnp.where(i == 0, ATT_BLOCK, 0) if j == 0 else 0
        valid = _band_mask(SWA_WINDOW - 1, key_lo)
        r0 = j * ATT_BLOCK
        outs = []
        for h in range(SWA_Q_HEADS):
            kv = h // g
            q = q_ref[r0:r0 + ATT_BLOCK, h * HEAD_DIM:(h + 1) * HEAD_DIM]
            kwin = kfull[r0:r0 + 2 * ATT_BLOCK, kv * HEAD_DIM:(kv + 1) * HEAD_DIM]
            vwin = vfull[r0:r0 + 2 * ATT_BLOCK, kv * HEAD_DIM:(kv + 1) * HEAD_DIM]
            outs.append(_attn_block(q, kwin, vwin, valid, sink_ref[h]))
        o_ref[r0:r0 + ATT_BLOCK, :] = jnp.concatenate(outs, axis=1).astype(BF16)


def _swa(q, k, v, sinks, b, s, tq=256):
    per = tq // ATT_BLOCK
    cur = lambda bi, i: (bi, i, 0)
    prev = lambda bi, i: (bi, jnp.maximum(i * per - 1, 0), 0)
    q3 = q.reshape(b, s, SWA_Q_DIM)
    k3 = k.reshape(b, s, SWA_KV_DIM)
    v3 = v.reshape(b, s, SWA_KV_DIM)
    out = pl.pallas_call(
        functools.partial(_swa_kernel, tq=tq),
        grid=(b, s // tq),
        in_specs=[pl.BlockSpec(memory_space=pltpu.SMEM),
                  pl.BlockSpec((None, tq, SWA_Q_DIM), cur),
                  pl.BlockSpec((None, ATT_BLOCK, SWA_KV_DIM), prev),
                  pl.BlockSpec((None, tq, SWA_KV_DIM), cur),
                  pl.BlockSpec((None, ATT_BLOCK, SWA_KV_DIM), prev),
                  pl.BlockSpec((None, tq, SWA_KV_DIM), cur)],
        out_specs=pl.BlockSpec((None, tq, SWA_Q_DIM), cur),
        out_shape=jax.ShapeDtypeStruct((b, s, SWA_Q_DIM), BF16),
        compiler_params=_params(("parallel", "parallel")),
        name="swa_attention",
    )(sinks, q3, k3, k3, v3, v3)
    return out.reshape(b * s, SWA_Q_DIM)


def _pool_mixer(u, halo, seq_row0):
    tm = u.shape[0]
    full = jnp.concatenate([halo, u], axis=0)
    sums = [full]
    for shift in (1, 2, 4, 8):
        prev = sums[-1]
        sums.append(prev + pltpu.roll(prev, shift, 0))
    lane = lax.broadcasted_iota(I32, (tm, POOL_WIDTH), 1)
    row = lax.broadcasted_iota(I32, (tm, POOL_WIDTH), 0)
    grp = lane // POOL_GROUP
    win = sums[4][POOL_HALO:]
    width = jnp.full((tm, POOL_WIDTH), POOL_WINDOWS[3], I32)
    for gi in (2, 1, 0):
        win = jnp.where(grp == gi, sums[gi + 1][POOL_HALO:], win)
        width = jnp.where(grp == gi, POOL_WINDOWS[gi], width)
    count = jnp.minimum(seq_row0 + row + 1, width).astype(F32)
    return win / count - u


def _outproj0_kernel(u_ref, uh_ref, o_ref, x_ref, pw_ref, ps_ref, wa_ref, wb_ref, lw_ref, lb_ref, out_ref, *, tm):
    i = pl.program_id(1)
    halo = jnp.where(i == 0, 0.0, uh_ref[...])
    d = _pool_mixer(u_ref[...], halo, i * tm)
    a = jnp.dot(d.astype(BF16), pw_ref[...], preferred_element_type=F32) * ps_ref[...]
    mix = jnp.dot(a.astype(BF16), wa_ref[...], preferred_element_type=F32)
    mix = mix + jnp.dot(o_ref[...], wb_ref[...], preferred_element_type=F32)
    y = ALPHA * x_ref[...] + mix
    out_ref[...] = _layer_norm(y, lw_ref[...], lb_ref[...])


def _outproj0(u, attn, x2d, pool_bd, pool_scale, wa, wb, ln_w, ln_b, b, s, tm=512):
    d = x2d.shape[1]
    per = tm // POOL_HALO
    cur = lambda bi, i: (bi, i, 0)
    prev = lambda bi, i: (bi, jnp.maximum(i * per - 1, 0), 0)
    const = lambda bi, i: (0, 0)
    out = pl.pallas_call(
        functools.partial(_outproj0_kernel, tm=tm),
        grid=(b, s // tm),
        in_specs=[pl.BlockSpec((None, tm, POOL_WIDTH), cur),
                  pl.BlockSpec((None, POOL_HALO, POOL_WIDTH), prev),
                  pl.BlockSpec((None, tm, SWA_Q_DIM), cur),
                  pl.BlockSpec((None, tm, d), cur),
                  pl.BlockSpec((POOL_WIDTH, POOL_WIDTH), const),
                  pl.BlockSpec((1, POOL_WIDTH), const),
                  pl.BlockSpec((POOL_WIDTH, d), const),
                  pl.BlockSpec((SWA_Q_DIM, d), const),
                  pl.BlockSpec((1, d), const),
                  pl.BlockSpec((1, d), const)],
        out_specs=pl.BlockSpec((None, tm, d), cur),
        out_shape=jax.ShapeDtypeStruct((b, s, d), F32),
        compiler_params=_params(("parallel", "parallel")),
        name="outproj0",
    )(u.reshape(b, s, POOL_WIDTH), u.reshape(b, s, POOL_WIDTH), attn.reshape(b, s, SWA_Q_DIM),
      x2d.reshape(b, s, d), pool_bd, pool_scale, wa, wb, ln_w, ln_b)
    return out.reshape(b * s, d)


def _swiglu_tile(xb, wg, wu, wd):
    g = jnp.dot(xb, wg, preferred_element_type=F32)
    u = jnp.dot(xb, wu, preferred_element_type=F32)
    h = (g * jax.nn.sigmoid(g)) * u
    return jnp.dot(h.astype(BF16), wd, preferred_element_type=F32)


def _ffn_kernel(x_ref, wg_ref, wu_ref, wd_ref, lw_ref, lb_ref, out_ref):
    x = x_ref[...]
    ffn = _swiglu_tile(x.astype(BF16), wg_ref[...], wu_ref[...], wd_ref[...])
    out_ref[...] = _layer_norm(ALPHA * x + ffn, lw_ref[...], lb_ref[...])


def _ffn_dense(x2d, wg, wu, wd, ln_w, ln_b, tm=512):
    t, d = x2d.shape
    f = wg.shape[1]
    const = lambda i: (0, 0)
    return pl.pallas_call(
        _ffn_kernel,
        grid=(t // tm,),
        in_specs=[pl.BlockSpec((tm, d), lambda i: (i, 0)),
                  pl.BlockSpec((d, f), const),
                  pl.BlockSpec((d, f), const),
                  pl.BlockSpec((f, d), const),
                  pl.BlockSpec((1, d), const),
                  pl.BlockSpec((1, d), const)],
        out_specs=pl.BlockSpec((tm, d), lambda i: (i, 0)),
        out_shape=jax.ShapeDtypeStruct((t, d), F32),
        compiler_params=_params(("parallel",), vmem=56 * 1024 * 1024),
        name="ffn_dense",
    )(x2d, wg, wu, wd, ln_w, ln_b)


def _kv_operands(kv, low):
    swapped = pltpu.roll(kv, HEAD_DIM, 1)
    return (jnp.where(low, kv, 0.0), jnp.where(low, 0.0, swapped),
            jnp.where(low, swapped, 0.0), jnp.where(low, 0.0, kv))


def _inproj1_kernel(x_ref, wq_ref, wkv_ref, wc_ref, cw_ref, cos_ref, sin_ref,
                    q0_ref, q1_ref, q2_ref, kv0_ref, kv1_ref, kv2_ref, d_ref,
                    zc_ref, sq_ref, skv_ref, *, tm):
    i = pl.program_id(1)
    xb = x_ref[...].astype(BF16)
    cos = cos_ref[...]
    sin = sin_ref[...]
    lane = lax.broadcasted_iota(I32, (tm, LANES), 1)
    low = lane < HEAD_DIM
    cos_k = jnp.where(low, cos, 1.0)
    sin_k = jnp.where(low, sin, 0.0)
    q_refs = (q0_ref, q1_ref, q2_ref)
    kv_refs = (kv0_ref, kv1_ref, kv2_ref)
    for gi, (_, dil) in enumerate(DIL_PAIRS):
        q = jnp.dot(xb, wq_ref[gi], preferred_element_type=F32)
        q = _rope(q, cos, sin) * QK_SCALE
        kv = jnp.dot(xb, wkv_ref[gi], preferred_element_type=F32)
        kv = _rope(kv, cos_k, sin_k)
        kvx = [_kv_operands(kv[:, h * LANES:(h + 1) * LANES], low) for h in range(DIL_KV_HEADS)]
        if dil == 1:
            q_refs[gi][0] = q.astype(BF16)
            for h in range(DIL_KV_HEADS):
                kv_refs[gi][h, 0] = jnp.concatenate(kvx[h], axis=1).astype(BF16)
        else:
            n = tm // dil
            for c in range(DIL_Q_DIM // LANES):
                sq_ref[c] = q[:, c * LANES:(c + 1) * LANES]
            for h in range(DIL_KV_HEADS):
                for part in range(KV_PARTS):
                    skv_ref[h * KV_PARTS + part] = kvx[h][part]
            for r in range(dil):
                rows = pl.ds(r, n, stride=dil)
                q_refs[gi][r] = jnp.concatenate(
                    [sq_ref[c, rows, :] for c in range(DIL_Q_DIM // LANES)], axis=1).astype(BF16)
                for h in range(DIL_KV_HEADS):
                    kv_refs[gi][h, r] = jnp.concatenate(
                        [skv_ref[h * KV_PARTS + part, rows, :] for part in range(KV_PARTS)], axis=1).astype(BF16)

    hc = jnp.dot(xb, wc_ref[...], preferred_element_type=F32)
    z = hc[:, 2 * CONV_WIDTH:] * hc[:, :CONV_WIDTH]
    zprev = jnp.where(i == 0, 0.0, zc_ref[...])
    zfull = jnp.concatenate([zprev, z], axis=0)
    z1 = pltpu.roll(zfull, 1, 0)[CONV_HALO:]
    z2 = pltpu.roll(zfull, 2, 0)[CONV_HALO:]
    cw = cw_ref[...]
    y = cw[0:1] * z2 + cw[1:2] * z1 + cw[2:3] * z
    d_ref[...] = (hc[:, CONV_WIDTH:2 * CONV_WIDTH] * y).astype(BF16)
    zc_ref[...] = z[tm - CONV_HALO:]


def _inproj1(x2d, wq, wkv, wc, conv_w, cos_t, sin_t, b, s, tm=512):
    d = x2d.shape[1]
    cur = lambda bi, i: (bi, i, 0)
    row = lambda bi, i: (bi * (s // tm) + i, 0)
    in_specs = [pl.BlockSpec((None, tm, d), cur),
                pl.BlockSpec(wq.shape, lambda bi, i: (0, 0, 0)),
                pl.BlockSpec(wkv.shape, lambda bi, i: (0, 0, 0)),
                pl.BlockSpec(wc.shape, lambda bi, i: (0, 0)),
                pl.BlockSpec(conv_w.shape, lambda bi, i: (0, 0)),
                pl.BlockSpec((tm, LANES), row),
                pl.BlockSpec((tm, LANES), row)]
    out_specs, out_shape = [], []
    for _, dil in DIL_PAIRS:
        out_specs.append(pl.BlockSpec((None, dil, tm // dil, DIL_Q_DIM), lambda bi, i: (bi, 0, i, 0)))
        out_shape.append(jax.ShapeDtypeStruct((b, dil, s // dil, DIL_Q_DIM), BF16))
    for _, dil in DIL_PAIRS:
        out_specs.append(pl.BlockSpec((None, DIL_KV_HEADS, dil, tm // dil, KV_PARTS * LANES),
                                      lambda bi, i: (bi, 0, 0, i, 0)))
        out_shape.append(jax.ShapeDtypeStruct((b, DIL_KV_HEADS, dil, s // dil, KV_PARTS * LANES), BF16))
    out_specs.append(pl.BlockSpec((None, tm, CONV_WIDTH), cur))
    out_shape.append(jax.ShapeDtypeStruct((b, s, CONV_WIDTH), BF16))
    return pl.pallas_call(
        functools.partial(_inproj1_kernel, tm=tm),
        grid=(b, s // tm),
        in_specs=in_specs,
        out_specs=out_specs,
        out_shape=out_shape,
        scratch_shapes=[pltpu.VMEM((CONV_HALO, CONV_WIDTH), F32),
                        pltpu.VMEM((DIL_Q_DIM // LANES, tm, LANES), F32),
                        pltpu.VMEM((DIL_KV_HEADS * KV_PARTS, tm, LANES), F32)],
        compiler_params=_params(("parallel", "arbitrary")),
        name="inproj1",
    )(x2d.reshape(b, s, d), wq, wkv, wc, conv_w, cos_t, sin_t)


def _dil_kernel(q0_ref, q1_ref, q2_ref, kv0_ref, kv1_ref, kv2_ref, out_ref, o_run, l_run, bias_ref, *, s):
    q_refs = (q0_ref, q1_ref, q2_ref)
    kv_refs = (kv0_ref, kv1_ref, kv2_ref)
    n_blocks = s // ATT_BLOCK
    width = DIL_Q_PER_KV * HEAD_DIM
    max_dist = DIL_PAIRS[0][0] // DIL_PAIRS[0][1]
    assert all(w // d == max_dist for w, d in DIL_PAIRS)
    stacked = (width // LANES) * ATT_BLOCK
    bias_ref[0] = _band_bias(stacked, max_dist, 0)
    bias_ref[1] = _band_bias(stacked, max_dist, ATT_BLOCK)
    for gi, (window, dil) in enumerate(DIL_PAIRS):
        q_ref, kv_ref = q_refs[gi], kv_refs[gi]
        blocks_per_sub = (s // dil) // ATT_BLOCK
        shift = blocks_per_sub.bit_length() - 1

        def body(n, carry, q_ref=q_ref, kv_ref=kv_ref, gi=gi, window=window, dil=dil,
                 blocks_per_sub=blocks_per_sub, shift=shift):
            res = lax.shift_right_logical(n, shift)
            bi = n & (blocks_per_sub - 1)
            rows = pl.ds(pl.multiple_of(n * ATT_BLOCK, ATT_BLOCK), ATT_BLOCK)
            prow = pl.ds(pl.multiple_of(jnp.maximum(n - 1, 0) * ATT_BLOCK, ATT_BLOCK), ATT_BLOCK)
            kvx = jnp.concatenate([kv_ref[prow, :], kv_ref[rows, :]], axis=0)
            kz, zk, vz, zv = _kv_parts(kvx)
            q4 = q_ref[rows, :]
            n_chunks = width // LANES
            qs = jnp.concatenate([q4[:, c * LANES:(c + 1) * LANES] for c in range(n_chunks)], axis=0)
            bias = bias_ref[jnp.where(bi == 0, 1, 0)]
            oa, ma, da = _attn_stack(qs, kz, vz, bias)
            ob, mb, db = _attn_stack(qs, zk, zv, bias)
            low = lax.broadcasted_iota(I32, (n_chunks * ATT_BLOCK, LANES), 1) < HEAD_DIM
            o_st = (oa + ob) * jnp.where(low, 1.0 / da, 1.0 / db)
            l_st = jnp.where(low, ma + jnp.log(da), mb + jnp.log(db))
            start = res + dil * ATT_BLOCK * bi
            if dil == 1:
                tok = pl.ds(pl.multiple_of(start, ATT_BLOCK), ATT_BLOCK)
            else:
                tok = pl.ds(start, ATT_BLOCK, stride=dil)
            for c in range(n_chunks):
                o_c = o_st[c * ATT_BLOCK:(c + 1) * ATT_BLOCK]
                l_c = l_st[c * ATT_BLOCK:(c + 1) * ATT_BLOCK]
                if gi == 0:
                    o_run[c, tok, :] = o_c
                    l_run[c, tok, :] = l_c
                else:
                    o_old = o_run[c, tok, :]
                    l_old = l_run[c, tok, :]
                    m = jnp.maximum(l_old, l_c)
                    a = jnp.exp(l_old - m)
                    bb = jnp.exp(l_c - m)
                    tot = a + bb
                    o_run[c, tok, :] = (o_old * a + o_c * bb) * (1.0 / tot)
                    if gi < len(DIL_PAIRS) - 1:
                        l_run[c, tok, :] = m + jnp.log(tot)
            return carry

        lax.fori_loop(0, n_blocks, body, 0, unroll=4)
    for c in range(width // LANES):
        out_ref[:, c * LANES:(c + 1) * LANES] = o_run[c].astype(BF16)


def _dilated(qs, kvs, b, s):
    width = DIL_Q_PER_KV * HEAD_DIM
    in_specs = [pl.BlockSpec((None, s, width), lambda bi, h: (bi, 0, h)) for _ in DIL_PAIRS]
    in_specs += [pl.BlockSpec((None, None, s, KV_PARTS * LANES), lambda bi, h: (bi, h, 0, 0)) for _ in DIL_PAIRS]
    qs = [q.reshape(b, s, DIL_Q_DIM) for q in qs]
    kvs = [kv.reshape(b, DIL_KV_HEADS, s, KV_PARTS * LANES) for kv in kvs]
    out = pl.pallas_call(
        functools.partial(_dil_kernel, s=s),
        grid=(b, DIL_KV_HEADS),
        in_specs=in_specs,
        out_specs=pl.BlockSpec((None, s, width), lambda bi, h: (bi, 0, h)),
        out_shape=jax.ShapeDtypeStruct((b, s, DIL_Q_DIM), BF16),
        scratch_shapes=[pltpu.VMEM((width // LANES, s, LANES), F32),
                        pltpu.VMEM((width // LANES, s, LANES), F32),
                        pltpu.VMEM((2, (width // LANES) * ATT_BLOCK, 2 * ATT_BLOCK), F32)],
        compiler_params=_params(("parallel", "parallel"), vmem=56 * 1024 * 1024),
        name="dilated_attention",
    )(*qs, *kvs)
    return out.reshape(b * s, DIL_Q_DIM)


def _to_slabs(ref, val):
    rows, d = val.shape
    per = d // LANES
    for c in range(per):
        ref[pl.ds(c, rows, stride=per), :] = val[:, c * LANES:(c + 1) * LANES]


def _from_slabs(ref, rows):
    per = ref.shape[0] // rows
    return jnp.concatenate([ref[pl.ds(c, rows, stride=per), :] for c in range(per)], axis=1)


def _outproj1_kernel(c_ref, d_ref, x_ref, wa_ref, wb_ref, lw_ref, lb_ref, out_ref, slab_ref):
    mix = jnp.dot(c_ref[...], wa_ref[...], preferred_element_type=F32)
    mix = mix + jnp.dot(d_ref[...], wb_ref[...], preferred_element_type=F32)
    y = _layer_norm(ALPHA * x_ref[...] + mix, lw_ref[...], lb_ref[...])
    out_ref[...] = y
    _to_slabs(slab_ref, y)


def _outproj1(c, dconv, x2d, wa, wb, ln_w, ln_b, tm=512):
    t, d = x2d.shape
    row = lambda i: (i, 0)
    const = lambda i: (0, 0)
    return pl.pallas_call(
        _outproj1_kernel,
        grid=(t // tm,),
        in_specs=[pl.BlockSpec((tm, DIL_Q_DIM), row),
                  pl.BlockSpec((tm, CONV_WIDTH), row),
                  pl.BlockSpec((tm, d), row),
                  pl.BlockSpec((DIL_Q_DIM, d), const),
                  pl.BlockSpec((CONV_WIDTH, d), const),
                  pl.BlockSpec((1, d), const),
                  pl.BlockSpec((1, d), const)],
        out_specs=[pl.BlockSpec((tm, d), row), pl.BlockSpec((tm * (d // LANES), LANES), row)],
        out_shape=[jax.ShapeDtypeStruct((t, d), F32), jax.ShapeDtypeStruct((t * (d // LANES), LANES), F32)],
        compiler_params=_params(("parallel",)),
        name="outproj1",
    )(c, dconv, x2d, wa, wb, ln_w, ln_b)


def _route_kernel(x_ref, wr_ref, e_ref, g_ref, r_ref, cnt_ref, run_ref, *, tm):
    i = pl.program_id(0)

    @pl.when(i == 0)
    def _():
        run_ref[...] = jnp.zeros_like(run_ref)

    logits = lax.dot_general(wr_ref[...], x_ref[...], (((1,), (1,)), ((), ())),
                             precision=lax.Precision.HIGHEST, preferred_element_type=F32)
    eid = lax.broadcasted_iota(I32, (N_EXPERTS, tm), 0)
    m1 = jnp.max(logits, axis=0, keepdims=True)
    i1 = jnp.min(jnp.where(logits == m1, eid, N_EXPERTS), axis=0, keepdims=True)
    rest = jnp.where(eid == i1, -jnp.inf, logits)
    m2 = jnp.max(rest, axis=0, keepdims=True)
    i2 = jnp.min(jnp.where(rest == m2, eid, N_EXPERTS), axis=0, keepdims=True)
    t2 = jnp.exp(m2 - m1)
    g1 = 1.0 / (1.0 + t2)
    g2 = t2 / (1.0 + t2)
    oh1 = eid == i1
    oh2 = eid == i2
    oh = jnp.where(oh1 | oh2, 1.0, 0.0)
    ri = lax.broadcasted_iota(I32, (tm, tm), 0)
    ci = lax.broadcasted_iota(I32, (tm, tm), 1)
    tri = jnp.where(ri <= ci, 1.0, 0.0).astype(BF16)
    incl = jnp.dot(oh.astype(BF16), tri, preferred_element_type=F32)
    rank = run_ref[:, 0:1] + incl - oh
    r1 = jnp.sum(jnp.where(oh1, rank, 0.0), axis=0, keepdims=True)
    r2 = jnp.sum(jnp.where(oh2, rank, 0.0), axis=0, keepdims=True)
    e_ref[0:1, :] = i1
    e_ref[1:2, :] = i2
    g_ref[0:1, :] = g1
    g_ref[1:2, :] = g2
    r_ref[0:1, :] = r1.astype(I32)
    r_ref[1:2, :] = r2.astype(I32)
    run_ref[...] = run_ref[...] + incl[:, tm - 1:tm]
    cnt_ref[...] = run_ref[...]


def _route(x2d, wr_t, tm=512):
    t, d = x2d.shape
    col = lambda i: (0, i)
    return pl.pallas_call(
        functools.partial(_route_kernel, tm=tm),
        grid=(t // tm,),
        in_specs=[pl.BlockSpec((tm, d), lambda i: (i, 0)),
                  pl.BlockSpec((N_EXPERTS, d), lambda i: (0, 0))],
        out_specs=[pl.BlockSpec((2, tm), col), pl.BlockSpec((2, tm), col), pl.BlockSpec((2, tm), col),
                   pl.BlockSpec((N_EXPERTS, LANES), lambda i: (0, 0))],
        out_shape=[jax.ShapeDtypeStruct((2, t), I32), jax.ShapeDtypeStruct((2, t), F32),
                   jax.ShapeDtypeStruct((2, t), I32), jax.ShapeDtypeStruct((N_EXPERTS, LANES), F32)],
        scratch_shapes=[pltpu.VMEM((N_EXPERTS, LANES), F32)],
        compiler_params=_params(("arbitrary",)),
        name="moe_route",
    )(x2d, wr_t)


ROW_SLAB = 8


def _slab_copy(src_ref, dst_ref, sem, src_row, dst_row, n=1):
    src = src_ref.at[pl.ds(pl.multiple_of(src_row * ROW_SLAB, ROW_SLAB), n * ROW_SLAB)]
    dst = dst_ref.at[pl.ds(pl.multiple_of(dst_row * ROW_SLAB, ROW_SLAB), n * ROW_SLAB)]
    return pltpu.make_async_copy(src, dst, sem)


def _moe_kernel(be_ref, nv_ref, tok_ref, tokn_ref, x_hbm, wg_ref, wu_ref, wd_ref, out_ref,
                xs_ref, xb_ref, acc_ref, sem, *, tm, nj):
    i = pl.program_id(0)
    j = pl.program_id(1)
    n_valid = nv_ref[0]
    slot = i & 1
    per_step = tm // nj

    def row_copy(tok, dst_slot, r):
        return _slab_copy(x_hbm, xs_ref, sem.at[dst_slot], tok, dst_slot * tm + r)

    @pl.when(i < n_valid)
    def _():
        @pl.when(j == 0)
        def _():
            @pl.when(i == 0)
            def _():
                def start(r, c):
                    row_copy(tok_ref[0, 0, r], 0, r).start()
                    return c
                lax.fori_loop(0, tm, start, 0)

            _slab_copy(x_hbm, xs_ref, sem.at[slot], 0, slot * tm, n=tm).wait()
            base = slot * (tm * ROW_SLAB)
            xb_ref[...] = jnp.concatenate(
                [xs_ref[pl.ds(base + c, tm, stride=ROW_SLAB), :] for c in range(ROW_SLAB)], axis=1).astype(BF16)
            acc_ref[...] = jnp.zeros_like(acc_ref)

        for r in range(per_step):
            rr = j * per_step + r
            row_copy(tokn_ref[0, 0, rr], 1 - slot, rr).start()

        acc_ref[...] += _swiglu_tile(xb_ref[...], wg_ref[...], wu_ref[...], wd_ref[...])

        @pl.when(j == nj - 1)
        def _():
            _to_slabs(out_ref, acc_ref[...])

            @pl.when(i + 1 >= n_valid)
            def _():
                _slab_copy(x_hbm, xs_ref, sem.at[1 - slot], 0, (1 - slot) * tm, n=tm).wait()

    @pl.when((i >= nv_ref[0]) & (j == pl.num_programs(1) - 1))
    def _():
        out_ref[...] = jnp.zeros_like(out_ref)


def _moe_experts(x_slabs, row_tok, block_e, n_valid, wg, wu, wd, tm, tf=1792):
    d = wg.shape[1]
    n_rows = row_tok.shape[0]
    n_blocks = n_rows // tm
    f = wg.shape[2]
    nj = f // tf

    def live(i, j, be, nv):
        ok = i < nv[0]
        return jnp.where(ok, i, nv[0] - 1), jnp.where(ok, j, nj - 1)

    def w_map(i, j, be, nv):
        ii, jj = live(i, j, be, nv)
        return be[ii], 0, jj

    def wd_map(i, j, be, nv):
        ii, jj = live(i, j, be, nv)
        return be[ii], jj, 0

    grid_spec = pltpu.PrefetchScalarGridSpec(
        num_scalar_prefetch=2,
        grid=(n_blocks, nj),
        in_specs=[pl.BlockSpec((1, 1, tm), lambda i, j, be, nv: (i, 0, 0), memory_space=pltpu.SMEM),
                  pl.BlockSpec((1, 1, tm), lambda i, j, be, nv: (jnp.minimum(i + 1, n_blocks - 1), 0, 0),
                               memory_space=pltpu.SMEM),
                  pl.BlockSpec(memory_space=pl.ANY),
                  pl.BlockSpec((None, d, tf), w_map),
                  pl.BlockSpec((None, d, tf), w_map),
                  pl.BlockSpec((None, tf, d), wd_map)],
        out_specs=pl.BlockSpec((tm * ROW_SLAB, LANES), lambda i, j, be, nv: (i, 0)),
        scratch_shapes=[pltpu.VMEM((2 * tm * ROW_SLAB, LANES), F32),
                        pltpu.VMEM((tm, d), BF16),
                        pltpu.VMEM((tm, d), F32),
                        pltpu.SemaphoreType.DMA((2,))],
    )
    tok3 = row_tok.reshape(n_blocks, 1, tm)
    return pl.pallas_call(
        functools.partial(_moe_kernel, tm=tm, nj=nj),
        grid_spec=grid_spec,
        out_shape=jax.ShapeDtypeStruct((n_rows * ROW_SLAB, LANES), F32),
        compiler_params=_params(("arbitrary", "arbitrary")),
        name="moe_experts",
    )(block_e, n_valid, tok3, tok3, x_slabs, wg, wu, wd)


def _combine_kernel(d1_ref, d2_ref, y_ref, x_ref, g1_ref, g2_ref, lw_ref, lb_ref, out_ref, ya, yb, sem, *, tm):
    def start(r, c):
        _slab_copy(y_ref, ya, sem.at[0], d1_ref[0, 0, r], r).start(priority=0)
        _slab_copy(y_ref, yb, sem.at[1], d2_ref[0, 0, r], r).start(priority=1)
        return c

    lax.fori_loop(0, tm, start, 0)
    _slab_copy(y_ref, ya, sem.at[0], 0, 0, n=tm).wait()
    _slab_copy(y_ref, yb, sem.at[1], 0, 0, n=tm).wait()
    ffn = _from_slabs(ya, tm) * g1_ref[...] + _from_slabs(yb, tm) * g2_ref[...]
    out_ref[...] = _layer_norm(ALPHA * x_ref[...] + ffn, lw_ref[...], lb_ref[...])


def _combine(y, dest1, dest2, x2d, g1, g2, ln_w, ln_b, tm=256):
    t, d = x2d.shape
    row = lambda i: (i, 0)
    smem = lambda: pl.BlockSpec((1, 1, tm), lambda i: (i, 0, 0), memory_space=pltpu.SMEM)
    return pl.pallas_call(
        functools.partial(_combine_kernel, tm=tm),
        grid=(t // tm,),
        in_specs=[smem(), smem(),
                  pl.BlockSpec(memory_space=pl.ANY),
                  pl.BlockSpec((tm, d), row),
                  pl.BlockSpec((tm, 1), row),
                  pl.BlockSpec((tm, 1), row),
                  pl.BlockSpec((1, d), lambda i: (0, 0)),
                  pl.BlockSpec((1, d), lambda i: (0, 0))],
        out_specs=pl.BlockSpec((tm, d), row),
        out_shape=jax.ShapeDtypeStruct((t, d), F32),
        scratch_shapes=[pltpu.VMEM((tm * ROW_SLAB, LANES), F32), pltpu.VMEM((tm * ROW_SLAB, LANES), F32),
                        pltpu.SemaphoreType.DMA((2,))],
        compiler_params=_params(("arbitrary",)),
        name="moe_combine",
    )(dest1.reshape(t // tm, 1, tm), dest2.reshape(t // tm, 1, tm), y, x2d,
      g1.reshape(t, 1), g2.reshape(t, 1), ln_w, ln_b)


def _moe_layer(x2d, xb, router_w, wg, wu, wd, ln_w, ln_b, tmoe=512):
    t, d = x2d.shape
    e_sel, gates, ranks, counts = _route(x2d, router_w.T)
    counts = counts[:, 0].astype(I32)
    padded = (counts + tmoe - 1) // tmoe * tmoe
    pend = jnp.cumsum(padded)
    pstart = pend - padded
    n_blocks = -(-(2 * t + N_EXPERTS * (tmoe - 1)) // tmoe)
    n_rows = n_blocks * tmoe
    first_row = sum(jnp.where(e_sel == e, pstart[e], 0) for e in range(N_EXPERTS))
    dest = first_row + ranks
    tok = jnp.arange(t, dtype=I32)
    row_tok = jnp.zeros((n_rows,), I32).at[dest[0]].set(tok).at[dest[1]].set(tok)
    block_row0 = jnp.arange(n_blocks, dtype=I32) * tmoe
    block_e = jnp.minimum(jnp.sum((block_row0[:, None] >= pend[None, :]).astype(I32), axis=1), N_EXPERTS - 1)
    n_valid = (pend[-1:] // tmoe).astype(I32)
    y = _moe_experts(xb, row_tok, block_e, n_valid, wg, wu, wd, tmoe)
    return _combine(y, dest[0], dest[1], x2d, gates[0], gates[1], ln_w, ln_b)


def kernel(x, positions, ln_w, ln_b, even_w_in, pool_w, pool_scale, swa_sinks, even_w_out, ffn_w_gate, ffn_w_up, ffn_w_down, odd_w_in, conv_w, odd_w_out, router_w, moe_w_gate, moe_w_up, moe_w_down):
    b, s, d = x.shape
    t = b * s
    x2d = x.reshape(t, d)
    cos_t, sin_t = _rope_tables(positions)
    ln = lambda layer, k: (ln_w[layer, k].reshape(1, d), ln_b[layer, k].reshape(1, d))

    u, q, k, v = _inproj0(x2d, even_w_in[0].astype(BF16), cos_t, sin_t)
    attn = _swa(q, k, v, swa_sinks[0], b, s)
    pool_bd = jax.scipy.linalg.block_diag(*[pool_w[0, gi] for gi in range(len(POOL_WINDOWS))]).astype(BF16)
    w_out0 = even_w_out[0].astype(BF16)
    x2d = _outproj0(u, attn, x2d, pool_bd, pool_scale[0].reshape(1, POOL_WIDTH),
                    w_out0[:POOL_WIDTH], w_out0[POOL_WIDTH:], *ln(0, 0), b, s)
    x2d = _ffn_dense(x2d, ffn_w_gate[0].astype(BF16), ffn_w_up[0].astype(BF16), ffn_w_down[0].astype(BF16),
                     *ln(0, 1))

    w_in1 = odd_w_in[0].astype(BF16)
    c_in = len(DIL_PAIRS) * DIL_GROUP_IN
    wq, wkv = [], []
    for gi in range(len(DIL_PAIRS)):
        g0 = gi * DIL_GROUP_IN
        wq.append(w_in1[:, g0:g0 + DIL_Q_DIM])
        kcol = g0 + DIL_Q_DIM
        vcol = kcol + DIL_KV_DIM
        parts = []
        for h in range(DIL_KV_HEADS):
            parts += [w_in1[:, kcol + h * HEAD_DIM:kcol + (h + 1) * HEAD_DIM],
                      w_in1[:, vcol + h * HEAD_DIM:vcol + (h + 1) * HEAD_DIM]]
        wkv.append(jnp.concatenate(parts, axis=1))
    outs = _inproj1(x2d, jnp.stack(wq), jnp.stack(wkv), w_in1[:, c_in:], conv_w[0], cos_t, sin_t, b, s)
    c_out = _dilated(outs[0:3], outs[3:6], b, s)
    w_out1 = odd_w_out[0].astype(BF16)
    x2d, xb = _outproj1(c_out, outs[6].reshape(t, CONV_WIDTH), x2d, w_out1[:DIL_Q_DIM], w_out1[DIL_Q_DIM:],
                        *ln(1, 0))
    x2d = _moe_layer(x2d, xb, router_w[0], moe_w_gate[0].astype(BF16), moe_w_up[0].astype(BF16),
                     moe_w_down[0].astype(BF16), *ln(1, 1))
    return x2d.reshape(b, s, d)
```

```python
import functools
import math

import jax
import jax.numpy as jnp
from jax import lax
from jax.experimental import pallas as pl
from jax.experimental.pallas import tpu as pltpu

F32 = jnp.float32
BF16 = jnp.bfloat16
I32 = jnp.int32

HEAD_DIM = 64
ROPE_THETA = 10000.0
ATT_BLOCK = 128
LN_EPS = 1e-5
POOL_WINDOWS = (2, 4, 8, 16)
POOL_GROUP = 64
POOL_WIDTH = 256
POOL_HALO = 16
SWA_WINDOW = 128
SWA_Q_HEADS = 12
SWA_KV_HEADS = 4
SWA_Q_DIM = SWA_Q_HEADS * HEAD_DIM
SWA_KV_DIM = SWA_KV_HEADS * HEAD_DIM
DIL_PAIRS = ((128, 1), (512, 4), (2048, 16))
DIL_Q_HEADS = 8
DIL_KV_HEADS = 2
DIL_Q_DIM = DIL_Q_HEADS * HEAD_DIM
DIL_KV_DIM = DIL_KV_HEADS * HEAD_DIM
DIL_GROUP_IN = DIL_Q_DIM + 2 * DIL_KV_DIM
DIL_Q_PER_KV = DIL_Q_HEADS // DIL_KV_HEADS
CONV_WIDTH = 512
CONV_K = 3
CONV_HALO = 8
N_EXPERTS = 8
DEPTH = 2
ALPHA = (2 * DEPTH) ** 0.25
QK_SCALE = 1.0 / math.sqrt(HEAD_DIM)
NEG_BIG = -1e30
KV_PARTS = 4

LANES = 128
ROW_SLAB = 8
VMEM_LIMIT = 48 * 1024 * 1024


def _params(sem, vmem=VMEM_LIMIT):
    return pltpu.CompilerParams(dimension_semantics=sem, vmem_limit_bytes=vmem)


def _layer_norm(y, w, b):
    mu = jnp.mean(y, axis=-1, keepdims=True)
    yc = y - mu
    var = jnp.mean(yc * yc, axis=-1, keepdims=True)
    return yc * lax.rsqrt(var + LN_EPS) * w + b


def _rope_chunk(xc, cos, sin_signed, first_half):
    rot = jnp.where(first_half, pltpu.roll(xc, 96, 1), pltpu.roll(xc, 32, 1))
    return xc * cos + rot * sin_signed


def _rope(x, cos, sin_signed):
    tm, c = x.shape
    lane = lax.broadcasted_iota(I32, (tm, LANES), 1)
    first_half = (lane & 32) == 0
    chunks = [_rope_chunk(x[:, i * LANES:(i + 1) * LANES], cos, sin_signed, first_half)
              for i in range(c // LANES)]
    return chunks[0] if len(chunks) == 1 else jnp.concatenate(chunks, axis=1)


def _trig_kernel(pos_ref, inv_ref, cos_ref, sin_ref):
    pos = pos_ref[...].astype(F32)
    ang = inv_ref[...] * pos
    c = jnp.cos(ang)
    s = jnp.sin(ang)
    c4 = jnp.concatenate([c, c, c, c], axis=0)
    s4 = jnp.concatenate([-s, s, -s, s], axis=0)
    cos_ref[...] = c4.T
    sin_ref[...] = s4.T


def _rope_tables(positions, tm=512):
    t = positions.size
    half = HEAD_DIM // 2
    inv = ROPE_THETA ** (-jnp.arange(half, dtype=F32) / half)
    return pl.pallas_call(
        _trig_kernel,
        grid=(t // tm,),
        in_specs=[pl.BlockSpec((1, tm), lambda i: (0, i)),
                  pl.BlockSpec((half, 1), lambda i: (0, 0))],
        out_specs=[pl.BlockSpec((tm, LANES), lambda i: (i, 0)),
                   pl.BlockSpec((tm, LANES), lambda i: (i, 0))],
        out_shape=[jax.ShapeDtypeStruct((t, LANES), F32)] * 2,
        compiler_params=_params(("parallel",)),
        name="rope_tables",
    )(positions.reshape(1, t), inv.reshape(half, 1))


def _inproj0_kernel(x_ref, w_ref, cos_ref, sin_ref, u_ref, q_ref, k_ref, v_ref):
    xb = x_ref[...].astype(BF16)
    cos = cos_ref[...]
    sin = sin_ref[...]
    q0 = POOL_WIDTH
    k0 = q0 + SWA_Q_DIM
    v0 = k0 + SWA_KV_DIM
    u_ref[...] = jnp.dot(xb, w_ref[:, :q0], preferred_element_type=F32)
    q = jnp.dot(xb, w_ref[:, q0:k0], preferred_element_type=F32)
    q_ref[...] = (_rope(q, cos, sin) * QK_SCALE).astype(BF16)
    k = jnp.dot(xb, w_ref[:, k0:v0], preferred_element_type=F32)
    k_ref[...] = _rope(k, cos, sin).astype(BF16)
    v_ref[...] = jnp.dot(xb, w_ref[:, v0:], preferred_element_type=F32).astype(BF16)


def _inproj0(x2d, w_bf, cos_t, sin_t, tm=512):
    t, d = x2d.shape
    n_in = w_bf.shape[1]
    row = lambda i: (i, 0)
    return pl.pallas_call(
        _inproj0_kernel,
        grid=(t // tm,),
        in_specs=[pl.BlockSpec((tm, d), row),
                  pl.BlockSpec((d, n_in), lambda i: (0, 0)),
                  pl.BlockSpec((tm, LANES), row),
                  pl.BlockSpec((tm, LANES), row)],
        out_specs=[pl.BlockSpec((tm, POOL_WIDTH), row),
                   pl.BlockSpec((tm, SWA_Q_DIM), row),
                   pl.BlockSpec((tm, SWA_KV_DIM), row),
                   pl.BlockSpec((tm, SWA_KV_DIM), row)],
        out_shape=[jax.ShapeDtypeStruct((t, POOL_WIDTH), F32),
                   jax.ShapeDtypeStruct((t, SWA_Q_DIM), BF16),
                   jax.ShapeDtypeStruct((t, SWA_KV_DIM), BF16),
                   jax.ShapeDtypeStruct((t, SWA_KV_DIM), BF16)],
        compiler_params=_params(("parallel",)),
        name="inproj0",
    )(x2d, w_bf, cos_t, sin_t)


def _band_bias(rows, max_dist, key_lo):
    qi = lax.broadcasted_iota(I32, (rows, 2 * ATT_BLOCK), 0) & (ATT_BLOCK - 1)
    sj = lax.broadcasted_iota(I32, (rows, 2 * ATT_BLOCK), 1)
    dist = ATT_BLOCK + qi - sj
    return jnp.where((dist >= 0) & (dist <= max_dist) & (sj >= key_lo), 0.0, NEG_BIG)


def _band_mask(max_dist, key_lo):
    qi = lax.broadcasted_iota(I32, (ATT_BLOCK, 2 * ATT_BLOCK), 0)
    sj = lax.broadcasted_iota(I32, (ATT_BLOCK, 2 * ATT_BLOCK), 1)
    dist = ATT_BLOCK + qi - sj
    return (dist >= 0) & (dist <= max_dist) & (sj >= key_lo)


def _attn_block(q, kwin, vwin, valid, sink):
    s = lax.dot_general(q, kwin, (((1,), (1,)), ((), ())), preferred_element_type=F32)
    s = jnp.where(valid, s, NEG_BIG)
    m = jnp.maximum(jnp.max(s, axis=1, keepdims=True), sink)
    p = jnp.exp(s - m)
    den = jnp.sum(p, axis=1, keepdims=True) + jnp.exp(sink - m)
    o = jnp.dot(p.astype(BF16), vwin, preferred_element_type=F32)
    return o * (1.0 / den)


def _attn_stack(qs, kpart, vpart, bias):
    s = lax.dot_general(qs, kpart, (((1,), (1,)), ((), ())), preferred_element_type=F32) + bias
    m = jnp.max(s, axis=1, keepdims=True)
    p = jnp.exp(s - m)
    den = jnp.sum(p, axis=1, keepdims=True)
    return jnp.dot(p.astype(BF16), vpart, preferred_element_type=F32), m, den


def _kv_parts(kvx, h=0):
    base = h * KV_PARTS * LANES
    return [kvx[:, base + part * LANES:base + (part + 1) * LANES] for part in range(KV_PARTS)]


def _swa_kernel(sink_ref, q_ref, kp_ref, kc_ref, vp_ref, vc_ref, o_ref, *, tq):
    i = pl.program_id(1)
    kfull = jnp.concatenate([kp_ref[...], kc_ref[...]], axis=0)
    vfull = jnp.concatenate([vp_ref[...], vc_ref[...]], axis=0)
    g = SWA_Q_HEADS // SWA_KV_HEADS
    for j in range(tq // ATT_BLOCK):
        key_lo = jnp.where(i == 0, ATT_BLOCK, 0) if j == 0 else 0
        valid = _band_mask(SWA_WINDOW - 1, key_lo)
        r0 = j * ATT_BLOCK
        outs = []
        for h in range(SWA_Q_HEADS):
            kv = h // g
            q = q_ref[r0:r0 + ATT_BLOCK, h * HEAD_DIM:(h + 1) * HEAD_DIM]
            kwin = kfull[r0:r0 + 2 * ATT_BLOCK, kv * HEAD_DIM:(kv + 1) * HEAD_DIM]
            vwin = vfull[r0:r0 + 2 * ATT_BLOCK, kv * HEAD_DIM:(kv + 1) * HEAD_DIM]
            outs.append(_attn_block(q, kwin, vwin, valid, sink_ref[h]))
        o_ref[r0:r0 + ATT_BLOCK, :] = jnp.concatenate(outs, axis=1).astype(BF16)


def _swa(q, k, v, sinks, b, s, tq=256):
    per = tq // ATT_BLOCK
    cur = lambda bi, i: (bi, i, 0)
    prev = lambda bi, i: (bi, jnp.maximum(i * per - 1, 0), 0)
    q3 = q.reshape(b, s, SWA_Q_DIM)
    k3 = k.reshape(b, s, SWA_KV_DIM)
    v3 = v.reshape(b, s, SWA_KV_DIM)
    out = pl.pallas_call(
        functools.partial(_swa_kernel, tq=tq),
        grid=(b, s // tq),
        in_specs=[pl.BlockSpec(memory_space=pltpu.SMEM),
                  pl.BlockSpec((None, tq, SWA_Q_DIM), cur),
                  pl.BlockSpec((None, ATT_BLOCK, SWA_KV_DIM), prev),
                  pl.BlockSpec((None, tq, SWA_KV_DIM), cur),
                  pl.BlockSpec((None, ATT_BLOCK, SWA_KV_DIM), prev),
                  pl.BlockSpec((None, tq, SWA_KV_DIM), cur)],
        out_specs=pl.BlockSpec((None, tq, SWA_Q_DIM), cur),
        out_shape=jax.ShapeDtypeStruct((b, s, SWA_Q_DIM), BF16),
        compiler_params=_params(("parallel", "parallel")),
        name="swa_attention",
    )(sinks, q3, k3, k3, v3, v3)
    return out.reshape(b * s, SWA_Q_DIM)


def _pool_mixer(u, halo, seq_row0):
    tm = u.shape[0]
    full = jnp.concatenate([halo, u], axis=0)
    sums = [full]
    for shift in (1, 2, 4, 8):
        prev = sums[-1]
        sums.append(prev + pltpu.roll(prev, shift, 0))
    lane = lax.broadcasted_iota(I32, (tm, POOL_WIDTH), 1)
    row = lax.broadcasted_iota(I32, (tm, POOL_WIDTH), 0)
    grp = lane // POOL_GROUP
    win = sums[4][POOL_HALO:]
    width = jnp.full((tm, POOL_WIDTH), POOL_WINDOWS[3], I32)
    for gi in (2, 1, 0):
        win = jnp.where(grp == gi, sums[gi + 1][POOL_HALO:], win)
        width = jnp.where(grp == gi, POOL_WINDOWS[gi], width)
    count = jnp.minimum(seq_row0 + row + 1, width).astype(F32)
    return win / count - u


def _outproj0_kernel(u_ref, uh_ref, o_ref, x_ref, pw_ref, ps_ref, wa_ref, wb_ref, lw_ref, lb_ref, out_ref, *, tm):
    i = pl.program_id(1)
    halo = jnp.where(i == 0, 0.0, uh_ref[...])
    d = _pool_mixer(u_ref[...], halo, i * tm)
    a = jnp.dot(d.astype(BF16), pw_ref[...], preferred_element_type=F32) * ps_ref[...]
    mix = jnp.dot(a.astype(BF16), wa_ref[...], preferred_element_type=F32)
    mix = mix + jnp.dot(o_ref[...], wb_ref[...], preferred_element_type=F32)
    y = ALPHA * x_ref[...] + mix
    out_ref[...] = _layer_norm(y, lw_ref[...], lb_ref[...])


def _outproj0(u, attn, x2d, pool_bd, pool_scale, wa, wb, ln_w, ln_b, b, s, tm=512):
    d = x2d.shape[1]
    per = tm // POOL_HALO
    cur = lambda bi, i: (bi, i, 0)
    prev = lambda bi, i: (bi, jnp.maximum(i * per - 1, 0), 0)
    const = lambda bi, i: (0, 0)
    out = pl.pallas_call(
        functools.partial(_outproj0_kernel, tm=tm),
        grid=(b, s // tm),
        in_specs=[pl.BlockSpec((None, tm, POOL_WIDTH), cur),
                  pl.BlockSpec((None, POOL_HALO, POOL_WIDTH), prev),
                  pl.BlockSpec((None, tm, SWA_Q_DIM), cur),
                  pl.BlockSpec((None, tm, d), cur),
                  pl.BlockSpec((POOL_WIDTH, POOL_WIDTH), const),
                  pl.BlockSpec((1, POOL_WIDTH), const),
                  pl.BlockSpec((POOL_WIDTH, d), const),
                  pl.BlockSpec((SWA_Q_DIM, d), const),
                  pl.BlockSpec((1, d), const),
                  pl.BlockSpec((1, d), const)],
        out_specs=pl.BlockSpec((None, tm, d), cur),
        out_shape=jax.ShapeDtypeStruct((b, s, d), F32),
        compiler_params=_params(("parallel", "parallel")),
        name="outproj0",
    )(u.reshape(b, s, POOL_WIDTH), u.reshape(b, s, POOL_WIDTH), attn.reshape(b, s, SWA_Q_DIM),
      x2d.reshape(b, s, d), pool_bd, pool_scale, wa, wb, ln_w, ln_b)
    return out.reshape(b * s, d)


def _swiglu_tile(xb, wg, wu, wd):
    g = jnp.dot(xb, wg, preferred_element_type=F32)
    u = jnp.dot(xb, wu, preferred_element_type=F32)
    h = (g * jax.nn.sigmoid(g)) * u
    return jnp.dot(h.astype(BF16), wd, preferred_element_type=F32)


def _ffn_kernel(x_ref, wg_ref, wu_ref, wd_ref, lw_ref, lb_ref, out_ref):
    x = x_ref[...]
    ffn = _swiglu_tile(x.astype(BF16), wg_ref[...], wu_ref[...], wd_ref[...])
    out_ref[...] = _layer_norm(ALPHA * x + ffn, lw_ref[...], lb_ref[...])


def _ffn_dense(x2d, wg, wu, wd, ln_w, ln_b, tm=512):
    t, d = x2d.shape
    f = wg.shape[1]
    const = lambda i: (0, 0)
    return pl.pallas_call(
        _ffn_kernel,
        grid=(t // tm,),
        in_specs=[pl.BlockSpec((tm, d), lambda i: (i, 0)),
                  pl.BlockSpec((d, f), const),
                  pl.BlockSpec((d, f), const),
                  pl.BlockSpec((f, d), const),
                  pl.BlockSpec((1, d), const),
                  pl.BlockSpec((1, d), const)],
        out_specs=pl.BlockSpec((tm, d), lambda i: (i, 0)),
        out_shape=jax.ShapeDtypeStruct((t, d), F32),
        compiler_params=_params(("parallel",), vmem=56 * 1024 * 1024),
        name="ffn_dense",
    )(x2d, wg, wu, wd, ln_w, ln_b)


def _kv_operands(kv, low):
    swapped = pltpu.roll(kv, HEAD_DIM, 1)
    return (jnp.where(low, kv, 0.0), jnp.where(low, 0.0, swapped),
            jnp.where(low, swapped, 0.0), jnp.where(low, 0.0, kv))


def _inproj1_kernel(x_ref, wq_ref, wkv_ref, wc_ref, cw_ref, cos_ref, sin_ref,
                    q0_ref, q1_ref, q2_ref, kv0_ref, kv1_ref, kv2_ref, d_ref,
                    zc_ref, sq_ref, skv_ref, *, tm):
    i = pl.program_id(1)
    xb = x_ref[...].astype(BF16)
    cos = cos_ref[...]
    sin = sin_ref[...]
    lane = lax.broadcasted_iota(I32, (tm, LANES), 1)
    low = lane < HEAD_DIM
    cos_k = jnp.where(low, cos, 1.0)
    sin_k = jnp.where(low, sin, 0.0)
    q_refs = (q0_ref, q1_ref, q2_ref)
    kv_refs = (kv0_ref, kv1_ref, kv2_ref)
    for gi, (_, dil) in enumerate(DIL_PAIRS):
        q = jnp.dot(xb, wq_ref[gi], preferred_element_type=F32)
        q = _rope(q, cos, sin) * QK_SCALE
        kv = jnp.dot(xb, wkv_ref[gi], preferred_element_type=F32)
        kv = _rope(kv, cos_k, sin_k)
        kvx = [_kv_operands(kv[:, h * LANES:(h + 1) * LANES], low) for h in range(DIL_KV_HEADS)]
        if dil == 1:
            q_refs[gi][0] = q.astype(BF16)
            for h in range(DIL_KV_HEADS):
                kv_refs[gi][h, 0] = jnp.concatenate(kvx[h], axis=1).astype(BF16)
        else:
            n = tm // dil
            for c in range(DIL_Q_DIM // LANES):
                sq_ref[c] = q[:, c * LANES:(c + 1) * LANES]
            for h in range(DIL_KV_HEADS):
                for part in range(KV_PARTS):
                    skv_ref[h * KV_PARTS + part] = kvx[h][part]
            for r in range(dil):
                rows = pl.ds(r, n, stride=dil)
                q_refs[gi][r] = jnp.concatenate(
                    [sq_ref[c, rows, :] for c in range(DIL_Q_DIM // LANES)], axis=1).astype(BF16)
                for h in range(DIL_KV_HEADS):
                    kv_refs[gi][h, r] = jnp.concatenate(
                        [skv_ref[h * KV_PARTS + part, rows, :] for part in range(KV_PARTS)], axis=1).astype(BF16)

    hc = jnp.dot(xb, wc_ref[...], preferred_element_type=F32)
    z = hc[:, 2 * CONV_WIDTH:] * hc[:, :CONV_WIDTH]
    zprev = jnp.where(i == 0, 0.0, zc_ref[...])
    zfull = jnp.concatenate([zprev, z], axis=0)
    z1 = pltpu.roll(zfull, 1, 0)[CONV_HALO:]
    z2 = pltpu.roll(zfull, 2, 0)[CONV_HALO:]
    cw = cw_ref[...]
    y = cw[0:1] * z2 + cw[1:2] * z1 + cw[2:3] * z
    d_ref[...] = (hc[:, CONV_WIDTH:2 * CONV_WIDTH] * y).astype(BF16)
    zc_ref[...] = z[tm - CONV_HALO:]


def _inproj1(x2d, wq, wkv, wc, conv_w, cos_t, sin_t, b, s, tm=512):
    d = x2d.shape[1]
    cur = lambda bi, i: (bi, i, 0)
    row = lambda bi, i: (bi * (s // tm) + i, 0)
    in_specs = [pl.BlockSpec((None, tm, d), cur),
                pl.BlockSpec(wq.shape, lambda bi, i: (0, 0, 0)),
                pl.BlockSpec(wkv.shape, lambda bi, i: (0, 0, 0)),
                pl.BlockSpec(wc.shape, lambda bi, i: (0, 0)),
                pl.BlockSpec(conv_w.shape, lambda bi, i: (0, 0)),
                pl.BlockSpec((tm, LANES), row),
                pl.BlockSpec((tm, LANES), row)]
    out_specs, out_shape = [], []
    for _, dil in DIL_PAIRS:
        out_specs.append(pl.BlockSpec((None, dil, tm // dil, DIL_Q_DIM), lambda bi, i: (bi, 0, i, 0)))
        out_shape.append(jax.ShapeDtypeStruct((b, dil, s // dil, DIL_Q_DIM), BF16))
    for _, dil in DIL_PAIRS:
        out_specs.append(pl.BlockSpec((None, DIL_KV_HEADS, dil, tm // dil, KV_PARTS * LANES),
                                      lambda bi, i: (bi, 0, 0, i, 0)))
        out_shape.append(jax.ShapeDtypeStruct((b, DIL_KV_HEADS, dil, s // dil, KV_PARTS * LANES), BF16))
    out_specs.append(pl.BlockSpec((None, tm, CONV_WIDTH), cur))
    out_shape.append(jax.ShapeDtypeStruct((b, s, CONV_WIDTH), BF16))
    return pl.pallas_call(
        functools.partial(_inproj1_kernel, tm=tm),
        grid=(b, s // tm),
        in_specs=in_specs,
        out_specs=out_specs,
        out_shape=out_shape,
        scratch_shapes=[pltpu.VMEM((CONV_HALO, CONV_WIDTH), F32),
                        pltpu.VMEM((DIL_Q_DIM // LANES, tm, LANES), F32),
                        pltpu.VMEM((DIL_KV_HEADS * KV_PARTS, tm, LANES), F32)],
        compiler_params=_params(("parallel", "arbitrary")),
        name="inproj1",
    )(x2d.reshape(b, s, d), wq, wkv, wc, conv_w, cos_t, sin_t)


def _dil_kernel(q0_ref, q1_ref, q2_ref, kv0_ref, kv1_ref, kv2_ref, out_ref, o_run, l_run, bias_ref, *, s):
    q_refs = (q0_ref, q1_ref, q2_ref)
    kv_refs = (kv0_ref, kv1_ref, kv2_ref)
    n_blocks = s // ATT_BLOCK
    width = DIL_Q_PER_KV * HEAD_DIM
    max_dist = DIL_PAIRS[0][0] // DIL_PAIRS[0][1]
    assert all(w // d == max_dist for w, d in DIL_PAIRS)
    stacked = (width // LANES) * ATT_BLOCK
    bias_ref[0] = _band_bias(stacked, max_dist, 0)
    bias_ref[1] = _band_bias(stacked, max_dist, ATT_BLOCK)
    for gi, (window, dil) in enumerate(DIL_PAIRS):
        q_ref, kv_ref = q_refs[gi], kv_refs[gi]
        blocks_per_sub = (s // dil) // ATT_BLOCK
        shift = blocks_per_sub.bit_length() - 1

        def body(n, carry, q_ref=q_ref, kv_ref=kv_ref, gi=gi, window=window, dil=dil,
                 blocks_per_sub=blocks_per_sub, shift=shift):
            res = lax.shift_right_logical(n, shift)
            bi = n & (blocks_per_sub - 1)
            rows = pl.ds(pl.multiple_of(n * ATT_BLOCK, ATT_BLOCK), ATT_BLOCK)
            prow = pl.ds(pl.multiple_of(jnp.maximum(n - 1, 0) * ATT_BLOCK, ATT_BLOCK), ATT_BLOCK)
            kvx = jnp.concatenate([kv_ref[prow, :], kv_ref[rows, :]], axis=0)
            kz, zk, vz, zv = _kv_parts(kvx)
            q4 = q_ref[rows, :]
            n_chunks = width // LANES
            qs = jnp.concatenate([q4[:, c * LANES:(c + 1) * LANES] for c in range(n_chunks)], axis=0)
            bias = bias_ref[jnp.where(bi == 0, 1, 0)]
            oa, ma, da = _attn_stack(qs, kz, vz, bias)
            ob, mb, db = _attn_stack(qs, zk, zv, bias)
            low = lax.broadcasted_iota(I32, (n_chunks * ATT_BLOCK, LANES), 1) < HEAD_DIM
            o_st = (oa + ob) * jnp.where(low, 1.0 / da, 1.0 / db)
            l_st = jnp.where(low, ma + jnp.log(da), mb + jnp.log(db))
            start = res + dil * ATT_BLOCK * bi
            if dil == 1:
                tok = pl.ds(pl.multiple_of(start, ATT_BLOCK), ATT_BLOCK)
            else:
                tok = pl.ds(start, ATT_BLOCK, stride=dil)
            for c in range(n_chunks):
                o_c = o_st[c * ATT_BLOCK:(c + 1) * ATT_BLOCK]
                l_c = l_st[c * ATT_BLOCK:(c + 1) * ATT_BLOCK]
                if gi == 0:
                    o_run[c, tok, :] = o_c
                    l_run[c, tok, :] = l_c
                else:
                    o_old = o_run[c, tok, :]
                    l_old = l_run[c, tok, :]
                    m = jnp.maximum(l_old, l_c)
                    a = jnp.exp(l_old - m)
                    bb = jnp.exp(l_c - m)
                    tot = a + bb
                    o_run[c, tok, :] = (o_old * a + o_c * bb) * (1.0 / tot)
                    if gi < len(DIL_PAIRS) - 1:
                        l_run[c, tok, :] = m + jnp.log(tot)
            return carry

        lax.fori_loop(0, n_blocks, body, 0, unroll=4)
    for c in range(width // LANES):
        out_ref[:, c * LANES:(c + 1) * LANES] = o_run[c].astype(BF16)


def _dilated(qs, kvs, b, s):
    width = DIL_Q_PER_KV * HEAD_DIM
    in_specs = [pl.BlockSpec((None, s, width), lambda bi, h: (bi, 0, h)) for _ in DIL_PAIRS]
    in_specs += [pl.BlockSpec((None, None, s, KV_PARTS * LANES), lambda bi, h: (bi, h, 0, 0)) for _ in DIL_PAIRS]
    qs = [q.reshape(b, s, DIL_Q_DIM) for q in qs]
    kvs = [kv.reshape(b, DIL_KV_HEADS, s, KV_PARTS * LANES) for kv in kvs]
    out = pl.pallas_call(
        functools.partial(_dil_kernel, s=s),
        grid=(b, DIL_KV_HEADS),
        in_specs=in_specs,
        out_specs=pl.BlockSpec((None, s, width), lambda bi, h: (bi, 0, h)),
        out_shape=jax.ShapeDtypeStruct((b, s, DIL_Q_DIM), BF16),
        scratch_shapes=[pltpu.VMEM((width // LANES, s, LANES), F32),
                        pltpu.VMEM((width // LANES, s, LANES), F32),
                        pltpu.VMEM((2, (width // LANES) * ATT_BLOCK, 2 * ATT_BLOCK), F32)],
        compiler_params=_params(("parallel", "parallel"), vmem=56 * 1024 * 1024),
        name="dilated_attention",
    )(*qs, *kvs)
    return out.reshape(b * s, DIL_Q_DIM)


def _to_slabs(ref, val, base=0):
    rows, d = val.shape
    per = d // LANES
    for c in range(per):
        ref[pl.ds(base + c, rows, stride=per), :] = val[:, c * LANES:(c + 1) * LANES]


def _from_slabs(ref, rows, base=0, per=ROW_SLAB):
    return jnp.concatenate([ref[pl.ds(base + c, rows, stride=per), :] for c in range(per)], axis=1)


def _outproj1_kernel(c_ref, d_ref, x_ref, wa_ref, wb_ref, lw_ref, lb_ref, wr_ref,
                     out_ref, slab_ref, e_ref, g_ref, r_ref, cnt_ref, run_ref, *, tm):
    mix = jnp.dot(c_ref[...], wa_ref[...], preferred_element_type=F32)
    mix = mix + jnp.dot(d_ref[...], wb_ref[...], preferred_element_type=F32)
    y = _layer_norm(ALPHA * x_ref[...] + mix, lw_ref[...], lb_ref[...])
    out_ref[...] = y
    _to_slabs(slab_ref, y)
    _route_tile(y, wr_ref[...], e_ref, g_ref, r_ref, cnt_ref, run_ref, tm)


def _outproj1(c, dconv, x2d, wa, wb, ln_w, ln_b, wr_t, tm=512):
    t, d = x2d.shape
    row = lambda i: (i, 0)
    col = lambda i: (0, i)
    const = lambda i: (0, 0)
    return pl.pallas_call(
        functools.partial(_outproj1_kernel, tm=tm),
        grid=(t // tm,),
        in_specs=[pl.BlockSpec((tm, DIL_Q_DIM), row),
                  pl.BlockSpec((tm, CONV_WIDTH), row),
                  pl.BlockSpec((tm, d), row),
                  pl.BlockSpec((DIL_Q_DIM, d), const),
                  pl.BlockSpec((CONV_WIDTH, d), const),
                  pl.BlockSpec((1, d), const),
                  pl.BlockSpec((1, d), const),
                  pl.BlockSpec((N_EXPERTS, d), const)],
        out_specs=[pl.BlockSpec((tm, d), row), pl.BlockSpec((tm * (d // LANES), LANES), row),
                   pl.BlockSpec((2, tm), col), pl.BlockSpec((2, tm), col), pl.BlockSpec((2, tm), col),
                   pl.BlockSpec((N_EXPERTS, LANES), const)],
        out_shape=[jax.ShapeDtypeStruct((t, d), F32), jax.ShapeDtypeStruct((t * (d // LANES), LANES), F32),
                   jax.ShapeDtypeStruct((2, t), I32), jax.ShapeDtypeStruct((2, t), F32),
                   jax.ShapeDtypeStruct((2, t), I32), jax.ShapeDtypeStruct((N_EXPERTS, LANES), F32)],
        scratch_shapes=[pltpu.VMEM((N_EXPERTS, LANES), F32)],
        compiler_params=_params(("arbitrary",)),
        name="outproj1_route",
    )(c, dconv, x2d, wa, wb, ln_w, ln_b, wr_t)


def _route_tile(x, wr, e_ref, g_ref, r_ref, cnt_ref, run_ref, tm):
    i = pl.program_id(0)

    @pl.when(i == 0)
    def _():
        run_ref[...] = jnp.zeros_like(run_ref)

    logits = lax.dot_general(wr, x, (((1,), (1,)), ((), ())),
                             precision=lax.Precision.HIGHEST, preferred_element_type=F32)
    eid = lax.broadcasted_iota(I32, (N_EXPERTS, tm), 0)
    m1 = jnp.max(logits, axis=0, keepdims=True)
    i1 = jnp.min(jnp.where(logits == m1, eid, N_EXPERTS), axis=0, keepdims=True)
    rest = jnp.where(eid == i1, -jnp.inf, logits)
    m2 = jnp.max(rest, axis=0, keepdims=True)
    i2 = jnp.min(jnp.where(rest == m2, eid, N_EXPERTS), axis=0, keepdims=True)
    t2 = jnp.exp(m2 - m1)
    g1 = 1.0 / (1.0 + t2)
    g2 = t2 / (1.0 + t2)
    oh1 = eid == i1
    oh2 = eid == i2
    oh = jnp.where(oh1 | oh2, 1.0, 0.0)
    ri = lax.broadcasted_iota(I32, (tm, tm), 0)
    ci = lax.broadcasted_iota(I32, (tm, tm), 1)
    tri = jnp.where(ri <= ci, 1.0, 0.0).astype(BF16)
    incl = jnp.dot(oh.astype(BF16), tri, preferred_element_type=F32)
    rank = run_ref[:, 0:1] + incl - oh
    r1 = jnp.sum(jnp.where(oh1, rank, 0.0), axis=0, keepdims=True)
    r2 = jnp.sum(jnp.where(oh2, rank, 0.0), axis=0, keepdims=True)
    e_ref[0:1, :] = i1
    e_ref[1:2, :] = i2
    g_ref[0:1, :] = g1
    g_ref[1:2, :] = g2
    r_ref[0:1, :] = r1.astype(I32)
    r_ref[1:2, :] = r2.astype(I32)
    run_ref[...] = run_ref[...] + incl[:, tm - 1:tm]
    cnt_ref[...] = run_ref[...]


def _slab_copy(src_ref, dst_ref, sem, src_row, dst_row, n=1):
    src = src_ref.at[pl.ds(pl.multiple_of(src_row * ROW_SLAB, ROW_SLAB), n * ROW_SLAB)]
    dst = dst_ref.at[pl.ds(pl.multiple_of(dst_row * ROW_SLAB, ROW_SLAB), n * ROW_SLAB)]
    return pltpu.make_async_copy(src, dst, sem)


def _moe_kernel(be_ref, nv_ref, tok_ref, tokn_ref, dstp_ref, dstc_ref, x_hbm, wg_ref, wu_ref, wd_ref, y_hbm,
                xs_ref, xb_ref, acc_ref, stage_ref, sem_in, sem_out, *, tm, nj, dump_row0):
    i = pl.program_id(0)
    j = pl.program_id(1)
    n_valid = nv_ref[0]
    slot = i & 1
    per_step = tm // nj

    def row_in(tok, dst_slot, r):
        return _slab_copy(x_hbm, xs_ref, sem_in.at[dst_slot], tok, dst_slot * tm + r)

    def row_out(src_slot, r, dst):
        return _slab_copy(stage_ref, y_hbm, sem_out.at[src_slot], src_slot * tm + r, dst)

    @pl.when(i < n_valid)
    def _():
        @pl.when(j == 0)
        def _():
            @pl.when(i == 0)
            def _():
                def start(r, c):
                    row_in(tok_ref[0, 0, r], 0, r).start()
                    return c
                lax.fori_loop(0, tm, start, 0)
                stage_ref[...] = jnp.zeros_like(stage_ref)
                fill = _slab_copy(stage_ref, y_hbm, sem_out.at[0], 0, dump_row0, n=2 * tm)
                fill.start()
                fill.wait()

            _slab_copy(x_hbm, xs_ref, sem_in.at[slot], 0, slot * tm, n=tm).wait()
            xb_ref[...] = _from_slabs(xs_ref, tm, base=slot * (tm * ROW_SLAB)).astype(BF16)
            acc_ref[...] = jnp.zeros_like(acc_ref)

        for r in range(per_step):
            rr = j * per_step + r
            row_in(tokn_ref[0, 0, rr], 1 - slot, rr).start(priority=0)
            row_out(1 - slot, rr, dstp_ref[0, 0, rr]).start(priority=1)

        acc_ref[...] += _swiglu_tile(xb_ref[...], wg_ref[...], wu_ref[...], wd_ref[...])

        @pl.when(j == nj - 1)
        def _():
            _slab_copy(stage_ref, y_hbm, sem_out.at[1 - slot], (1 - slot) * tm, 0, n=tm).wait()
            _to_slabs(stage_ref, acc_ref[...], base=slot * (tm * ROW_SLAB))

            @pl.when(i + 1 >= n_valid)
            def _():
                _slab_copy(x_hbm, xs_ref, sem_in.at[1 - slot], 0, (1 - slot) * tm, n=tm).wait()

                def start(r, c):
                    row_out(slot, r, dstc_ref[0, 0, r]).start()
                    return c
                lax.fori_loop(0, tm, start, 0)
                _slab_copy(stage_ref, y_hbm, sem_out.at[slot], slot * tm, 0, n=tm).wait()


def _moe_experts(x_slabs, row_tok, row_dst, dump_row0, block_e, n_valid, wg, wu, wd, tm, tf=1792):
    d = wg.shape[1]
    n_rows = row_tok.shape[0]
    n_blocks = n_rows // tm
    f = wg.shape[2]
    nj = f // tf
    y_rows = dump_row0 + 2 * tm

    def live(i, j, be, nv):
        ok = i < nv[0]
        return jnp.where(ok, i, nv[0] - 1), jnp.where(ok, j, nj - 1)

    def w_map(i, j, be, nv):
        ii, jj = live(i, j, be, nv)
        return be[ii], 0, jj

    def wd_map(i, j, be, nv):
        ii, jj = live(i, j, be, nv)
        return be[ii], jj, 0

    def smem(index_map):
        return pl.BlockSpec((1, 1, tm), index_map, memory_space=pltpu.SMEM)

    grid_spec = pltpu.PrefetchScalarGridSpec(
        num_scalar_prefetch=2,
        grid=(n_blocks, nj),
        in_specs=[smem(lambda i, j, be, nv: (i, 0, 0)),
                  smem(lambda i, j, be, nv: (jnp.minimum(i + 1, n_blocks - 1), 0, 0)),
                  smem(lambda i, j, be, nv: (jnp.where(i == 0, n_blocks, i - 1), 0, 0)),
                  smem(lambda i, j, be, nv: (i, 0, 0)),
                  pl.BlockSpec(memory_space=pl.ANY),
                  pl.BlockSpec((None, d, tf), w_map),
                  pl.BlockSpec((None, d, tf), w_map),
                  pl.BlockSpec((None, tf, d), wd_map)],
        out_specs=pl.BlockSpec(memory_space=pl.ANY),
        scratch_shapes=[pltpu.VMEM((2 * tm * ROW_SLAB, LANES), F32),
                        pltpu.VMEM((tm, d), BF16),
                        pltpu.VMEM((tm, d), F32),
                        pltpu.VMEM((2 * tm * ROW_SLAB, LANES), F32),
                        pltpu.SemaphoreType.DMA((2,)),
                        pltpu.SemaphoreType.DMA((2,))],
    )
    tok3 = row_tok.reshape(n_blocks, 1, tm)
    dst3 = row_dst.reshape(n_blocks + 1, 1, tm)
    return pl.pallas_call(
        functools.partial(_moe_kernel, tm=tm, nj=nj, dump_row0=dump_row0),
        grid_spec=grid_spec,
        out_shape=jax.ShapeDtypeStruct((y_rows * ROW_SLAB, LANES), F32),
        compiler_params=_params(("arbitrary", "arbitrary")),
        name="moe_experts",
    )(block_e, n_valid, tok3, tok3, dst3, dst3, x_slabs, wg, wu, wd)


def _combine_kernel(ya_ref, yb_ref, x_ref, g1_ref, g2_ref, lw_ref, lb_ref, out_ref, *, tm):
    ffn = _from_slabs(ya_ref, tm) * g1_ref[...] + _from_slabs(yb_ref, tm) * g2_ref[...]
    out_ref[...] = _layer_norm(ALPHA * x_ref[...] + ffn, lw_ref[...], lb_ref[...])


def _combine(y, x2d, g1, g2, ln_w, ln_b, tm=512):
    t, d = x2d.shape
    row = lambda i: (i, 0)
    return pl.pallas_call(
        functools.partial(_combine_kernel, tm=tm),
        grid=(t // tm,),
        in_specs=[pl.BlockSpec((tm * ROW_SLAB, LANES), row),
                  pl.BlockSpec((tm * ROW_SLAB, LANES), lambda i: (t // tm + i, 0)),
                  pl.BlockSpec((tm, d), row),
                  pl.BlockSpec((tm, 1), row),
                  pl.BlockSpec((tm, 1), row),
                  pl.BlockSpec((1, d), lambda i: (0, 0)),
                  pl.BlockSpec((1, d), lambda i: (0, 0))],
        out_specs=pl.BlockSpec((tm, d), row),
        out_shape=jax.ShapeDtypeStruct((t, d), F32),
        compiler_params=_params(("parallel",)),
        name="moe_combine",
    )(y, y, x2d, g1.reshape(t, 1), g2.reshape(t, 1), ln_w, ln_b)


def _moe_layer(x2d, x_slabs, e_sel, gates, ranks, counts, wg, wu, wd, ln_w, ln_b, tmoe=512):
    t, d = x2d.shape
    counts = counts[:, 0].astype(I32)
    padded = (counts + tmoe - 1) // tmoe * tmoe
    pend = jnp.cumsum(padded)
    pstart = pend - padded
    n_blocks = -(-(2 * t + N_EXPERTS * (tmoe - 1)) // tmoe)
    n_rows = n_blocks * tmoe
    first_row = sum(jnp.where(e_sel == e, pstart[e], 0) for e in range(N_EXPERTS))
    dest = first_row + ranks
    dump_row0 = 2 * t
    r = jnp.arange(n_rows, dtype=I32)
    dump = dump_row0 + ((r // tmoe) & 1) * tmoe + r % tmoe
    tok = jnp.arange(t, dtype=I32)
    row_dst = dump.at[dest.reshape(-1)].set(jnp.concatenate([tok, t + tok]))
    row_tok = jnp.where(row_dst < dump_row0, row_dst % t, 0)
    row_dst = jnp.concatenate([row_dst, dump_row0 + tmoe + jnp.arange(tmoe, dtype=I32)])
    block_row0 = jnp.arange(n_blocks, dtype=I32) * tmoe
    block_e = jnp.minimum(jnp.sum((block_row0[:, None] >= pend[None, :]).astype(I32), axis=1), N_EXPERTS - 1)
    n_valid = (pend[-1:] // tmoe).astype(I32)
    y = _moe_experts(x_slabs, row_tok, row_dst, dump_row0, block_e, n_valid, wg, wu, wd, tmoe)
    return _combine(y, x2d, gates[0], gates[1], ln_w, ln_b)


def kernel(x, positions, ln_w, ln_b, even_w_in, pool_w, pool_scale, swa_sinks, even_w_out, ffn_w_gate, ffn_w_up, ffn_w_down, odd_w_in, conv_w, odd_w_out, router_w, moe_w_gate, moe_w_up, moe_w_down):
    b, s, d = x.shape
    t = b * s
    x2d = x.reshape(t, d)
    cos_t, sin_t = _rope_tables(positions)
    ln = lambda layer, k: (ln_w[layer, k].reshape(1, d), ln_b[layer, k].reshape(1, d))

    u, q, k, v = _inproj0(x2d, even_w_in[0].astype(BF16), cos_t, sin_t)
    attn = _swa(q, k, v, swa_sinks[0], b, s)
    pool_bd = jax.scipy.linalg.block_diag(*[pool_w[0, gi] for gi in range(len(POOL_WINDOWS))]).astype(BF16)
    w_out0 = even_w_out[0].astype(BF16)
    x2d = _outproj0(u, attn, x2d, pool_bd, pool_scale[0].reshape(1, POOL_WIDTH),
                    w_out0[:POOL_WIDTH], w_out0[POOL_WIDTH:], *ln(0, 0), b, s)
    x2d = _ffn_dense(x2d, ffn_w_gate[0].astype(BF16), ffn_w_up[0].astype(BF16), ffn_w_down[0].astype(BF16),
                     *ln(0, 1))

    w_in1 = odd_w_in[0].astype(BF16)
    c_in = len(DIL_PAIRS) * DIL_GROUP_IN
    wq, wkv = [], []
    for gi in range(len(DIL_PAIRS)):
        g0 = gi * DIL_GROUP_IN
        wq.append(w_in1[:, g0:g0 + DIL_Q_DIM])
        kcol = g0 + DIL_Q_DIM
        vcol = kcol + DIL_KV_DIM
        parts = []
        for h in range(DIL_KV_HEADS):
            parts += [w_in1[:, kcol + h * HEAD_DIM:kcol + (h + 1) * HEAD_DIM],
                      w_in1[:, vcol + h * HEAD_DIM:vcol + (h + 1) * HEAD_DIM]]
        wkv.append(jnp.concatenate(parts, axis=1))
    outs = _inproj1(x2d, jnp.stack(wq), jnp.stack(wkv), w_in1[:, c_in:], conv_w[0], cos_t, sin_t, b, s)
    c_out = _dilated(outs[0:3], outs[3:6], b, s)
    w_out1 = odd_w_out[0].astype(BF16)
    x2d, x_slabs, e_sel, gates, ranks, counts = _outproj1(
        c_out, outs[6].reshape(t, CONV_WIDTH), x2d, w_out1[:DIL_Q_DIM], w_out1[DIL_Q_DIM:], *ln(1, 0),
        router_w[0].T)
    x2d = _moe_layer(x2d, x_slabs, e_sel, gates, ranks, counts, moe_w_gate[0].astype(BF16),
                     moe_w_up[0].astype(BF16), moe_w_down[0].astype(BF16), *ln(1, 1))
    return x2d.reshape(b, s, d)
```

```python
import functools
import math

import jax
import jax.numpy as jnp
from jax import lax
from jax.experimental import pallas as pl
from jax.experimental.pallas import tpu as pltpu

F32 = jnp.float32
BF16 = jnp.bfloat16
I32 = jnp.int32

HEAD_DIM = 64
ROPE_THETA = 10000.0
ATT_BLOCK = 128
LN_EPS = 1e-5
POOL_WINDOWS = (2, 4, 8, 16)
POOL_GROUP = 64
POOL_WIDTH = 256
POOL_HALO = 16
SWA_WINDOW = 128
SWA_Q_HEADS = 12
SWA_KV_HEADS = 4
SWA_Q_DIM = SWA_Q_HEADS * HEAD_DIM
SWA_KV_DIM = SWA_KV_HEADS * HEAD_DIM
DIL_PAIRS = ((128, 1), (512, 4), (2048, 16))
DIL_Q_HEADS = 8
DIL_KV_HEADS = 2
DIL_Q_DIM = DIL_Q_HEADS * HEAD_DIM
DIL_KV_DIM = DIL_KV_HEADS * HEAD_DIM
DIL_GROUP_IN = DIL_Q_DIM + 2 * DIL_KV_DIM
DIL_Q_PER_KV = DIL_Q_HEADS // DIL_KV_HEADS
CONV_WIDTH = 512
CONV_K = 3
CONV_HALO = 8
N_EXPERTS = 8
DEPTH = 2
ALPHA = (2 * DEPTH) ** 0.25
QK_SCALE = 1.0 / math.sqrt(HEAD_DIM)
NEG_BIG = -1e30
KV_PARTS = 4

LANES = 128
ROW_SLAB = 8
VMEM_LIMIT = 48 * 1024 * 1024
VMEM_LIMIT_BIG = 56 * 1024 * 1024


def _params(sem, vmem=VMEM_LIMIT):
    return pltpu.CompilerParams(dimension_semantics=sem, vmem_limit_bytes=vmem)


def _layer_norm(y, w, b):
    mu = jnp.mean(y, axis=-1, keepdims=True)
    yc = y - mu
    var = jnp.mean(yc * yc, axis=-1, keepdims=True)
    return yc * lax.rsqrt(var + LN_EPS) * w + b


def _rope_chunk(xc, cos, sin_signed, first_half):
    rot = jnp.where(first_half, pltpu.roll(xc, 96, 1), pltpu.roll(xc, 32, 1))
    return xc * cos + rot * sin_signed


def _rope(x, cos, sin_signed):
    tm, c = x.shape
    lane = lax.broadcasted_iota(I32, (tm, LANES), 1)
    first_half = (lane & 32) == 0
    chunks = [_rope_chunk(x[:, i * LANES:(i + 1) * LANES], cos, sin_signed, first_half)
              for i in range(c // LANES)]
    return chunks[0] if len(chunks) == 1 else jnp.concatenate(chunks, axis=1)


def _trig_kernel(pos_ref, inv_ref, cos_ref, sin_ref):
    pos = pos_ref[...].astype(F32)
    ang = inv_ref[...] * pos
    c = jnp.cos(ang)
    s = jnp.sin(ang)
    c4 = jnp.concatenate([c, c, c, c], axis=0)
    s4 = jnp.concatenate([-s, s, -s, s], axis=0)
    cos_ref[...] = c4.T
    sin_ref[...] = s4.T


def _rope_tables(positions, tm=512):
    t = positions.size
    half = HEAD_DIM // 2
    inv = ROPE_THETA ** (-jnp.arange(half, dtype=F32) / half)
    return pl.pallas_call(
        _trig_kernel,
        grid=(t // tm,),
        in_specs=[pl.BlockSpec((1, tm), lambda i: (0, i)),
                  pl.BlockSpec((half, 1), lambda i: (0, 0))],
        out_specs=[pl.BlockSpec((tm, LANES), lambda i: (i, 0)),
                   pl.BlockSpec((tm, LANES), lambda i: (i, 0))],
        out_shape=[jax.ShapeDtypeStruct((t, LANES), F32)] * 2,
        compiler_params=_params(("parallel",)),
        name="rope_tables",
    )(positions.reshape(1, t), inv.reshape(half, 1))


def _inproj0_kernel(x_ref, w_ref, cos_ref, sin_ref, u_ref, q_ref, k_ref, v_ref):
    xb = x_ref[...].astype(BF16)
    cos = cos_ref[...]
    sin = sin_ref[...]
    q0 = POOL_WIDTH
    k0 = q0 + SWA_Q_DIM
    v0 = k0 + SWA_KV_DIM
    u_ref[...] = jnp.dot(xb, w_ref[:, :q0], preferred_element_type=F32)
    q = jnp.dot(xb, w_ref[:, q0:k0], preferred_element_type=F32)
    q_ref[...] = (_rope(q, cos, sin) * QK_SCALE).astype(BF16)
    k = jnp.dot(xb, w_ref[:, k0:v0], preferred_element_type=F32)
    k_ref[...] = _rope(k, cos, sin).astype(BF16)
    v_ref[...] = jnp.dot(xb, w_ref[:, v0:], preferred_element_type=F32).astype(BF16)


def _inproj0(x2d, w_bf, cos_t, sin_t, tm=512):
    t, d = x2d.shape
    n_in = w_bf.shape[1]
    row = lambda i: (i, 0)
    return pl.pallas_call(
        _inproj0_kernel,
        grid=(t // tm,),
        in_specs=[pl.BlockSpec((tm, d), row),
                  pl.BlockSpec((d, n_in), lambda i: (0, 0)),
                  pl.BlockSpec((tm, LANES), row),
                  pl.BlockSpec((tm, LANES), row)],
        out_specs=[pl.BlockSpec((tm, POOL_WIDTH), row),
                   pl.BlockSpec((tm, SWA_Q_DIM), row),
                   pl.BlockSpec((tm, SWA_KV_DIM), row),
                   pl.BlockSpec((tm, SWA_KV_DIM), row)],
        out_shape=[jax.ShapeDtypeStruct((t, POOL_WIDTH), F32),
                   jax.ShapeDtypeStruct((t, SWA_Q_DIM), BF16),
                   jax.ShapeDtypeStruct((t, SWA_KV_DIM), BF16),
                   jax.ShapeDtypeStruct((t, SWA_KV_DIM), BF16)],
        compiler_params=_params(("parallel",)),
        name="inproj0",
    )(x2d, w_bf, cos_t, sin_t)


def _band_bias(rows, max_dist, key_lo):
    qi = lax.broadcasted_iota(I32, (rows, 2 * ATT_BLOCK), 0) & (ATT_BLOCK - 1)
    sj = lax.broadcasted_iota(I32, (rows, 2 * ATT_BLOCK), 1)
    dist = ATT_BLOCK + qi - sj
    return jnp.where((dist >= 0) & (dist <= max_dist) & (sj >= key_lo), 0.0, NEG_BIG)


def _band_mask(max_dist, key_lo):
    qi = lax.broadcasted_iota(I32, (ATT_BLOCK, 2 * ATT_BLOCK), 0)
    sj = lax.broadcasted_iota(I32, (ATT_BLOCK, 2 * ATT_BLOCK), 1)
    dist = ATT_BLOCK + qi - sj
    return (dist >= 0) & (dist <= max_dist) & (sj >= key_lo)


def _attn_block(q, kwin, vwin, valid, sink):
    s = lax.dot_general(q, kwin, (((1,), (1,)), ((), ())), preferred_element_type=F32)
    s = jnp.where(valid, s, NEG_BIG)
    m = jnp.maximum(jnp.max(s, axis=1, keepdims=True), sink)
    p = jnp.exp(s - m)
    den = jnp.sum(p, axis=1, keepdims=True) + jnp.exp(sink - m)
    o = jnp.dot(p.astype(BF16), vwin, preferred_element_type=F32)
    return o * (1.0 / den)


def _attn_stack(qs, kpart, vpart, bias):
    s = lax.dot_general(qs, kpart, (((1,), (1,)), ((), ())), preferred_element_type=F32) + bias
    m = jnp.max(s, axis=1, keepdims=True)
    p = jnp.exp(s - m)
    den = jnp.sum(p, axis=1, keepdims=True)
    return jnp.dot(p.astype(BF16), vpart, preferred_element_type=F32), m, den


def _kv_parts(kvx, h=0):
    base = h * KV_PARTS * LANES
    return [kvx[:, base + part * LANES:base + (part + 1) * LANES] for part in range(KV_PARTS)]


def _swa_kernel(sink_ref, q_ref, kp_ref, kc_ref, vp_ref, vc_ref, o_ref, *, tq):
    i = pl.program_id(1)
    kfull = jnp.concatenate([kp_ref[...], kc_ref[...]], axis=0)
    vfull = jnp.concatenate([vp_ref[...], vc_ref[...]], axis=0)
    g = SWA_Q_HEADS // SWA_KV_HEADS
    for j in range(tq // ATT_BLOCK):
        key_lo = jnp.where(i == 0, ATT_BLOCK, 0) if j == 0 else 0
        valid = _band_mask(SWA_WINDOW - 1, key_lo)
        r0 = j * ATT_BLOCK
        outs = []
        for h in range(SWA_Q_HEADS):
            kv = h // g
            q = q_ref[r0:r0 + ATT_BLOCK, h * HEAD_DIM:(h + 1) * HEAD_DIM]
            kwin = kfull[r0:r0 + 2 * ATT_BLOCK, kv * HEAD_DIM:(kv + 1) * HEAD_DIM]
            vwin = vfull[r0:r0 + 2 * ATT_BLOCK, kv * HEAD_DIM:(kv + 1) * HEAD_DIM]
            outs.append(_attn_block(q, kwin, vwin, valid, sink_ref[h]))
        o_ref[r0:r0 + ATT_BLOCK, :] = jnp.concatenate(outs, axis=1).astype(BF16)


def _swa(q, k, v, sinks, b, s, tq=256):
    per = tq // ATT_BLOCK
    cur = lambda bi, i: (bi, i, 0)
    prev = lambda bi, i: (bi, jnp.maximum(i * per - 1, 0), 0)
    q3 = q.reshape(b, s, SWA_Q_DIM)
    k3 = k.reshape(b, s, SWA_KV_DIM)
    v3 = v.reshape(b, s, SWA_KV_DIM)
    out = pl.pallas_call(
        functools.partial(_swa_kernel, tq=tq),
        grid=(b, s // tq),
        in_specs=[pl.BlockSpec(memory_space=pltpu.SMEM),
                  pl.BlockSpec((None, tq, SWA_Q_DIM), cur),
                  pl.BlockSpec((None, ATT_BLOCK, SWA_KV_DIM), prev),
                  pl.BlockSpec((None, tq, SWA_KV_DIM), cur),
                  pl.BlockSpec((None, ATT_BLOCK, SWA_KV_DIM), prev),
                  pl.BlockSpec((None, tq, SWA_KV_DIM), cur)],
        out_specs=pl.BlockSpec((None, tq, SWA_Q_DIM), cur),
        out_shape=jax.ShapeDtypeStruct((b, s, SWA_Q_DIM), BF16),
        compiler_params=_params(("parallel", "parallel")),
        name="swa_attention",
    )(sinks, q3, k3, k3, v3, v3)
    return out.reshape(b * s, SWA_Q_DIM)


def _pool_mixer(u, halo, seq_row0):
    tm = u.shape[0]
    full = jnp.concatenate([halo, u], axis=0)
    sums = [full]
    for shift in (1, 2, 4, 8):
        prev = sums[-1]
        sums.append(prev + pltpu.roll(prev, shift, 0))
    lane = lax.broadcasted_iota(I32, (tm, POOL_WIDTH), 1)
    row = lax.broadcasted_iota(I32, (tm, POOL_WIDTH), 0)
    grp = lane // POOL_GROUP
    win = sums[4][POOL_HALO:]
    width = jnp.full((tm, POOL_WIDTH), POOL_WINDOWS[3], I32)
    for gi in (2, 1, 0):
        win = jnp.where(grp == gi, sums[gi + 1][POOL_HALO:], win)
        width = jnp.where(grp == gi, POOL_WINDOWS[gi], width)
    count = jnp.minimum(seq_row0 + row + 1, width).astype(F32)
    return win / count - u


def _swiglu_tile(xb, wg, wu, wd):
    g = jnp.dot(xb, wg, preferred_element_type=F32)
    u = jnp.dot(xb, wu, preferred_element_type=F32)
    h = (g * jax.nn.sigmoid(g)) * u
    return jnp.dot(h.astype(BF16), wd, preferred_element_type=F32)


def _layer0_tail_kernel(u_ref, uh_ref, o_ref, x_ref, pw_ref, ps_ref, wa_ref, wb_ref, lw0_ref, lb0_ref,
                        wg_ref, wu_ref, wd_ref, lw1_ref, lb1_ref, out_ref, *, tm):
    i = pl.program_id(1)
    halo = jnp.where(i == 0, 0.0, uh_ref[...])
    d = _pool_mixer(u_ref[...], halo, i * tm)
    a = jnp.dot(d.astype(BF16), pw_ref[...], preferred_element_type=F32) * ps_ref[...]
    mix = jnp.dot(a.astype(BF16), wa_ref[...], preferred_element_type=F32)
    mix = mix + jnp.dot(o_ref[...], wb_ref[...], preferred_element_type=F32)
    x1 = _layer_norm(ALPHA * x_ref[...] + mix, lw0_ref[...], lb0_ref[...])
    ffn = _swiglu_tile(x1.astype(BF16), wg_ref[...], wu_ref[...], wd_ref[...])
    out_ref[...] = _layer_norm(ALPHA * x1 + ffn, lw1_ref[...], lb1_ref[...])


def _layer0_tail(u, attn, x2d, pool_bd, pool_scale, wa, wb, ln0, wg, wu, wd, ln1, b, s, tm=512):
    d = x2d.shape[1]
    f = wg.shape[1]
    per = tm // POOL_HALO
    cur = lambda bi, i: (bi, i, 0)
    prev = lambda bi, i: (bi, jnp.maximum(i * per - 1, 0), 0)
    const = lambda bi, i: (0, 0)
    out = pl.pallas_call(
        functools.partial(_layer0_tail_kernel, tm=tm),
        grid=(b, s // tm),
        in_specs=[pl.BlockSpec((None, tm, POOL_WIDTH), cur),
                  pl.BlockSpec((None, POOL_HALO, POOL_WIDTH), prev),
                  pl.BlockSpec((None, tm, SWA_Q_DIM), cur),
                  pl.BlockSpec((None, tm, d), cur),
                  pl.BlockSpec((POOL_WIDTH, POOL_WIDTH), const),
                  pl.BlockSpec((1, POOL_WIDTH), const),
                  pl.BlockSpec((POOL_WIDTH, d), const),
                  pl.BlockSpec((SWA_Q_DIM, d), const),
                  pl.BlockSpec((1, d), const),
                  pl.BlockSpec((1, d), const),
                  pl.BlockSpec((d, f), const),
                  pl.BlockSpec((d, f), const),
                  pl.BlockSpec((f, d), const),
                  pl.BlockSpec((1, d), const),
                  pl.BlockSpec((1, d), const)],
        out_specs=pl.BlockSpec((None, tm, d), cur),
        out_shape=jax.ShapeDtypeStruct((b, s, d), F32),
        compiler_params=_params(("parallel", "parallel"), vmem=VMEM_LIMIT_BIG),
        name="layer0_tail",
    )(u.reshape(b, s, POOL_WIDTH), u.reshape(b, s, POOL_WIDTH), attn.reshape(b, s, SWA_Q_DIM),
      x2d.reshape(b, s, d), pool_bd, pool_scale, wa, wb, *ln0, wg, wu, wd, *ln1)
    return out.reshape(b * s, d)


def _kv_operands(kv, low):
    swapped = pltpu.roll(kv, HEAD_DIM, 1)
    return (jnp.where(low, kv, 0.0), jnp.where(low, 0.0, swapped),
            jnp.where(low, swapped, 0.0), jnp.where(low, 0.0, kv))


def _inproj1_kernel(x_ref, wq_ref, wkv_ref, wc_ref, cw_ref, cos_ref, sin_ref,
                    q0_ref, q1_ref, q2_ref, kv0_ref, kv1_ref, kv2_ref, d_ref,
                    zc_ref, sq_ref, skv_ref, *, tm):
    i = pl.program_id(1)
    xb = x_ref[...].astype(BF16)
    cos = cos_ref[...]
    sin = sin_ref[...]
    lane = lax.broadcasted_iota(I32, (tm, LANES), 1)
    low = lane < HEAD_DIM
    cos_k = jnp.where(low, cos, 1.0)
    sin_k = jnp.where(low, sin, 0.0)
    q_refs = (q0_ref, q1_ref, q2_ref)
    kv_refs = (kv0_ref, kv1_ref, kv2_ref)
    for gi, (_, dil) in enumerate(DIL_PAIRS):
        q = jnp.dot(xb, wq_ref[gi], preferred_element_type=F32)
        q = _rope(q, cos, sin) * QK_SCALE
        kv = jnp.dot(xb, wkv_ref[gi], preferred_element_type=F32)
        kv = _rope(kv, cos_k, sin_k)
        kvx = [_kv_operands(kv[:, h * LANES:(h + 1) * LANES], low) for h in range(DIL_KV_HEADS)]
        if dil == 1:
            q_refs[gi][0] = q.astype(BF16)
            for h in range(DIL_KV_HEADS):
                kv_refs[gi][h, 0] = jnp.concatenate(kvx[h], axis=1).astype(BF16)
        else:
            n = tm // dil
            for c in range(DIL_Q_DIM // LANES):
                sq_ref[c] = q[:, c * LANES:(c + 1) * LANES]
            for h in range(DIL_KV_HEADS):
                for part in range(KV_PARTS):
                    skv_ref[h * KV_PARTS + part] = kvx[h][part]
            for r in range(dil):
                rows = pl.ds(r, n, stride=dil)
                q_refs[gi][r] = jnp.concatenate(
                    [sq_ref[c, rows, :] for c in range(DIL_Q_DIM // LANES)], axis=1).astype(BF16)
                for h in range(DIL_KV_HEADS):
                    kv_refs[gi][h, r] = jnp.concatenate(
                        [skv_ref[h * KV_PARTS + part, rows, :] for part in range(KV_PARTS)], axis=1).astype(BF16)

    hc = jnp.dot(xb, wc_ref[...], preferred_element_type=F32)
    z = hc[:, 2 * CONV_WIDTH:] * hc[:, :CONV_WIDTH]
    zprev = jnp.where(i == 0, 0.0, zc_ref[...])
    zfull = jnp.concatenate([zprev, z], axis=0)
    z1 = pltpu.roll(zfull, 1, 0)[CONV_HALO:]
    z2 = pltpu.roll(zfull, 2, 0)[CONV_HALO:]
    cw = cw_ref[...]
    y = cw[0:1] * z2 + cw[1:2] * z1 + cw[2:3] * z
    d_ref[...] = (hc[:, CONV_WIDTH:2 * CONV_WIDTH] * y).astype(BF16)
    zc_ref[...] = z[tm - CONV_HALO:]


def _inproj1(x2d, wq, wkv, wc, conv_w, cos_t, sin_t, b, s, tm=512):
    d = x2d.shape[1]
    cur = lambda bi, i: (bi, i, 0)
    row = lambda bi, i: (bi * (s // tm) + i, 0)
    in_specs = [pl.BlockSpec((None, tm, d), cur),
                pl.BlockSpec(wq.shape, lambda bi, i: (0, 0, 0)),
                pl.BlockSpec(wkv.shape, lambda bi, i: (0, 0, 0)),
                pl.BlockSpec(wc.shape, lambda bi, i: (0, 0)),
                pl.BlockSpec(conv_w.shape, lambda bi, i: (0, 0)),
                pl.BlockSpec((tm, LANES), row),
                pl.BlockSpec((tm, LANES), row)]
    out_specs, out_shape = [], []
    for _, dil in DIL_PAIRS:
        out_specs.append(pl.BlockSpec((None, dil, tm // dil, DIL_Q_DIM), lambda bi, i: (bi, 0, i, 0)))
        out_shape.append(jax.ShapeDtypeStruct((b, dil, s // dil, DIL_Q_DIM), BF16))
    for _, dil in DIL_PAIRS:
        out_specs.append(pl.BlockSpec((None, DIL_KV_HEADS, dil, tm // dil, KV_PARTS * LANES),
                                      lambda bi, i: (bi, 0, 0, i, 0)))
        out_shape.append(jax.ShapeDtypeStruct((b, DIL_KV_HEADS, dil, s // dil, KV_PARTS * LANES), BF16))
    out_specs.append(pl.BlockSpec((None, tm, CONV_WIDTH), cur))
    out_shape.append(jax.ShapeDtypeStruct((b, s, CONV_WIDTH), BF16))
    return pl.pallas_call(
        functools.partial(_inproj1_kernel, tm=tm),
        grid=(b, s // tm),
        in_specs=in_specs,
        out_specs=out_specs,
        out_shape=out_shape,
        scratch_shapes=[pltpu.VMEM((CONV_HALO, CONV_WIDTH), F32),
                        pltpu.VMEM((DIL_Q_DIM // LANES, tm, LANES), F32),
                        pltpu.VMEM((DIL_KV_HEADS * KV_PARTS, tm, LANES), F32)],
        compiler_params=_params(("parallel", "arbitrary")),
        name="inproj1",
    )(x2d.reshape(b, s, d), wq, wkv, wc, conv_w, cos_t, sin_t)


def _dil_kernel(q0_ref, q1_ref, q2_ref, kv0_ref, kv1_ref, kv2_ref, out_ref, o_run, l_run, bias_ref, *, s):
    q_refs = (q0_ref, q1_ref, q2_ref)
    kv_refs = (kv0_ref, kv1_ref, kv2_ref)
    n_blocks = s // ATT_BLOCK
    width = DIL_Q_PER_KV * HEAD_DIM
    n_chunks = width // LANES
    max_dist = DIL_PAIRS[0][0] // DIL_PAIRS[0][1]
    assert all(w // d == max_dist for w, d in DIL_PAIRS)
    stacked = n_chunks * ATT_BLOCK
    bias_ref[0] = _band_bias(stacked, max_dist, 0)
    bias_ref[1] = _band_bias(stacked, max_dist, ATT_BLOCK)
    for gi, (_, dil) in enumerate(DIL_PAIRS):
        q_ref, kv_ref = q_refs[gi], kv_refs[gi]
        blocks_per_sub = (s // dil) // ATT_BLOCK
        shift = blocks_per_sub.bit_length() - 1

        def body(n, carry, q_ref=q_ref, kv_ref=kv_ref, gi=gi, dil=dil, blocks_per_sub=blocks_per_sub, shift=shift):
            res = lax.shift_right_logical(n, shift)
            bi = n & (blocks_per_sub - 1)
            rows = pl.ds(pl.multiple_of(n * ATT_BLOCK, ATT_BLOCK), ATT_BLOCK)
            prow = pl.ds(pl.multiple_of(jnp.maximum(n - 1, 0) * ATT_BLOCK, ATT_BLOCK), ATT_BLOCK)
            kvx = jnp.concatenate([kv_ref[prow, :], kv_ref[rows, :]], axis=0)
            kz, zk, vz, zv = _kv_parts(kvx)
            q4 = q_ref[rows, :]
            qs = jnp.concatenate([q4[:, c * LANES:(c + 1) * LANES] for c in range(n_chunks)], axis=0)
            bias = bias_ref[jnp.where(bi == 0, 1, 0)]
            oa, ma, da = _attn_stack(qs, kz, vz, bias)
            ob, mb, db = _attn_stack(qs, zk, zv, bias)
            low = lax.broadcasted_iota(I32, (n_chunks * ATT_BLOCK, LANES), 1) < HEAD_DIM
            o_st = (oa + ob) * jnp.where(low, 1.0 / da, 1.0 / db)
            l_st = jnp.where(low, ma + jnp.log(da), mb + jnp.log(db))
            start = res + dil * ATT_BLOCK * bi
            if dil == 1:
                tok = pl.ds(pl.multiple_of(start, ATT_BLOCK), ATT_BLOCK)
            else:
                tok = pl.ds(start, ATT_BLOCK, stride=dil)
            for c in range(n_chunks):
                o_c = o_st[c * ATT_BLOCK:(c + 1) * ATT_BLOCK]
                l_c = l_st[c * ATT_BLOCK:(c + 1) * ATT_BLOCK]
                if gi == 0:
                    o_run[c, tok, :] = o_c
                    l_run[c, tok, :] = l_c
                else:
                    o_old = o_run[c, tok, :]
                    l_old = l_run[c, tok, :]
                    m = jnp.maximum(l_old, l_c)
                    a = jnp.exp(l_old - m)
                    bb = jnp.exp(l_c - m)
                    tot = a + bb
                    o_run[c, tok, :] = (o_old * a + o_c * bb) * (1.0 / tot)
                    if gi < len(DIL_PAIRS) - 1:
                        l_run[c, tok, :] = m + jnp.log(tot)
            return carry

        lax.fori_loop(0, n_blocks, body, 0, unroll=4)
    for c in range(n_chunks):
        out_ref[:, c * LANES:(c + 1) * LANES] = o_run[c].astype(BF16)


def _dilated(qs, kvs, b, s):
    width = DIL_Q_PER_KV * HEAD_DIM
    in_specs = [pl.BlockSpec((None, s, width), lambda bi, h: (bi, 0, h)) for _ in DIL_PAIRS]
    in_specs += [pl.BlockSpec((None, None, s, KV_PARTS * LANES), lambda bi, h: (bi, h, 0, 0)) for _ in DIL_PAIRS]
    qs = [q.reshape(b, s, DIL_Q_DIM) for q in qs]
    kvs = [kv.reshape(b, DIL_KV_HEADS, s, KV_PARTS * LANES) for kv in kvs]
    out = pl.pallas_call(
        functools.partial(_dil_kernel, s=s),
        grid=(b, DIL_KV_HEADS),
        in_specs=in_specs,
        out_specs=pl.BlockSpec((None, s, width), lambda bi, h: (bi, 0, h)),
        out_shape=jax.ShapeDtypeStruct((b, s, DIL_Q_DIM), BF16),
        scratch_shapes=[pltpu.VMEM((width // LANES, s, LANES), F32),
                        pltpu.VMEM((width // LANES, s, LANES), F32),
                        pltpu.VMEM((2, (width // LANES) * ATT_BLOCK, 2 * ATT_BLOCK), F32)],
        compiler_params=_params(("parallel", "parallel"), vmem=VMEM_LIMIT_BIG),
        name="dilated_attention",
    )(*qs, *kvs)
    return out.reshape(b * s, DIL_Q_DIM)


def _to_slabs(ref, val, base=0):
    rows, d = val.shape
    per = d // LANES
    for c in range(per):
        ref[pl.ds(base + c, rows, stride=per), :] = val[:, c * LANES:(c + 1) * LANES]


def _from_slabs(ref, rows, base=0, per=ROW_SLAB):
    return jnp.concatenate([ref[pl.ds(base + c, rows, stride=per), :] for c in range(per)], axis=1)


def _route_tile(x, wr, e_ref, g_ref, r_ref, cnt_ref, run_ref, tm):
    i = pl.program_id(0)

    @pl.when(i == 0)
    def _():
        run_ref[...] = jnp.zeros_like(run_ref)

    logits = lax.dot_general(wr, x, (((1,), (1,)), ((), ())),
                             precision=lax.Precision.HIGHEST, preferred_element_type=F32)
    eid = lax.broadcasted_iota(I32, (N_EXPERTS, tm), 0)
    m1 = jnp.max(logits, axis=0, keepdims=True)
    i1 = jnp.min(jnp.where(logits == m1, eid, N_EXPERTS), axis=0, keepdims=True)
    rest = jnp.where(eid == i1, -jnp.inf, logits)
    m2 = jnp.max(rest, axis=0, keepdims=True)
    i2 = jnp.min(jnp.where(rest == m2, eid, N_EXPERTS), axis=0, keepdims=True)
    t2 = jnp.exp(m2 - m1)
    g1 = 1.0 / (1.0 + t2)
    g2 = t2 / (1.0 + t2)
    oh1 = eid == i1
    oh2 = eid == i2
    oh = jnp.where(oh1 | oh2, 1.0, 0.0)
    ri = lax.broadcasted_iota(I32, (tm, tm), 0)
    ci = lax.broadcasted_iota(I32, (tm, tm), 1)
    tri = jnp.where(ri <= ci, 1.0, 0.0).astype(BF16)
    incl = jnp.dot(oh.astype(BF16), tri, preferred_element_type=F32)
    rank = run_ref[:, 0:1] + incl - oh
    r1 = jnp.sum(jnp.where(oh1, rank, 0.0), axis=0, keepdims=True)
    r2 = jnp.sum(jnp.where(oh2, rank, 0.0), axis=0, keepdims=True)
    e_ref[0:1, :] = i1
    e_ref[1:2, :] = i2
    g_ref[0:1, :] = g1
    g_ref[1:2, :] = g2
    r_ref[0:1, :] = r1.astype(I32)
    r_ref[1:2, :] = r2.astype(I32)
    run_ref[...] = run_ref[...] + incl[:, tm - 1:tm]
    cnt_ref[...] = run_ref[...]


def _outproj1_kernel(c_ref, d_ref, x_ref, wa_ref, wb_ref, lw_ref, lb_ref, wr_ref,
                     slab_ref, e_ref, g_ref, r_ref, cnt_ref, run_ref, *, tm):
    mix = jnp.dot(c_ref[...], wa_ref[...], preferred_element_type=F32)
    mix = mix + jnp.dot(d_ref[...], wb_ref[...], preferred_element_type=F32)
    y = _layer_norm(ALPHA * x_ref[...] + mix, lw_ref[...], lb_ref[...])
    _to_slabs(slab_ref, y)
    _route_tile(y, wr_ref[...], e_ref, g_ref, r_ref, cnt_ref, run_ref, tm)


def _outproj1(c, dconv, x2d, wa, wb, ln_w, ln_b, wr_t, tm=512):
    t, d = x2d.shape
    row = lambda i: (i, 0)
    col = lambda i: (0, i)
    const = lambda i: (0, 0)
    return pl.pallas_call(
        functools.partial(_outproj1_kernel, tm=tm),
        grid=(t // tm,),
        in_specs=[pl.BlockSpec((tm, DIL_Q_DIM), row),
                  pl.BlockSpec((tm, CONV_WIDTH), row),
                  pl.BlockSpec((tm, d), row),
                  pl.BlockSpec((DIL_Q_DIM, d), const),
                  pl.BlockSpec((CONV_WIDTH, d), const),
                  pl.BlockSpec((1, d), const),
                  pl.BlockSpec((1, d), const),
                  pl.BlockSpec((N_EXPERTS, d), const)],
        out_specs=[pl.BlockSpec((tm * (d // LANES), LANES), row),
                   pl.BlockSpec((2, tm), col), pl.BlockSpec((2, tm), col), pl.BlockSpec((2, tm), col),
                   pl.BlockSpec((N_EXPERTS, LANES), const)],
        out_shape=[jax.ShapeDtypeStruct((t * (d // LANES), LANES), F32),
                   jax.ShapeDtypeStruct((2, t), I32), jax.ShapeDtypeStruct((2, t), F32),
                   jax.ShapeDtypeStruct((2, t), I32), jax.ShapeDtypeStruct((N_EXPERTS, LANES), F32)],
        scratch_shapes=[pltpu.VMEM((N_EXPERTS, LANES), F32)],
        compiler_params=_params(("arbitrary",)),
        name="outproj1_route",
    )(c, dconv, x2d, wa, wb, ln_w, ln_b, wr_t)


def _slab_copy(src_ref, dst_ref, sem, src_row, dst_row, n=1):
    src = src_ref.at[pl.ds(pl.multiple_of(src_row * ROW_SLAB, ROW_SLAB), n * ROW_SLAB)]
    dst = dst_ref.at[pl.ds(pl.multiple_of(dst_row * ROW_SLAB, ROW_SLAB), n * ROW_SLAB)]
    return pltpu.make_async_copy(src, dst, sem)


def _moe_kernel(be_ref, nv_ref, tok_ref, tokn_ref, dstp_ref, dstc_ref, x_hbm, wg_ref, wu_ref, wd_ref, y_hbm,
                xs_ref, xb_ref, acc_ref, stage_ref, sem_in, sem_out, *, tm, nj, dump_row0):
    i = pl.program_id(0)
    j = pl.program_id(1)
    n_valid = nv_ref[0]
    slot = i & 1
    per_step = tm // nj

    def row_in(tok, dst_slot, r):
        return _slab_copy(x_hbm, xs_ref, sem_in.at[dst_slot], tok, dst_slot * tm + r)

    def row_out(src_slot, r, dst):
        return _slab_copy(stage_ref, y_hbm, sem_out.at[src_slot], src_slot * tm + r, dst)

    @pl.when(i < n_valid)
    def _():
        @pl.when(j == 0)
        def _():
            @pl.when(i == 0)
            def _():
                def start(r, c):
                    row_in(tok_ref[0, 0, r], 0, r).start()
                    return c
                lax.fori_loop(0, tm, start, 0)
                stage_ref[...] = jnp.zeros_like(stage_ref)
                fill = _slab_copy(stage_ref, y_hbm, sem_out.at[0], 0, dump_row0, n=2 * tm)
                fill.start()
                fill.wait()

            _slab_copy(x_hbm, xs_ref, sem_in.at[slot], 0, slot * tm, n=tm).wait()
            xb_ref[...] = _from_slabs(xs_ref, tm, base=slot * (tm * ROW_SLAB)).astype(BF16)
            acc_ref[...] = jnp.zeros_like(acc_ref)

        for r in range(per_step):
            rr = j * per_step + r
            row_in(tokn_ref[0, 0, rr], 1 - slot, rr).start(priority=1)
            row_out(1 - slot, rr, dstp_ref[0, 0, rr]).start(priority=1)

        acc_ref[...] += _swiglu_tile(xb_ref[...], wg_ref[...], wu_ref[...], wd_ref[...])

        @pl.when(j == nj - 1)
        def _():
            _slab_copy(stage_ref, y_hbm, sem_out.at[1 - slot], (1 - slot) * tm, 0, n=tm).wait()
            _to_slabs(stage_ref, acc_ref[...], base=slot * (tm * ROW_SLAB))

            @pl.when(i + 1 >= n_valid)
            def _():
                _slab_copy(x_hbm, xs_ref, sem_in.at[1 - slot], 0, (1 - slot) * tm, n=tm).wait()

                def start(r, c):
                    row_out(slot, r, dstc_ref[0, 0, r]).start()
                    return c
                lax.fori_loop(0, tm, start, 0)
                _slab_copy(stage_ref, y_hbm, sem_out.at[slot], slot * tm, 0, n=tm).wait()


def _moe_experts(x_slabs, row_tok, row_dst, dump_row0, block_e, n_valid, wg, wu, wd, tm, tf=1792):
    d = wg.shape[1]
    n_rows = row_tok.shape[0]
    n_blocks = n_rows // tm
    f = wg.shape[2]
    nj = f // tf
    y_rows = dump_row0 + 2 * tm

    def live(i, j, be, nv):
        ok = i < nv[0]
        return jnp.where(ok, i, nv[0] - 1), jnp.where(ok, j, nj - 1)

    def w_map(i, j, be, nv):
        ii, jj = live(i, j, be, nv)
        return be[ii], 0, jj

    def wd_map(i, j, be, nv):
        ii, jj = live(i, j, be, nv)
        return be[ii], jj, 0

    def smem(index_map):
        return pl.BlockSpec((1, 1, tm), index_map, memory_space=pltpu.SMEM)

    grid_spec = pltpu.PrefetchScalarGridSpec(
        num_scalar_prefetch=2,
        grid=(n_blocks, nj),
        in_specs=[smem(lambda i, j, be, nv: (i, 0, 0)),
                  smem(lambda i, j, be, nv: (jnp.minimum(i + 1, n_blocks - 1), 0, 0)),
                  smem(lambda i, j, be, nv: (jnp.where(i == 0, n_blocks, i - 1), 0, 0)),
                  smem(lambda i, j, be, nv: (i, 0, 0)),
                  pl.BlockSpec(memory_space=pl.ANY),
                  pl.BlockSpec((None, d, tf), w_map),
                  pl.BlockSpec((None, d, tf), w_map),
                  pl.BlockSpec((None, tf, d), wd_map)],
        out_specs=pl.BlockSpec(memory_space=pl.ANY),
        scratch_shapes=[pltpu.VMEM((2 * tm * ROW_SLAB, LANES), F32),
                        pltpu.VMEM((tm, d), BF16),
                        pltpu.VMEM((tm, d), F32),
                        pltpu.VMEM((2 * tm * ROW_SLAB, LANES), F32),
                        pltpu.SemaphoreType.DMA((2,)),
                        pltpu.SemaphoreType.DMA((2,))],
    )
    tok3 = row_tok.reshape(n_blocks, 1, tm)
    dst3 = row_dst.reshape(n_blocks + 1, 1, tm)
    return pl.pallas_call(
        functools.partial(_moe_kernel, tm=tm, nj=nj, dump_row0=dump_row0),
        grid_spec=grid_spec,
        out_shape=jax.ShapeDtypeStruct((y_rows * ROW_SLAB, LANES), F32),
        compiler_params=_params(("arbitrary", "arbitrary")),
        name="moe_experts",
    )(block_e, n_valid, tok3, tok3, dst3, dst3, x_slabs, wg, wu, wd)


def _combine_kernel(ya_ref, yb_ref, x_ref, g1_ref, g2_ref, lw_ref, lb_ref, out_ref, *, tm):
    ffn = _from_slabs(ya_ref, tm) * g1_ref[...] + _from_slabs(yb_ref, tm) * g2_ref[...]
    out_ref[...] = _layer_norm(ALPHA * _from_slabs(x_ref, tm) + ffn, lw_ref[...], lb_ref[...])


def _combine(y, x_slabs, g1, g2, ln_w, ln_b, tm=512):
    t = g1.shape[0]
    d = ln_w.shape[1]
    row = lambda i: (i, 0)
    return pl.pallas_call(
        functools.partial(_combine_kernel, tm=tm),
        grid=(t // tm,),
        in_specs=[pl.BlockSpec((tm * ROW_SLAB, LANES), row),
                  pl.BlockSpec((tm * ROW_SLAB, LANES), lambda i: (t // tm + i, 0)),
                  pl.BlockSpec((tm * ROW_SLAB, LANES), row),
                  pl.BlockSpec((tm, 1), row),
                  pl.BlockSpec((tm, 1), row),
                  pl.BlockSpec((1, d), lambda i: (0, 0)),
                  pl.BlockSpec((1, d), lambda i: (0, 0))],
        out_specs=pl.BlockSpec((tm, d), row),
        out_shape=jax.ShapeDtypeStruct((t, d), F32),
        compiler_params=_params(("parallel",)),
        name="moe_combine",
    )(y, y, x_slabs, g1.reshape(t, 1), g2.reshape(t, 1), ln_w, ln_b)


def _moe_layer(x_slabs, e_sel, gates, ranks, counts, wg, wu, wd, ln_w, ln_b, tmoe=512):
    t = e_sel.shape[1]
    counts = counts[:, 0].astype(I32)
    padded = (counts + tmoe - 1) // tmoe * tmoe
    pend = jnp.cumsum(padded)
    pstart = pend - padded
    n_blocks = -(-(2 * t + N_EXPERTS * (tmoe - 1)) // tmoe)
    n_rows = n_blocks * tmoe
    first_row = sum(jnp.where(e_sel == e, pstart[e], 0) for e in range(N_EXPERTS))
    dest = first_row + ranks
    dump_row0 = 2 * t
    r = jnp.arange(n_rows, dtype=I32)
    dump = dump_row0 + ((r // tmoe) & 1) * tmoe + r % tmoe
    tok = jnp.arange(t, dtype=I32)
    row_dst = dump.at[dest.reshape(-1)].set(jnp.concatenate([tok, t + tok]))
    row_tok = jnp.where(row_dst < dump_row0, row_dst % t, 0)
    row_dst = jnp.concatenate([row_dst, dump_row0 + tmoe + jnp.arange(tmoe, dtype=I32)])
    block_row0 = jnp.arange(n_blocks, dtype=I32) * tmoe
    block_e = jnp.minimum(jnp.sum((block_row0[:, None] >= pend[None, :]).astype(I32), axis=1), N_EXPERTS - 1)
    n_valid = (pend[-1:] // tmoe).astype(I32)
    y = _moe_experts(x_slabs, row_tok, row_dst, dump_row0, block_e, n_valid, wg, wu, wd, tmoe)
    return _combine(y, x_slabs, gates[0], gates[1], ln_w, ln_b)


def kernel(x, positions, ln_w, ln_b, even_w_in, pool_w, pool_scale, swa_sinks, even_w_out, ffn_w_gate, ffn_w_up, ffn_w_down, odd_w_in, conv_w, odd_w_out, router_w, moe_w_gate, moe_w_up, moe_w_down):
    b, s, d = x.shape
    t = b * s
    x2d = x.reshape(t, d)
    cos_t, sin_t = _rope_tables(positions)
    ln = lambda layer, k: (ln_w[layer, k].reshape(1, d), ln_b[layer, k].reshape(1, d))

    u, q, k, v = _inproj0(x2d, even_w_in[0].astype(BF16), cos_t, sin_t)
    attn = _swa(q, k, v, swa_sinks[0], b, s)
    pool_bd = jax.scipy.linalg.block_diag(*[pool_w[0, gi] for gi in range(len(POOL_WINDOWS))]).astype(BF16)
    w_out0 = even_w_out[0].astype(BF16)
    x2d = _layer0_tail(u, attn, x2d, pool_bd, pool_scale[0].reshape(1, POOL_WIDTH),
                       w_out0[:POOL_WIDTH], w_out0[POOL_WIDTH:], ln(0, 0),
                       ffn_w_gate[0].astype(BF16), ffn_w_up[0].astype(BF16), ffn_w_down[0].astype(BF16),
                       ln(0, 1), b, s)

    w_in1 = odd_w_in[0].astype(BF16)
    c_in = len(DIL_PAIRS) * DIL_GROUP_IN
    wq, wkv = [], []
    for gi in range(len(DIL_PAIRS)):
        g0 = gi * DIL_GROUP_IN
        wq.append(w_in1[:, g0:g0 + DIL_Q_DIM])
        kcol = g0 + DIL_Q_DIM
        vcol = kcol + DIL_KV_DIM
        parts = []
        for h in range(DIL_KV_HEADS):
            parts += [w_in1[:, kcol + h * HEAD_DIM:kcol + (h + 1) * HEAD_DIM],
                      w_in1[:, vcol + h * HEAD_DIM:vcol + (h + 1) * HEAD_DIM]]
        wkv.append(jnp.concatenate(parts, axis=1))
    outs = _inproj1(x2d, jnp.stack(wq), jnp.stack(wkv), w_in1[:, c_in:], conv_w[0], cos_t, sin_t, b, s)
    c_out = _dilated(outs[0:3], outs[3:6], b, s)
    w_out1 = odd_w_out[0].astype(BF16)
    x_slabs, e_sel, gates, ranks, counts = _outproj1(
        c_out, outs[6].reshape(t, CONV_WIDTH), x2d, w_out1[:DIL_Q_DIM], w_out1[DIL_Q_DIM:], *ln(1, 0),
        router_w[0].T)
    x2d = _moe_layer(x_slabs, e_sel, gates, ranks, counts, moe_w_gate[0].astype(BF16),
                     moe_w_up[0].astype(BF16), moe_w_down[0].astype(BF16), *ln(1, 1))
    return x2d.reshape(b, s, d)
```

```python
import functools
import math

import jax
import jax.numpy as jnp
from jax import lax
from jax.experimental import pallas as pl
from jax.experimental.pallas import tpu as pltpu

F32 = jnp.float32
BF16 = jnp.bfloat16
I32 = jnp.int32

HEAD_DIM = 64
ROPE_THETA = 10000.0
ATT_BLOCK = 128
LN_EPS = 1e-5
POOL_WINDOWS = (2, 4, 8, 16)
POOL_GROUP = 64
POOL_WIDTH = 256
POOL_HALO = 16
SWA_WINDOW = 128
SWA_Q_HEADS = 12
SWA_KV_HEADS = 4
SWA_Q_DIM = SWA_Q_HEADS * HEAD_DIM
SWA_KV_DIM = SWA_KV_HEADS * HEAD_DIM
DIL_PAIRS = ((128, 1), (512, 4), (2048, 16))
DIL_Q_HEADS = 8
DIL_KV_HEADS = 2
DIL_Q_DIM = DIL_Q_HEADS * HEAD_DIM
DIL_KV_DIM = DIL_KV_HEADS * HEAD_DIM
DIL_GROUP_IN = DIL_Q_DIM + 2 * DIL_KV_DIM
DIL_Q_PER_KV = DIL_Q_HEADS // DIL_KV_HEADS
CONV_WIDTH = 512
CONV_K = 3
CONV_HALO = 8
N_EXPERTS = 8
DEPTH = 2
ALPHA = (2 * DEPTH) ** 0.25
QK_SCALE = 1.0 / math.sqrt(HEAD_DIM)
NEG_BIG = -1e30
KV_PARTS = 4

LANES = 128
ROW_SLAB = 8
VMEM_LIMIT = 48 * 1024 * 1024
VMEM_LIMIT_BIG = 56 * 1024 * 1024


def _params(sem, vmem=VMEM_LIMIT):
    return pltpu.CompilerParams(dimension_semantics=sem, vmem_limit_bytes=vmem)


def _layer_norm(y, w, b):
    mu = jnp.mean(y, axis=-1, keepdims=True)
    yc = y - mu
    var = jnp.mean(yc * yc, axis=-1, keepdims=True)
    return yc * lax.rsqrt(var + LN_EPS) * w + b


def _rope_chunk(xc, cos, sin_signed, first_half):
    rot = jnp.where(first_half, pltpu.roll(xc, 96, 1), pltpu.roll(xc, 32, 1))
    return xc * cos + rot * sin_signed


def _rope(x, cos, sin_signed):
    tm, c = x.shape
    lane = lax.broadcasted_iota(I32, (tm, LANES), 1)
    first_half = (lane & 32) == 0
    chunks = [_rope_chunk(x[:, i * LANES:(i + 1) * LANES], cos, sin_signed, first_half)
              for i in range(c // LANES)]
    return chunks[0] if len(chunks) == 1 else jnp.concatenate(chunks, axis=1)


def _trig_kernel(pos_ref, inv_ref, cos_ref, sin_ref):
    pos = pos_ref[...].astype(F32)
    ang = inv_ref[...] * pos
    c = jnp.cos(ang)
    s = jnp.sin(ang)
    c4 = jnp.concatenate([c, c, c, c], axis=0)
    s4 = jnp.concatenate([-s, s, -s, s], axis=0)
    cos_ref[...] = c4.T
    sin_ref[...] = s4.T


def _rope_tables(positions, tm=512):
    t = positions.size
    half = HEAD_DIM // 2
    inv = ROPE_THETA ** (-jnp.arange(half, dtype=F32) / half)
    return pl.pallas_call(
        _trig_kernel,
        grid=(t // tm,),
        in_specs=[pl.BlockSpec((1, tm), lambda i: (0, i)),
                  pl.BlockSpec((half, 1), lambda i: (0, 0))],
        out_specs=[pl.BlockSpec((tm, LANES), lambda i: (i, 0)),
                   pl.BlockSpec((tm, LANES), lambda i: (i, 0))],
        out_shape=[jax.ShapeDtypeStruct((t, LANES), F32)] * 2,
        compiler_params=_params(("parallel",)),
        name="rope_tables",
    )(positions.reshape(1, t), inv.reshape(half, 1))


def _inproj0_kernel(x_ref, w_ref, cos_ref, sin_ref, u_ref, q_ref, k_ref, v_ref):
    xb = x_ref[...].astype(BF16)
    cos = cos_ref[...]
    sin = sin_ref[...]
    q0 = POOL_WIDTH
    k0 = q0 + SWA_Q_DIM
    v0 = k0 + SWA_KV_DIM
    u_ref[...] = jnp.dot(xb, w_ref[:, :q0], preferred_element_type=F32)
    q = jnp.dot(xb, w_ref[:, q0:k0], preferred_element_type=F32)
    q_ref[...] = (_rope(q, cos, sin) * QK_SCALE).astype(BF16)
    k = jnp.dot(xb, w_ref[:, k0:v0], preferred_element_type=F32)
    k_ref[...] = _rope(k, cos, sin).astype(BF16)
    v_ref[...] = jnp.dot(xb, w_ref[:, v0:], preferred_element_type=F32).astype(BF16)


def _inproj0(x2d, w_bf, cos_t, sin_t, tm=512):
    t, d = x2d.shape
    n_in = w_bf.shape[1]
    row = lambda i: (i, 0)
    return pl.pallas_call(
        _inproj0_kernel,
        grid=(t // tm,),
        in_specs=[pl.BlockSpec((tm, d), row),
                  pl.BlockSpec((d, n_in), lambda i: (0, 0)),
                  pl.BlockSpec((tm, LANES), row),
                  pl.BlockSpec((tm, LANES), row)],
        out_specs=[pl.BlockSpec((tm, POOL_WIDTH), row),
                   pl.BlockSpec((tm, SWA_Q_DIM), row),
                   pl.BlockSpec((tm, SWA_KV_DIM), row),
                   pl.BlockSpec((tm, SWA_KV_DIM), row)],
        out_shape=[jax.ShapeDtypeStruct((t, POOL_WIDTH), F32),
                   jax.ShapeDtypeStruct((t, SWA_Q_DIM), BF16),
                   jax.ShapeDtypeStruct((t, SWA_KV_DIM), BF16),
                   jax.ShapeDtypeStruct((t, SWA_KV_DIM), BF16)],
        compiler_params=_params(("parallel",)),
        name="inproj0",
    )(x2d, w_bf, cos_t, sin_t)


def _band_bias(rows, max_dist, key_lo):
    qi = lax.broadcasted_iota(I32, (rows, 2 * ATT_BLOCK), 0) & (ATT_BLOCK - 1)
    sj = lax.broadcasted_iota(I32, (rows, 2 * ATT_BLOCK), 1)
    dist = ATT_BLOCK + qi - sj
    return jnp.where((dist >= 0) & (dist <= max_dist) & (sj >= key_lo), 0.0, NEG_BIG)


def _band_mask(max_dist, key_lo):
    qi = lax.broadcasted_iota(I32, (ATT_BLOCK, 2 * ATT_BLOCK), 0)
    sj = lax.broadcasted_iota(I32, (ATT_BLOCK, 2 * ATT_BLOCK), 1)
    dist = ATT_BLOCK + qi - sj
    return (dist >= 0) & (dist <= max_dist) & (sj >= key_lo)


def _attn_block(q, kwin, vwin, valid, sink):
    s = lax.dot_general(q, kwin, (((1,), (1,)), ((), ())), preferred_element_type=F32)
    s = jnp.where(valid, s, NEG_BIG)
    m = jnp.maximum(jnp.max(s, axis=1, keepdims=True), sink)
    p = jnp.exp(s - m)
    den = jnp.sum(p, axis=1, keepdims=True) + jnp.exp(sink - m)
    o = jnp.dot(p.astype(BF16), vwin, preferred_element_type=F32)
    return o * (1.0 / den)


def _attn_stack(qs, kpart, vpart, bias):
    s = lax.dot_general(qs, kpart, (((1,), (1,)), ((), ())), preferred_element_type=F32) + bias
    m = jnp.max(s, axis=1, keepdims=True)
    p = jnp.exp(s - m)
    den = jnp.sum(p, axis=1, keepdims=True)
    return jnp.dot(p.astype(BF16), vpart, preferred_element_type=F32), m, den


def _kv_parts(kvx, h=0):
    base = h * KV_PARTS * LANES
    return [kvx[:, base + part * LANES:base + (part + 1) * LANES] for part in range(KV_PARTS)]


def _swa_kernel(sink_ref, q_ref, kp_ref, kc_ref, vp_ref, vc_ref, o_ref, *, tq):
    i = pl.program_id(1)
    kfull = jnp.concatenate([kp_ref[...], kc_ref[...]], axis=0)
    vfull = jnp.concatenate([vp_ref[...], vc_ref[...]], axis=0)
    g = SWA_Q_HEADS // SWA_KV_HEADS
    for j in range(tq // ATT_BLOCK):
        key_lo = jnp.where(i == 0, ATT_BLOCK, 0) if j == 0 else 0
        valid = _band_mask(SWA_WINDOW - 1, key_lo)
        r0 = j * ATT_BLOCK
        outs = []
        for h in range(SWA_Q_HEADS):
            kv = h // g
            q = q_ref[r0:r0 + ATT_BLOCK, h * HEAD_DIM:(h + 1) * HEAD_DIM]
            kwin = kfull[r0:r0 + 2 * ATT_BLOCK, kv * HEAD_DIM:(kv + 1) * HEAD_DIM]
            vwin = vfull[r0:r0 + 2 * ATT_BLOCK, kv * HEAD_DIM:(kv + 1) * HEAD_DIM]
            outs.append(_attn_block(q, kwin, vwin, valid, sink_ref[h]))
        o_ref[r0:r0 + ATT_BLOCK, :] = jnp.concatenate(outs, axis=1).astype(BF16)


def _swa(q, k, v, sinks, b, s, tq=256):
    per = tq // ATT_BLOCK
    cur = lambda bi, i: (bi, i, 0)
    prev = lambda bi, i: (bi, jnp.maximum(i * per - 1, 0), 0)
    q3 = q.reshape(b, s, SWA_Q_DIM)
    k3 = k.reshape(b, s, SWA_KV_DIM)
    v3 = v.reshape(b, s, SWA_KV_DIM)
    out = pl.pallas_call(
        functools.partial(_swa_kernel, tq=tq),
        grid=(b, s // tq),
        in_specs=[pl.BlockSpec(memory_space=pltpu.SMEM),
                  pl.BlockSpec((None, tq, SWA_Q_DIM), cur),
                  pl.BlockSpec((None, ATT_BLOCK, SWA_KV_DIM), prev),
                  pl.BlockSpec((None, tq, SWA_KV_DIM), cur),
                  pl.BlockSpec((None, ATT_BLOCK, SWA_KV_DIM), prev),
                  pl.BlockSpec((None, tq, SWA_KV_DIM), cur)],
        out_specs=pl.BlockSpec((None, tq, SWA_Q_DIM), cur),
        out_shape=jax.ShapeDtypeStruct((b, s, SWA_Q_DIM), BF16),
        compiler_params=_params(("parallel", "parallel")),
        name="swa_attention",
    )(sinks, q3, k3, k3, v3, v3)
    return out.reshape(b * s, SWA_Q_DIM)


def _pool_mixer(u, halo, seq_row0):
    tm = u.shape[0]
    full = jnp.concatenate([halo, u], axis=0)
    sums = [full]
    for shift in (1, 2, 4, 8):
        prev = sums[-1]
        sums.append(prev + pltpu.roll(prev, shift, 0))
    lane = lax.broadcasted_iota(I32, (tm, POOL_WIDTH), 1)
    row = lax.broadcasted_iota(I32, (tm, POOL_WIDTH), 0)
    grp = lane // POOL_GROUP
    win = sums[4][POOL_HALO:]
    width = jnp.full((tm, POOL_WIDTH), POOL_WINDOWS[3], I32)
    for gi in (2, 1, 0):
        win = jnp.where(grp == gi, sums[gi + 1][POOL_HALO:], win)
        width = jnp.where(grp == gi, POOL_WINDOWS[gi], width)
    count = jnp.minimum(seq_row0 + row + 1, width).astype(F32)
    return win / count - u


def _swiglu_tile(xb, wg, wu, wd):
    g = jnp.dot(xb, wg, preferred_element_type=F32)
    u = jnp.dot(xb, wu, preferred_element_type=F32)
    h = (g * jax.nn.sigmoid(g)) * u
    return jnp.dot(h.astype(BF16), wd, preferred_element_type=F32)


def _layer0_tail_kernel(u_ref, uh_ref, o_ref, x_ref, pw_ref, ps_ref, wa_ref, wb_ref, lw0_ref, lb0_ref,
                        wg_ref, wu_ref, wd_ref, lw1_ref, lb1_ref, out_ref, *, tm):
    i = pl.program_id(1)
    halo = jnp.where(i == 0, 0.0, uh_ref[...])
    d = _pool_mixer(u_ref[...], halo, i * tm)
    a = jnp.dot(d.astype(BF16), pw_ref[...], preferred_element_type=F32) * ps_ref[...]
    mix = jnp.dot(a.astype(BF16), wa_ref[...], preferred_element_type=F32)
    mix = mix + jnp.dot(o_ref[...], wb_ref[...], preferred_element_type=F32)
    x1 = _layer_norm(ALPHA * x_ref[...] + mix, lw0_ref[...], lb0_ref[...])
    ffn = _swiglu_tile(x1.astype(BF16), wg_ref[...], wu_ref[...], wd_ref[...])
    out_ref[...] = _layer_norm(ALPHA * x1 + ffn, lw1_ref[...], lb1_ref[...])


def _layer0_tail(u, attn, x2d, pool_bd, pool_scale, wa, wb, ln0, wg, wu, wd, ln1, b, s, tm=512):
    d = x2d.shape[1]
    f = wg.shape[1]
    per = tm // POOL_HALO
    cur = lambda bi, i: (bi, i, 0)
    prev = lambda bi, i: (bi, jnp.maximum(i * per - 1, 0), 0)
    const = lambda bi, i: (0, 0)
    out = pl.pallas_call(
        functools.partial(_layer0_tail_kernel, tm=tm),
        grid=(b, s // tm),
        in_specs=[pl.BlockSpec((None, tm, POOL_WIDTH), cur),
                  pl.BlockSpec((None, POOL_HALO, POOL_WIDTH), prev),
                  pl.BlockSpec((None, tm, SWA_Q_DIM), cur),
                  pl.BlockSpec((None, tm, d), cur),
                  pl.BlockSpec((POOL_WIDTH, POOL_WIDTH), const),
                  pl.BlockSpec((1, POOL_WIDTH), const),
                  pl.BlockSpec((POOL_WIDTH, d), const),
                  pl.BlockSpec((SWA_Q_DIM, d), const),
                  pl.BlockSpec((1, d), const),
                  pl.BlockSpec((1, d), const),
                  pl.BlockSpec((d, f), const),
                  pl.BlockSpec((d, f), const),
                  pl.BlockSpec((f, d), const),
                  pl.BlockSpec((1, d), const),
                  pl.BlockSpec((1, d), const)],
        out_specs=pl.BlockSpec((None, tm, d), cur),
        out_shape=jax.ShapeDtypeStruct((b, s, d), F32),
        compiler_params=_params(("parallel", "parallel"), vmem=VMEM_LIMIT_BIG),
        name="layer0_tail",
    )(u.reshape(b, s, POOL_WIDTH), u.reshape(b, s, POOL_WIDTH), attn.reshape(b, s, SWA_Q_DIM),
      x2d.reshape(b, s, d), pool_bd, pool_scale, wa, wb, *ln0, wg, wu, wd, *ln1)
    return out.reshape(b * s, d)


def _kv_operands(kv, low):
    swapped = pltpu.roll(kv, HEAD_DIM, 1)
    return (jnp.where(low, kv, 0.0), jnp.where(low, 0.0, swapped),
            jnp.where(low, swapped, 0.0), jnp.where(low, 0.0, kv))


def _inproj1_kernel(x_ref, wq_ref, wkv_ref, wc_ref, cw_ref, cos_ref, sin_ref,
                    q0_ref, q1_ref, q2_ref, kv0_ref, kv1_ref, kv2_ref, d_ref,
                    zc_ref, sq_ref, skv_ref, *, tm):
    i = pl.program_id(1)
    xb = x_ref[...].astype(BF16)
    cos = cos_ref[...]
    sin = sin_ref[...]
    lane = lax.broadcasted_iota(I32, (tm, LANES), 1)
    low = lane < HEAD_DIM
    cos_k = jnp.where(low, cos, 1.0)
    sin_k = jnp.where(low, sin, 0.0)
    q_refs = (q0_ref, q1_ref, q2_ref)
    kv_refs = (kv0_ref, kv1_ref, kv2_ref)
    for gi, (_, dil) in enumerate(DIL_PAIRS):
        q = jnp.dot(xb, wq_ref[gi], preferred_element_type=F32)
        q = _rope(q, cos, sin) * QK_SCALE
        kv = jnp.dot(xb, wkv_ref[gi], preferred_element_type=F32)
        kv = _rope(kv, cos_k, sin_k)
        kvx = [_kv_operands(kv[:, h * LANES:(h + 1) * LANES], low) for h in range(DIL_KV_HEADS)]
        if dil == 1:
            q_refs[gi][0] = q.astype(BF16)
            for h in range(DIL_KV_HEADS):
                kv_refs[gi][h, 0] = jnp.concatenate(kvx[h], axis=1).astype(BF16)
        else:
            n = tm // dil
            for c in range(DIL_Q_DIM // LANES):
                sq_ref[c] = q[:, c * LANES:(c + 1) * LANES]
            for h in range(DIL_KV_HEADS):
                for part in range(KV_PARTS):
                    skv_ref[h * KV_PARTS + part] = kvx[h][part]
            for r in range(dil):
                rows = pl.ds(r, n, stride=dil)
                q_refs[gi][r] = jnp.concatenate(
                    [sq_ref[c, rows, :] for c in range(DIL_Q_DIM // LANES)], axis=1).astype(BF16)
                for h in range(DIL_KV_HEADS):
                    kv_refs[gi][h, r] = jnp.concatenate(
                        [skv_ref[h * KV_PARTS + part, rows, :] for part in range(KV_PARTS)], axis=1).astype(BF16)

    hc = jnp.dot(xb, wc_ref[...], preferred_element_type=F32)
    z = hc[:, 2 * CONV_WIDTH:] * hc[:, :CONV_WIDTH]
    zprev = jnp.where(i == 0, 0.0, zc_ref[...])
    zfull = jnp.concatenate([zprev, z], axis=0)
    z1 = pltpu.roll(zfull, 1, 0)[CONV_HALO:]
    z2 = pltpu.roll(zfull, 2, 0)[CONV_HALO:]
    cw = cw_ref[...]
    y = cw[0:1] * z2 + cw[1:2] * z1 + cw[2:3] * z
    d_ref[...] = (hc[:, CONV_WIDTH:2 * CONV_WIDTH] * y).astype(BF16)
    zc_ref[...] = z[tm - CONV_HALO:]


def _inproj1(x2d, wq, wkv, wc, conv_w, cos_t, sin_t, b, s, tm=512):
    d = x2d.shape[1]
    cur = lambda bi, i: (bi, i, 0)
    row = lambda bi, i: (bi * (s // tm) + i, 0)
    in_specs = [pl.BlockSpec((None, tm, d), cur),
                pl.BlockSpec(wq.shape, lambda bi, i: (0, 0, 0)),
                pl.BlockSpec(wkv.shape, lambda bi, i: (0, 0, 0)),
                pl.BlockSpec(wc.shape, lambda bi, i: (0, 0)),
                pl.BlockSpec(conv_w.shape, lambda bi, i: (0, 0)),
                pl.BlockSpec((tm, LANES), row),
                pl.BlockSpec((tm, LANES), row)]
    out_specs, out_shape = [], []
    for _, dil in DIL_PAIRS:
        out_specs.append(pl.BlockSpec((None, dil, tm // dil, DIL_Q_DIM), lambda bi, i: (bi, 0, i, 0)))
        out_shape.append(jax.ShapeDtypeStruct((b, dil, s // dil, DIL_Q_DIM), BF16))
    for _, dil in DIL_PAIRS:
        out_specs.append(pl.BlockSpec((None, DIL_KV_HEADS, dil, tm // dil, KV_PARTS * LANES),
                                      lambda bi, i: (bi, 0, 0, i, 0)))
        out_shape.append(jax.ShapeDtypeStruct((b, DIL_KV_HEADS, dil, s // dil, KV_PARTS * LANES), BF16))
    out_specs.append(pl.BlockSpec((None, tm, CONV_WIDTH), cur))
    out_shape.append(jax.ShapeDtypeStruct((b, s, CONV_WIDTH), BF16))
    return pl.pallas_call(
        functools.partial(_inproj1_kernel, tm=tm),
        grid=(b, s // tm),
        in_specs=in_specs,
        out_specs=out_specs,
        out_shape=out_shape,
        scratch_shapes=[pltpu.VMEM((CONV_HALO, CONV_WIDTH), F32),
                        pltpu.VMEM((DIL_Q_DIM // LANES, tm, LANES), F32),
                        pltpu.VMEM((DIL_KV_HEADS * KV_PARTS, tm, LANES), F32)],
        compiler_params=_params(("parallel", "arbitrary")),
        name="inproj1",
    )(x2d.reshape(b, s, d), wq, wkv, wc, conv_w, cos_t, sin_t)


def _dil_kernel(q0_ref, q1_ref, q2_ref, kv0_ref, kv1_ref, kv2_ref, out_ref, o_run, l_run, bias_ref, *, s):
    q_refs = (q0_ref, q1_ref, q2_ref)
    kv_refs = (kv0_ref, kv1_ref, kv2_ref)
    n_blocks = s // ATT_BLOCK
    width = DIL_Q_PER_KV * HEAD_DIM
    n_chunks = width // LANES
    max_dist = DIL_PAIRS[0][0] // DIL_PAIRS[0][1]
    assert all(w // d == max_dist for w, d in DIL_PAIRS)
    stacked = n_chunks * ATT_BLOCK
    bias_ref[0] = _band_bias(stacked, max_dist, 0)
    bias_ref[1] = _band_bias(stacked, max_dist, ATT_BLOCK)
    for gi, (_, dil) in enumerate(DIL_PAIRS):
        q_ref, kv_ref = q_refs[gi], kv_refs[gi]
        blocks_per_sub = (s // dil) // ATT_BLOCK
        shift = blocks_per_sub.bit_length() - 1

        def body(n, carry, q_ref=q_ref, kv_ref=kv_ref, gi=gi, dil=dil, blocks_per_sub=blocks_per_sub, shift=shift):
            res = lax.shift_right_logical(n, shift)
            bi = n & (blocks_per_sub - 1)
            rows = pl.ds(pl.multiple_of(n * ATT_BLOCK, ATT_BLOCK), ATT_BLOCK)
            prow = pl.ds(pl.multiple_of(jnp.maximum(n - 1, 0) * ATT_BLOCK, ATT_BLOCK), ATT_BLOCK)
            kvx = jnp.concatenate([kv_ref[prow, :], kv_ref[rows, :]], axis=0)
            kz, zk, vz, zv = _kv_parts(kvx)
            q4 = q_ref[rows, :]
            qs = jnp.concatenate([q4[:, c * LANES:(c + 1) * LANES] for c in range(n_chunks)], axis=0)
            bias = bias_ref[jnp.where(bi == 0, 1, 0)]
            oa, ma, da = _attn_stack(qs, kz, vz, bias)
            ob, mb, db = _attn_stack(qs, zk, zv, bias)
            low = lax.broadcasted_iota(I32, (n_chunks * ATT_BLOCK, LANES), 1) < HEAD_DIM
            o_st = (oa + ob) * jnp.where(low, 1.0 / da, 1.0 / db)
            l_st = jnp.where(low, ma + jnp.log(da), mb + jnp.log(db))
            start = res + dil * ATT_BLOCK * bi
            if dil == 1:
                tok = pl.ds(pl.multiple_of(start, ATT_BLOCK), ATT_BLOCK)
            else:
                tok = pl.ds(start, ATT_BLOCK, stride=dil)
            for c in range(n_chunks):
                o_c = o_st[c * ATT_BLOCK:(c + 1) * ATT_BLOCK]
                l_c = l_st[c * ATT_BLOCK:(c + 1) * ATT_BLOCK]
                if gi == 0:
                    o_run[c, tok, :] = o_c
                    l_run[c, tok, :] = l_c
                else:
                    o_old = o_run[c, tok, :]
                    l_old = l_run[c, tok, :]
                    m = jnp.maximum(l_old, l_c)
                    a = jnp.exp(l_old - m)
                    bb = jnp.exp(l_c - m)
                    tot = a + bb
                    o_run[c, tok, :] = (o_old * a + o_c * bb) * (1.0 / tot)
                    if gi < len(DIL_PAIRS) - 1:
                        l_run[c, tok, :] = m + jnp.log(tot)
            return carry

        lax.fori_loop(0, n_blocks, body, 0, unroll=4)
    for c in range(n_chunks):
        out_ref[:, c * LANES:(c + 1) * LANES] = o_run[c].astype(BF16)


def _dilated(qs, kvs, b, s):
    width = DIL_Q_PER_KV * HEAD_DIM
    in_specs = [pl.BlockSpec((None, s, width), lambda bi, h: (bi, 0, h)) for _ in DIL_PAIRS]
    in_specs += [pl.BlockSpec((None, None, s, KV_PARTS * LANES), lambda bi, h: (bi, h, 0, 0)) for _ in DIL_PAIRS]
    qs = [q.reshape(b, s, DIL_Q_DIM) for q in qs]
    kvs = [kv.reshape(b, DIL_KV_HEADS, s, KV_PARTS * LANES) for kv in kvs]
    out = pl.pallas_call(
        functools.partial(_dil_kernel, s=s),
        grid=(b, DIL_KV_HEADS),
        in_specs=in_specs,
        out_specs=pl.BlockSpec((None, s, width), lambda bi, h: (bi, 0, h)),
        out_shape=jax.ShapeDtypeStruct((b, s, DIL_Q_DIM), BF16),
        scratch_shapes=[pltpu.VMEM((width // LANES, s, LANES), F32),
                        pltpu.VMEM((width // LANES, s, LANES), F32),
                        pltpu.VMEM((2, (width // LANES) * ATT_BLOCK, 2 * ATT_BLOCK), F32)],
        compiler_params=_params(("parallel", "parallel"), vmem=VMEM_LIMIT_BIG),
        name="dilated_attention",
    )(*qs, *kvs)
    return out.reshape(b * s, DIL_Q_DIM)


def _to_slabs(ref, val, base=0):
    rows, d = val.shape
    per = d // LANES
    for c in range(per):
        ref[pl.ds(base + c, rows, stride=per), :] = val[:, c * LANES:(c + 1) * LANES]


def _from_slabs(ref, rows, base=0, per=ROW_SLAB):
    return jnp.concatenate([ref[pl.ds(base + c, rows, stride=per), :] for c in range(per)], axis=1)


def _route_tile(x, wr, e_ref, g_ref, r_ref, cnt_ref, run_ref, tm):
    i = pl.program_id(0)

    @pl.when(i == 0)
    def _():
        run_ref[...] = jnp.zeros_like(run_ref)

    logits = lax.dot_general(wr, x, (((1,), (1,)), ((), ())),
                             precision=lax.Precision.HIGHEST, preferred_element_type=F32)
    eid = lax.broadcasted_iota(I32, (N_EXPERTS, tm), 0)
    m1 = jnp.max(logits, axis=0, keepdims=True)
    i1 = jnp.min(jnp.where(logits == m1, eid, N_EXPERTS), axis=0, keepdims=True)
    rest = jnp.where(eid == i1, -jnp.inf, logits)
    m2 = jnp.max(rest, axis=0, keepdims=True)
    i2 = jnp.min(jnp.where(rest == m2, eid, N_EXPERTS), axis=0, keepdims=True)
    t2 = jnp.exp(m2 - m1)
    g1 = 1.0 / (1.0 + t2)
    g2 = t2 / (1.0 + t2)
    oh1 = eid == i1
    oh2 = eid == i2
    oh = jnp.where(oh1 | oh2, 1.0, 0.0)
    ri = lax.broadcasted_iota(I32, (tm, tm), 0)
    ci = lax.broadcasted_iota(I32, (tm, tm), 1)
    tri = jnp.where(ri <= ci, 1.0, 0.0).astype(BF16)
    incl = jnp.dot(oh.astype(BF16), tri, preferred_element_type=F32)
    rank = run_ref[:, 0:1] + incl - oh
    r1 = jnp.sum(jnp.where(oh1, rank, 0.0), axis=0, keepdims=True)
    r2 = jnp.sum(jnp.where(oh2, rank, 0.0), axis=0, keepdims=True)
    e_ref[0:1, :] = i1
    e_ref[1:2, :] = i2
    g_ref[0:1, :] = g1
    g_ref[1:2, :] = g2
    r_ref[0:1, :] = r1.astype(I32)
    r_ref[1:2, :] = r2.astype(I32)
    run_ref[...] = run_ref[...] + incl[:, tm - 1:tm]
    cnt_ref[...] = run_ref[...]


def _outproj1_kernel(c_ref, d_ref, x_ref, wa_ref, wb_ref, lw_ref, lb_ref, wr_ref,
                     slab_ref, e_ref, g_ref, r_ref, cnt_ref, run_ref, *, tm):
    mix = jnp.dot(c_ref[...], wa_ref[...], preferred_element_type=F32)
    mix = mix + jnp.dot(d_ref[...], wb_ref[...], preferred_element_type=F32)
    y = _layer_norm(ALPHA * x_ref[...] + mix, lw_ref[...], lb_ref[...])
    _to_slabs(slab_ref, y)
    _route_tile(y, wr_ref[...], e_ref, g_ref, r_ref, cnt_ref, run_ref, tm)


def _outproj1(c, dconv, x2d, wa, wb, ln_w, ln_b, wr_t, tm=512):
    t, d = x2d.shape
    row = lambda i: (i, 0)
    col = lambda i: (0, i)
    const = lambda i: (0, 0)
    return pl.pallas_call(
        functools.partial(_outproj1_kernel, tm=tm),
        grid=(t // tm,),
        in_specs=[pl.BlockSpec((tm, DIL_Q_DIM), row),
                  pl.BlockSpec((tm, CONV_WIDTH), row),
                  pl.BlockSpec((tm, d), row),
                  pl.BlockSpec((DIL_Q_DIM, d), const),
                  pl.BlockSpec((CONV_WIDTH, d), const),
                  pl.BlockSpec((1, d), const),
                  pl.BlockSpec((1, d), const),
                  pl.BlockSpec((N_EXPERTS, d), const)],
        out_specs=[pl.BlockSpec((tm * (d // LANES), LANES), row),
                   pl.BlockSpec((2, tm), col), pl.BlockSpec((2, tm), col), pl.BlockSpec((2, tm), col),
                   pl.BlockSpec((N_EXPERTS, LANES), const)],
        out_shape=[jax.ShapeDtypeStruct((t * (d // LANES), LANES), F32),
                   jax.ShapeDtypeStruct((2, t), I32), jax.ShapeDtypeStruct((2, t), F32),
                   jax.ShapeDtypeStruct((2, t), I32), jax.ShapeDtypeStruct((N_EXPERTS, LANES), F32)],
        scratch_shapes=[pltpu.VMEM((N_EXPERTS, LANES), F32)],
        compiler_params=_params(("arbitrary",)),
        name="outproj1_route",
    )(c, dconv, x2d, wa, wb, ln_w, ln_b, wr_t)


def _slab_copy(src_ref, dst_ref, sem, src_row, dst_row, n=1):
    src = src_ref.at[pl.ds(pl.multiple_of(src_row * ROW_SLAB, ROW_SLAB), n * ROW_SLAB)]
    dst = dst_ref.at[pl.ds(pl.multiple_of(dst_row * ROW_SLAB, ROW_SLAB), n * ROW_SLAB)]
    return pltpu.make_async_copy(src, dst, sem)


def _moe_kernel(be_ref, nv_ref, tok_ref, tokn_ref, dstp_ref, dstc_ref, x_hbm, wg_ref, wu_ref, wd_ref, y_hbm,
                xs_ref, xb_ref, acc_ref, stage_ref, sem_in, sem_out, *, tm, nj, dump_row0):
    i = pl.program_id(0)
    j = pl.program_id(1)
    n_valid = nv_ref[0]
    slot = i & 1
    per_step = tm // nj

    def row_in(tok, dst_slot, r):
        return _slab_copy(x_hbm, xs_ref, sem_in.at[dst_slot], tok, dst_slot * tm + r)

    def row_out(src_slot, r, dst):
        return _slab_copy(stage_ref, y_hbm, sem_out.at[src_slot], src_slot * tm + r, dst)

    @pl.when(i < n_valid)
    def _():
        @pl.when(j == 0)
        def _():
            @pl.when(i == 0)
            def _():
                def start(r, c):
                    row_in(tok_ref[0, 0, r], 0, r).start()
                    return c
                lax.fori_loop(0, tm, start, 0)
                stage_ref[...] = jnp.zeros_like(stage_ref)
                fill = _slab_copy(stage_ref, y_hbm, sem_out.at[0], 0, dump_row0, n=2 * tm)
                fill.start()
                fill.wait()

            _slab_copy(x_hbm, xs_ref, sem_in.at[slot], 0, slot * tm, n=tm).wait()
            xb_ref[...] = _from_slabs(xs_ref, tm, base=slot * (tm * ROW_SLAB)).astype(BF16)

        for r in range(per_step):
            rr = j * per_step + r
            row_in(tokn_ref[0, 0, rr], 1 - slot, rr).start(priority=1)
            row_out(1 - slot, rr, dstp_ref[0, 0, rr]).start(priority=1)

        part = _swiglu_tile(xb_ref[...], wg_ref[...], wu_ref[...], wd_ref[...])
        if nj > 1:
            @pl.when(j == 0)
            def _():
                acc_ref[...] = part

            @pl.when(j > 0)
            def _():
                acc_ref[...] += part

        @pl.when(j == nj - 1)
        def _():
            _slab_copy(stage_ref, y_hbm, sem_out.at[1 - slot], (1 - slot) * tm, 0, n=tm).wait()
            _to_slabs(stage_ref, acc_ref[...] if nj > 1 else part, base=slot * (tm * ROW_SLAB))

            @pl.when(i + 1 >= n_valid)
            def _():
                _slab_copy(x_hbm, xs_ref, sem_in.at[1 - slot], 0, (1 - slot) * tm, n=tm).wait()

                def start(r, c):
                    row_out(slot, r, dstc_ref[0, 0, r]).start()
                    return c
                lax.fori_loop(0, tm, start, 0)
                _slab_copy(stage_ref, y_hbm, sem_out.at[slot], slot * tm, 0, n=tm).wait()


def _moe_experts(x_slabs, row_tok, row_dst, dump_row0, block_e, n_valid, wg, wu, wd, tm, tf=None):
    d = wg.shape[1]
    n_rows = row_tok.shape[0]
    n_blocks = n_rows // tm
    f = wg.shape[2]
    tf = f if tf is None else tf
    nj = f // tf
    y_rows = dump_row0 + 2 * tm
    w_mode = dict(pipeline_mode=pl.Buffered(1)) if nj == 1 else {}

    def live(i, j, be, nv):
        ok = i < nv[0]
        return jnp.where(ok, i, nv[0] - 1), jnp.where(ok, j, nj - 1)

    def w_map(i, j, be, nv):
        ii, jj = live(i, j, be, nv)
        return be[ii], 0, jj

    def wd_map(i, j, be, nv):
        ii, jj = live(i, j, be, nv)
        return be[ii], jj, 0

    def smem(index_map):
        return pl.BlockSpec((1, 1, tm), index_map, memory_space=pltpu.SMEM)

    grid_spec = pltpu.PrefetchScalarGridSpec(
        num_scalar_prefetch=2,
        grid=(n_blocks, nj),
        in_specs=[smem(lambda i, j, be, nv: (i, 0, 0)),
                  smem(lambda i, j, be, nv: (jnp.minimum(i + 1, n_blocks - 1), 0, 0)),
                  smem(lambda i, j, be, nv: (jnp.where(i == 0, n_blocks, i - 1), 0, 0)),
                  smem(lambda i, j, be, nv: (i, 0, 0)),
                  pl.BlockSpec(memory_space=pl.ANY),
                  pl.BlockSpec((None, d, tf), w_map, **w_mode),
                  pl.BlockSpec((None, d, tf), w_map, **w_mode),
                  pl.BlockSpec((None, tf, d), wd_map, **w_mode)],
        out_specs=pl.BlockSpec(memory_space=pl.ANY),
        scratch_shapes=[pltpu.VMEM((2 * tm * ROW_SLAB, LANES), F32),
                        pltpu.VMEM((tm, d), BF16),
                        pltpu.VMEM((tm, d) if nj > 1 else (ROW_SLAB, LANES), F32),
                        pltpu.VMEM((2 * tm * ROW_SLAB, LANES), F32),
                        pltpu.SemaphoreType.DMA((2,)),
                        pltpu.SemaphoreType.DMA((2,))],
    )
    tok3 = row_tok.reshape(n_blocks, 1, tm)
    dst3 = row_dst.reshape(n_blocks + 1, 1, tm)
    return pl.pallas_call(
        functools.partial(_moe_kernel, tm=tm, nj=nj, dump_row0=dump_row0),
        grid_spec=grid_spec,
        out_shape=jax.ShapeDtypeStruct((y_rows * ROW_SLAB, LANES), F32),
        compiler_params=_params(("arbitrary", "arbitrary"), vmem=VMEM_LIMIT_BIG),
        name="moe_experts",
    )(block_e, n_valid, tok3, tok3, dst3, dst3, x_slabs, wg, wu, wd)


def _combine_kernel(ya_ref, yb_ref, x_ref, g1_ref, g2_ref, lw_ref, lb_ref, out_ref, *, tm):
    ffn = _from_slabs(ya_ref, tm) * g1_ref[...] + _from_slabs(yb_ref, tm) * g2_ref[...]
    out_ref[...] = _layer_norm(ALPHA * _from_slabs(x_ref, tm) + ffn, lw_ref[...], lb_ref[...])


def _combine(y, x_slabs, g1, g2, ln_w, ln_b, tm=512):
    t = g1.shape[0]
    d = ln_w.shape[1]
    row = lambda i: (i, 0)
    return pl.pallas_call(
        functools.partial(_combine_kernel, tm=tm),
        grid=(t // tm,),
        in_specs=[pl.BlockSpec((tm * ROW_SLAB, LANES), row),
                  pl.BlockSpec((tm * ROW_SLAB, LANES), lambda i: (t // tm + i, 0)),
                  pl.BlockSpec((tm * ROW_SLAB, LANES), row),
                  pl.BlockSpec((tm, 1), row),
                  pl.BlockSpec((tm, 1), row),
                  pl.BlockSpec((1, d), lambda i: (0, 0)),
                  pl.BlockSpec((1, d), lambda i: (0, 0))],
        out_specs=pl.BlockSpec((tm, d), row),
        out_shape=jax.ShapeDtypeStruct((t, d), F32),
        compiler_params=_params(("parallel",)),
        name="moe_combine",
    )(y, y, x_slabs, g1.reshape(t, 1), g2.reshape(t, 1), ln_w, ln_b)


def _moe_layer(x_slabs, e_sel, gates, ranks, counts, wg, wu, wd, ln_w, ln_b, tmoe=512):
    t = e_sel.shape[1]
    counts = counts[:, 0].astype(I32)
    padded = (counts + tmoe - 1) // tmoe * tmoe
    pend = jnp.cumsum(padded)
    pstart = pend - padded
    n_blocks = -(-(2 * t + N_EXPERTS * (tmoe - 1)) // tmoe)
    n_rows = n_blocks * tmoe
    first_row = sum(jnp.where(e_sel == e, pstart[e], 0) for e in range(N_EXPERTS))
    dest = first_row + ranks
    dump_row0 = 2 * t
    r = jnp.arange(n_rows, dtype=I32)
    dump = dump_row0 + ((r // tmoe) & 1) * tmoe + r % tmoe
    tok = jnp.arange(t, dtype=I32)
    row_dst = dump.at[dest.reshape(-1)].set(jnp.concatenate([tok, t + tok]))
    row_tok = jnp.where(row_dst < dump_row0, row_dst % t, 0)
    row_dst = jnp.concatenate([row_dst, dump_row0 + tmoe + jnp.arange(tmoe, dtype=I32)])
    block_row0 = jnp.arange(n_blocks, dtype=I32) * tmoe
    block_e = jnp.minimum(jnp.sum((block_row0[:, None] >= pend[None, :]).astype(I32), axis=1), N_EXPERTS - 1)
    n_valid = (pend[-1:] // tmoe).astype(I32)
    y = _moe_experts(x_slabs, row_tok, row_dst, dump_row0, block_e, n_valid, wg, wu, wd, tmoe)
    return _combine(y, x_slabs, gates[0], gates[1], ln_w, ln_b)


def kernel(x, positions, ln_w, ln_b, even_w_in, pool_w, pool_scale, swa_sinks, even_w_out, ffn_w_gate, ffn_w_up, ffn_w_down, odd_w_in, conv_w, odd_w_out, router_w, moe_w_gate, moe_w_up, moe_w_down):
    b, s, d = x.shape
    t = b * s
    x2d = x.reshape(t, d)
    cos_t, sin_t = _rope_tables(positions)
    ln = lambda layer, k: (ln_w[layer, k].reshape(1, d), ln_b[layer, k].reshape(1, d))

    u, q, k, v = _inproj0(x2d, even_w_in[0].astype(BF16), cos_t, sin_t)
    attn = _swa(q, k, v, swa_sinks[0], b, s)
    pool_bd = jax.scipy.linalg.block_diag(*[pool_w[0, gi] for gi in range(len(POOL_WINDOWS))]).astype(BF16)
    w_out0 = even_w_out[0].astype(BF16)
    x2d = _layer0_tail(u, attn, x2d, pool_bd, pool_scale[0].reshape(1, POOL_WIDTH),
                       w_out0[:POOL_WIDTH], w_out0[POOL_WIDTH:], ln(0, 0),
                       ffn_w_gate[0].astype(BF16), ffn_w_up[0].astype(BF16), ffn_w_down[0].astype(BF16),
                       ln(0, 1), b, s)

    w_in1 = odd_w_in[0].astype(BF16)
    c_in = len(DIL_PAIRS) * DIL_GROUP_IN
    wq, wkv = [], []
    for gi in range(len(DIL_PAIRS)):
        g0 = gi * DIL_GROUP_IN
        wq.append(w_in1[:, g0:g0 + DIL_Q_DIM])
        kcol = g0 + DIL_Q_DIM
        vcol = kcol + DIL_KV_DIM
        parts = []
        for h in range(DIL_KV_HEADS):
            parts += [w_in1[:, kcol + h * HEAD_DIM:kcol + (h + 1) * HEAD_DIM],
                      w_in1[:, vcol + h * HEAD_DIM:vcol + (h + 1) * HEAD_DIM]]
        wkv.append(jnp.concatenate(parts, axis=1))
    outs = _inproj1(x2d, jnp.stack(wq), jnp.stack(wkv), w_in1[:, c_in:], conv_w[0], cos_t, sin_t, b, s)
    c_out = _dilated(outs[0:3], outs[3:6], b, s)
    w_out1 = odd_w_out[0].astype(BF16)
    x_slabs, e_sel, gates, ranks, counts = _outproj1(
        c_out, outs[6].reshape(t, CONV_WIDTH), x2d, w_out1[:DIL_Q_DIM], w_out1[DIL_Q_DIM:], *ln(1, 0),
        router_w[0].T)
    x2d = _moe_layer(x_slabs, e_sel, gates, ranks, counts, moe_w_gate[0].astype(BF16),
                     moe_w_up[0].astype(BF16), moe_w_down[0].astype(BF16), *ln(1, 1))
    return x2d.reshape(b, s, d)
```

```python
import functools
import math

import jax
import jax.numpy as jnp
from jax import lax
from jax.experimental import pallas as pl
from jax.experimental.pallas import tpu as pltpu

F32 = jnp.float32
BF16 = jnp.bfloat16
I32 = jnp.int32

HEAD_DIM = 64
ROPE_THETA = 10000.0
ATT_BLOCK = 128
LN_EPS = 1e-5
POOL_WINDOWS = (2, 4, 8, 16)
POOL_GROUP = 64
POOL_WIDTH = 256
POOL_HALO = 16
SWA_WINDOW = 128
SWA_Q_HEADS = 12
SWA_KV_HEADS = 4
SWA_Q_DIM = SWA_Q_HEADS * HEAD_DIM
SWA_KV_DIM = SWA_KV_HEADS * HEAD_DIM
DIL_PAIRS = ((128, 1), (512, 4), (2048, 16))
DIL_Q_HEADS = 8
DIL_KV_HEADS = 2
DIL_Q_DIM = DIL_Q_HEADS * HEAD_DIM
DIL_KV_DIM = DIL_KV_HEADS * HEAD_DIM
DIL_GROUP_IN = DIL_Q_DIM + 2 * DIL_KV_DIM
DIL_Q_PER_KV = DIL_Q_HEADS // DIL_KV_HEADS
CONV_WIDTH = 512
CONV_K = 3
CONV_HALO = 8
N_EXPERTS = 8
DEPTH = 2
ALPHA = (2 * DEPTH) ** 0.25
QK_SCALE = 1.0 / math.sqrt(HEAD_DIM)
NEG_BIG = -1e30
KV_PARTS = 4

LANES = 128
ROW_SLAB = 8
VMEM_LIMIT = 48 * 1024 * 1024
VMEM_LIMIT_BIG = 56 * 1024 * 1024


def _params(sem, vmem=VMEM_LIMIT):
    return pltpu.CompilerParams(dimension_semantics=sem, vmem_limit_bytes=vmem)


def _layer_norm(y, w, b):
    mu = jnp.mean(y, axis=-1, keepdims=True)
    yc = y - mu
    var = jnp.mean(yc * yc, axis=-1, keepdims=True)
    return yc * lax.rsqrt(var + LN_EPS) * w + b


def _rope_chunk(xc, cos, sin_signed, first_half):
    rot = jnp.where(first_half, pltpu.roll(xc, 96, 1), pltpu.roll(xc, 32, 1))
    return xc * cos + rot * sin_signed


def _rope(x, cos, sin_signed):
    tm, c = x.shape
    lane = lax.broadcasted_iota(I32, (tm, LANES), 1)
    first_half = (lane & 32) == 0
    chunks = [_rope_chunk(x[:, i * LANES:(i + 1) * LANES], cos, sin_signed, first_half)
              for i in range(c // LANES)]
    return chunks[0] if len(chunks) == 1 else jnp.concatenate(chunks, axis=1)


def _trig_kernel(pos_ref, inv_ref, cos_ref, sin_ref):
    pos = pos_ref[...].astype(F32)
    ang = inv_ref[...] * pos
    c = jnp.cos(ang)
    s = jnp.sin(ang)
    c4 = jnp.concatenate([c, c, c, c], axis=0)
    s4 = jnp.concatenate([-s, s, -s, s], axis=0)
    cos_ref[...] = c4.T
    sin_ref[...] = s4.T


def _rope_tables(positions, tm=512):
    t = positions.size
    half = HEAD_DIM // 2
    inv = ROPE_THETA ** (-jnp.arange(half, dtype=F32) / half)
    return pl.pallas_call(
        _trig_kernel,
        grid=(t // tm,),
        in_specs=[pl.BlockSpec((1, tm), lambda i: (0, i)),
                  pl.BlockSpec((half, 1), lambda i: (0, 0))],
        out_specs=[pl.BlockSpec((tm, LANES), lambda i: (i, 0)),
                   pl.BlockSpec((tm, LANES), lambda i: (i, 0))],
        out_shape=[jax.ShapeDtypeStruct((t, LANES), F32)] * 2,
        compiler_params=_params(("parallel",)),
        name="rope_tables",
    )(positions.reshape(1, t), inv.reshape(half, 1))


def _inproj0_kernel(x_ref, w_ref, cos_ref, sin_ref, u_ref, q_ref, k_ref, v_ref):
    xb = x_ref[...].astype(BF16)
    cos = cos_ref[...]
    sin = sin_ref[...]
    q0 = POOL_WIDTH
    k0 = q0 + SWA_Q_DIM
    v0 = k0 + SWA_KV_DIM
    u_ref[...] = jnp.dot(xb, w_ref[:, :q0], preferred_element_type=F32)
    q = jnp.dot(xb, w_ref[:, q0:k0], preferred_element_type=F32)
    q_ref[...] = (_rope(q, cos, sin) * QK_SCALE).astype(BF16)
    k = jnp.dot(xb, w_ref[:, k0:v0], preferred_element_type=F32)
    k_ref[...] = _rope(k, cos, sin).astype(BF16)
    v_ref[...] = jnp.dot(xb, w_ref[:, v0:], preferred_element_type=F32).astype(BF16)


def _inproj0(x2d, w_bf, cos_t, sin_t, tm=512):
    t, d = x2d.shape
    n_in = w_bf.shape[1]
    row = lambda i: (i, 0)
    return pl.pallas_call(
        _inproj0_kernel,
        grid=(t // tm,),
        in_specs=[pl.BlockSpec((tm, d), row),
                  pl.BlockSpec((d, n_in), lambda i: (0, 0)),
                  pl.BlockSpec((tm, LANES), row),
                  pl.BlockSpec((tm, LANES), row)],
        out_specs=[pl.BlockSpec((tm, POOL_WIDTH), row),
                   pl.BlockSpec((tm, SWA_Q_DIM), row),
                   pl.BlockSpec((tm, SWA_KV_DIM), row),
                   pl.BlockSpec((tm, SWA_KV_DIM), row)],
        out_shape=[jax.ShapeDtypeStruct((t, POOL_WIDTH), F32),
                   jax.ShapeDtypeStruct((t, SWA_Q_DIM), BF16),
                   jax.ShapeDtypeStruct((t, SWA_KV_DIM), BF16),
                   jax.ShapeDtypeStruct((t, SWA_KV_DIM), BF16)],
        compiler_params=_params(("parallel",)),
        name="inproj0",
    )(x2d, w_bf, cos_t, sin_t)


def _band_bias(rows, max_dist, key_lo):
    qi = lax.broadcasted_iota(I32, (rows, 2 * ATT_BLOCK), 0) & (ATT_BLOCK - 1)
    sj = lax.broadcasted_iota(I32, (rows, 2 * ATT_BLOCK), 1)
    dist = ATT_BLOCK + qi - sj
    return jnp.where((dist >= 0) & (dist <= max_dist) & (sj >= key_lo), 0.0, NEG_BIG)


def _band_mask(max_dist, key_lo):
    qi = lax.broadcasted_iota(I32, (ATT_BLOCK, 2 * ATT_BLOCK), 0)
    sj = lax.broadcasted_iota(I32, (ATT_BLOCK, 2 * ATT_BLOCK), 1)
    dist = ATT_BLOCK + qi - sj
    return (dist >= 0) & (dist <= max_dist) & (sj >= key_lo)


def _attn_block(q, kwin, vwin, valid, sink):
    s = lax.dot_general(q, kwin, (((1,), (1,)), ((), ())), preferred_element_type=F32)
    s = jnp.where(valid, s, NEG_BIG)
    m = jnp.maximum(jnp.max(s, axis=1, keepdims=True), sink)
    p = jnp.exp(s - m)
    den = jnp.sum(p, axis=1, keepdims=True) + jnp.exp(sink - m)
    o = jnp.dot(p.astype(BF16), vwin, preferred_element_type=F32)
    return o * (1.0 / den)


def _attn_stack(qs, kpart, vpart, bias):
    s = lax.dot_general(qs, kpart, (((1,), (1,)), ((), ())), preferred_element_type=F32) + bias
    m = jnp.max(s, axis=1, keepdims=True)
    p = jnp.exp(s - m)
    den = jnp.sum(p, axis=1, keepdims=True)
    return jnp.dot(p.astype(BF16), vpart, preferred_element_type=F32), m, den


def _kv_parts(kvx, h=0):
    base = h * KV_PARTS * LANES
    return [kvx[:, base + part * LANES:base + (part + 1) * LANES] for part in range(KV_PARTS)]


def _swa_kernel(sink_ref, q_ref, kp_ref, kc_ref, vp_ref, vc_ref, *rest, tq, n_cast):
    o_ref = rest[n_cast]
    for src, dst in zip(rest[:n_cast], rest[n_cast + 1:]):
        dst[...] = src[...].astype(BF16)
    i = pl.program_id(1)
    kfull = jnp.concatenate([kp_ref[...], kc_ref[...]], axis=0)
    vfull = jnp.concatenate([vp_ref[...], vc_ref[...]], axis=0)
    g = SWA_Q_HEADS // SWA_KV_HEADS
    for j in range(tq // ATT_BLOCK):
        key_lo = jnp.where(i == 0, ATT_BLOCK, 0) if j == 0 else 0
        valid = _band_mask(SWA_WINDOW - 1, key_lo)
        r0 = j * ATT_BLOCK
        outs = []
        for h in range(SWA_Q_HEADS):
            kv = h // g
            q = q_ref[r0:r0 + ATT_BLOCK, h * HEAD_DIM:(h + 1) * HEAD_DIM]
            kwin = kfull[r0:r0 + 2 * ATT_BLOCK, kv * HEAD_DIM:(kv + 1) * HEAD_DIM]
            vwin = vfull[r0:r0 + 2 * ATT_BLOCK, kv * HEAD_DIM:(kv + 1) * HEAD_DIM]
            outs.append(_attn_block(q, kwin, vwin, valid, sink_ref[h]))
        o_ref[r0:r0 + ATT_BLOCK, :] = jnp.concatenate(outs, axis=1).astype(BF16)


def _swa(q, k, v, sinks, b, s, cast=(), tq=256):
    per = tq // ATT_BLOCK
    steps = s // tq
    cur = lambda bi, i: (bi, i, 0)
    prev = lambda bi, i: (bi, jnp.maximum(i * per - 1, 0), 0)
    q3 = q.reshape(b, s, SWA_Q_DIM)
    k3 = k.reshape(b, s, SWA_KV_DIM)
    v3 = v.reshape(b, s, SWA_KV_DIM)
    cast_specs = []
    for w in cast:
        rows = w.shape[0] // (b * steps)
        assert rows * b * steps == w.shape[0] and rows % 16 == 0, w.shape
        cast_specs.append(pl.BlockSpec((rows, w.shape[1]), lambda bi, i: (bi * steps + i, 0)))
    outs = pl.pallas_call(
        functools.partial(_swa_kernel, tq=tq, n_cast=len(cast)),
        grid=(b, steps),
        in_specs=[pl.BlockSpec(memory_space=pltpu.SMEM),
                  pl.BlockSpec((None, tq, SWA_Q_DIM), cur),
                  pl.BlockSpec((None, ATT_BLOCK, SWA_KV_DIM), prev),
                  pl.BlockSpec((None, tq, SWA_KV_DIM), cur),
                  pl.BlockSpec((None, ATT_BLOCK, SWA_KV_DIM), prev),
                  pl.BlockSpec((None, tq, SWA_KV_DIM), cur)] + cast_specs,
        out_specs=[pl.BlockSpec((None, tq, SWA_Q_DIM), cur)] + cast_specs,
        out_shape=[jax.ShapeDtypeStruct((b, s, SWA_Q_DIM), BF16)]
                  + [jax.ShapeDtypeStruct(w.shape, BF16) for w in cast],
        compiler_params=_params(("parallel", "parallel")),
        name="swa_attention",
    )(sinks, q3, k3, k3, v3, v3, *cast)
    return outs[0].reshape(b * s, SWA_Q_DIM), outs[1:]


def _pool_mixer(u, halo, seq_row0):
    tm = u.shape[0]
    full = jnp.concatenate([halo, u], axis=0)
    sums = [full]
    for shift in (1, 2, 4, 8):
        prev = sums[-1]
        sums.append(prev + pltpu.roll(prev, shift, 0))
    lane = lax.broadcasted_iota(I32, (tm, POOL_WIDTH), 1)
    row = lax.broadcasted_iota(I32, (tm, POOL_WIDTH), 0)
    grp = lane // POOL_GROUP
    win = sums[4][POOL_HALO:]
    width = jnp.full((tm, POOL_WIDTH), POOL_WINDOWS[3], I32)
    for gi in (2, 1, 0):
        win = jnp.where(grp == gi, sums[gi + 1][POOL_HALO:], win)
        width = jnp.where(grp == gi, POOL_WINDOWS[gi], width)
    count = jnp.minimum(seq_row0 + row + 1, width).astype(F32)
    return win / count - u


def _swiglu_tile(xb, wg, wu, wd):
    g = jnp.dot(xb, wg, preferred_element_type=F32)
    u = jnp.dot(xb, wu, preferred_element_type=F32)
    h = (g * jax.nn.sigmoid(g)) * u
    return jnp.dot(h.astype(BF16), wd, preferred_element_type=F32)


def _layer0_tail_kernel(u_ref, uh_ref, o_ref, x_ref, pw_ref, ps_ref, wa_ref, wb_ref, lw0_ref, lb0_ref,
                        wg_ref, wu_ref, wd_ref, lw1_ref, lb1_ref, out_ref, *, tm):
    i = pl.program_id(1)
    halo = jnp.where(i == 0, 0.0, uh_ref[...])
    d = _pool_mixer(u_ref[...], halo, i * tm)
    a = jnp.dot(d.astype(BF16), pw_ref[...], preferred_element_type=F32) * ps_ref[...]
    mix = jnp.dot(a.astype(BF16), wa_ref[...], preferred_element_type=F32)
    mix = mix + jnp.dot(o_ref[...], wb_ref[...], preferred_element_type=F32)
    x1 = _layer_norm(ALPHA * x_ref[...] + mix, lw0_ref[...], lb0_ref[...])
    ffn = _swiglu_tile(x1.astype(BF16), wg_ref[...], wu_ref[...], wd_ref[...])
    out_ref[...] = _layer_norm(ALPHA * x1 + ffn, lw1_ref[...], lb1_ref[...])


def _layer0_tail(u, attn, x2d, pool_bd, pool_scale, wa, wb, ln0, wg, wu, wd, ln1, b, s, tm=512):
    d = x2d.shape[1]
    f = wg.shape[1]
    per = tm // POOL_HALO
    cur = lambda bi, i: (bi, i, 0)
    prev = lambda bi, i: (bi, jnp.maximum(i * per - 1, 0), 0)
    const = lambda bi, i: (0, 0)
    out = pl.pallas_call(
        functools.partial(_layer0_tail_kernel, tm=tm),
        grid=(b, s // tm),
        in_specs=[pl.BlockSpec((None, tm, POOL_WIDTH), cur),
                  pl.BlockSpec((None, POOL_HALO, POOL_WIDTH), prev),
                  pl.BlockSpec((None, tm, SWA_Q_DIM), cur),
                  pl.BlockSpec((None, tm, d), cur),
                  pl.BlockSpec((POOL_WIDTH, POOL_WIDTH), const),
                  pl.BlockSpec((1, POOL_WIDTH), const),
                  pl.BlockSpec((POOL_WIDTH, d), const),
                  pl.BlockSpec((SWA_Q_DIM, d), const),
                  pl.BlockSpec((1, d), const),
                  pl.BlockSpec((1, d), const),
                  pl.BlockSpec((d, f), const),
                  pl.BlockSpec((d, f), const),
                  pl.BlockSpec((f, d), const),
                  pl.BlockSpec((1, d), const),
                  pl.BlockSpec((1, d), const)],
        out_specs=pl.BlockSpec((None, tm, d), cur),
        out_shape=jax.ShapeDtypeStruct((b, s, d), F32),
        compiler_params=_params(("parallel", "parallel"), vmem=VMEM_LIMIT_BIG),
        name="layer0_tail",
    )(u.reshape(b, s, POOL_WIDTH), u.reshape(b, s, POOL_WIDTH), attn.reshape(b, s, SWA_Q_DIM),
      x2d.reshape(b, s, d), pool_bd, pool_scale, wa, wb, *ln0, wg, wu, wd, *ln1)
    return out.reshape(b * s, d)


def _kv_operands(kv, low):
    swapped = pltpu.roll(kv, HEAD_DIM, 1)
    return (jnp.where(low, kv, 0.0), jnp.where(low, 0.0, swapped),
            jnp.where(low, swapped, 0.0), jnp.where(low, 0.0, kv))


def _inproj1_kernel(x_ref, wq_ref, wkv_ref, wc_ref, cw_ref, cos_ref, sin_ref,
                    q0_ref, q1_ref, q2_ref, kv0_ref, kv1_ref, kv2_ref, d_ref,
                    zc_ref, sq_ref, skv_ref, *, tm):
    i = pl.program_id(1)
    xb = x_ref[...].astype(BF16)
    cos = cos_ref[...]
    sin = sin_ref[...]
    lane = lax.broadcasted_iota(I32, (tm, LANES), 1)
    low = lane < HEAD_DIM
    cos_k = jnp.where(low, cos, 1.0)
    sin_k = jnp.where(low, sin, 0.0)
    q_refs = (q0_ref, q1_ref, q2_ref)
    kv_refs = (kv0_ref, kv1_ref, kv2_ref)
    for gi, (_, dil) in enumerate(DIL_PAIRS):
        q = jnp.dot(xb, wq_ref[gi], preferred_element_type=F32)
        q = _rope(q, cos, sin) * QK_SCALE
        kv = jnp.dot(xb, wkv_ref[gi], preferred_element_type=F32)
        kv = _rope(kv, cos_k, sin_k)
        kvx = [_kv_operands(kv[:, h * LANES:(h + 1) * LANES], low) for h in range(DIL_KV_HEADS)]
        if dil == 1:
            q_refs[gi][0] = q.astype(BF16)
            for h in range(DIL_KV_HEADS):
                kv_refs[gi][h, 0] = jnp.concatenate(kvx[h], axis=1).astype(BF16)
        else:
            n = tm // dil
            for c in range(DIL_Q_DIM // LANES):
                sq_ref[c] = q[:, c * LANES:(c + 1) * LANES]
            for h in range(DIL_KV_HEADS):
                for part in range(KV_PARTS):
                    skv_ref[h * KV_PARTS + part] = kvx[h][part]
            for r in range(dil):
                rows = pl.ds(r, n, stride=dil)
                q_refs[gi][r] = jnp.concatenate(
                    [sq_ref[c, rows, :] for c in range(DIL_Q_DIM // LANES)], axis=1).astype(BF16)
                for h in range(DIL_KV_HEADS):
                    kv_refs[gi][h, r] = jnp.concatenate(
                        [skv_ref[h * KV_PARTS + part, rows, :] for part in range(KV_PARTS)], axis=1).astype(BF16)

    hc = jnp.dot(xb, wc_ref[...], preferred_element_type=F32)
    z = hc[:, 2 * CONV_WIDTH:] * hc[:, :CONV_WIDTH]
    zprev = jnp.where(i == 0, 0.0, zc_ref[...])
    zfull = jnp.concatenate([zprev, z], axis=0)
    z1 = pltpu.roll(zfull, 1, 0)[CONV_HALO:]
    z2 = pltpu.roll(zfull, 2, 0)[CONV_HALO:]
    cw = cw_ref[...]
    y = cw[0:1] * z2 + cw[1:2] * z1 + cw[2:3] * z
    d_ref[...] = (hc[:, CONV_WIDTH:2 * CONV_WIDTH] * y).astype(BF16)
    zc_ref[...] = z[tm - CONV_HALO:]


def _inproj1(x2d, wq, wkv, wc, conv_w, cos_t, sin_t, b, s, tm=512):
    d = x2d.shape[1]
    cur = lambda bi, i: (bi, i, 0)
    row = lambda bi, i: (bi * (s // tm) + i, 0)
    in_specs = [pl.BlockSpec((None, tm, d), cur),
                pl.BlockSpec(wq.shape, lambda bi, i: (0, 0, 0)),
                pl.BlockSpec(wkv.shape, lambda bi, i: (0, 0, 0)),
                pl.BlockSpec(wc.shape, lambda bi, i: (0, 0)),
                pl.BlockSpec(conv_w.shape, lambda bi, i: (0, 0)),
                pl.BlockSpec((tm, LANES), row),
                pl.BlockSpec((tm, LANES), row)]
    out_specs, out_shape = [], []
    for _, dil in DIL_PAIRS:
        out_specs.append(pl.BlockSpec((None, dil, tm // dil, DIL_Q_DIM), lambda bi, i: (bi, 0, i, 0)))
        out_shape.append(jax.ShapeDtypeStruct((b, dil, s // dil, DIL_Q_DIM), BF16))
    for _, dil in DIL_PAIRS:
        out_specs.append(pl.BlockSpec((None, DIL_KV_HEADS, dil, tm // dil, KV_PARTS * LANES),
                                      lambda bi, i: (bi, 0, 0, i, 0)))
        out_shape.append(jax.ShapeDtypeStruct((b, DIL_KV_HEADS, dil, s // dil, KV_PARTS * LANES), BF16))
    out_specs.append(pl.BlockSpec((None, tm, CONV_WIDTH), cur))
    out_shape.append(jax.ShapeDtypeStruct((b, s, CONV_WIDTH), BF16))
    return pl.pallas_call(
        functools.partial(_inproj1_kernel, tm=tm),
        grid=(b, s // tm),
        in_specs=in_specs,
        out_specs=out_specs,
        out_shape=out_shape,
        scratch_shapes=[pltpu.VMEM((CONV_HALO, CONV_WIDTH), F32),
                        pltpu.VMEM((DIL_Q_DIM // LANES, tm, LANES), F32),
                        pltpu.VMEM((DIL_KV_HEADS * KV_PARTS, tm, LANES), F32)],
        compiler_params=_params(("parallel", "arbitrary")),
        name="inproj1",
    )(x2d.reshape(b, s, d), wq, wkv, wc, conv_w, cos_t, sin_t)


def _dil_kernel(q0_ref, q1_ref, q2_ref, kv0_ref, kv1_ref, kv2_ref, out_ref, o_run, l_run, bias_ref, *, s):
    q_refs = (q0_ref, q1_ref, q2_ref)
    kv_refs = (kv0_ref, kv1_ref, kv2_ref)
    n_blocks = s // ATT_BLOCK
    width = DIL_Q_PER_KV * HEAD_DIM
    n_chunks = width // LANES
    max_dist = DIL_PAIRS[0][0] // DIL_PAIRS[0][1]
    assert all(w // d == max_dist for w, d in DIL_PAIRS)
    stacked = n_chunks * ATT_BLOCK
    bias_ref[0] = _band_bias(stacked, max_dist, 0)
    bias_ref[1] = _band_bias(stacked, max_dist, ATT_BLOCK)
    for gi, (_, dil) in enumerate(DIL_PAIRS):
        q_ref, kv_ref = q_refs[gi], kv_refs[gi]
        blocks_per_sub = (s // dil) // ATT_BLOCK
        shift = blocks_per_sub.bit_length() - 1

        def body(n, carry, q_ref=q_ref, kv_ref=kv_ref, gi=gi, dil=dil, blocks_per_sub=blocks_per_sub, shift=shift):
            res = lax.shift_right_logical(n, shift)
            bi = n & (blocks_per_sub - 1)
            rows = pl.ds(pl.multiple_of(n * ATT_BLOCK, ATT_BLOCK), ATT_BLOCK)
            prow = pl.ds(pl.multiple_of(jnp.maximum(n - 1, 0) * ATT_BLOCK, ATT_BLOCK), ATT_BLOCK)
            kvx = jnp.concatenate([kv_ref[prow, :], kv_ref[rows, :]], axis=0)
            kz, zk, vz, zv = _kv_parts(kvx)
            q4 = q_ref[rows, :]
            qs = jnp.concatenate([q4[:, c * LANES:(c + 1) * LANES] for c in range(n_chunks)], axis=0)
            bias = bias_ref[jnp.where(bi == 0, 1, 0)]
            oa, ma, da = _attn_stack(qs, kz, vz, bias)
            ob, mb, db = _attn_stack(qs, zk, zv, bias)
            low = lax.broadcasted_iota(I32, (n_chunks * ATT_BLOCK, LANES), 1) < HEAD_DIM
            o_st = (oa + ob) * jnp.where(low, 1.0 / da, 1.0 / db)
            l_st = jnp.where(low, ma + jnp.log(da), mb + jnp.log(db))
            start = res + dil * ATT_BLOCK * bi
            if dil == 1:
                tok = pl.ds(pl.multiple_of(start, ATT_BLOCK), ATT_BLOCK)
            else:
                tok = pl.ds(start, ATT_BLOCK, stride=dil)
            for c in range(n_chunks):
                o_c = o_st[c * ATT_BLOCK:(c + 1) * ATT_BLOCK]
                l_c = l_st[c * ATT_BLOCK:(c + 1) * ATT_BLOCK]
                if gi == 0:
                    o_run[c, tok, :] = o_c
                    l_run[c, tok, :] = l_c
                else:
                    o_old = o_run[c, tok, :]
                    l_old = l_run[c, tok, :]
                    m = jnp.maximum(l_old, l_c)
                    a = jnp.exp(l_old - m)
                    bb = jnp.exp(l_c - m)
                    tot = a + bb
                    o_run[c, tok, :] = (o_old * a + o_c * bb) * (1.0 / tot)
                    if gi < len(DIL_PAIRS) - 1:
                        l_run[c, tok, :] = m + jnp.log(tot)
            return carry

        lax.fori_loop(0, n_blocks, body, 0, unroll=4)
    for c in range(n_chunks):
        out_ref[:, c * LANES:(c + 1) * LANES] = o_run[c].astype(BF16)


def _dilated(qs, kvs, b, s):
    width = DIL_Q_PER_KV * HEAD_DIM
    in_specs = [pl.BlockSpec((None, s, width), lambda bi, h: (bi, 0, h)) for _ in DIL_PAIRS]
    in_specs += [pl.BlockSpec((None, None, s, KV_PARTS * LANES), lambda bi, h: (bi, h, 0, 0)) for _ in DIL_PAIRS]
    qs = [q.reshape(b, s, DIL_Q_DIM) for q in qs]
    kvs = [kv.reshape(b, DIL_KV_HEADS, s, KV_PARTS * LANES) for kv in kvs]
    out = pl.pallas_call(
        functools.partial(_dil_kernel, s=s),
        grid=(b, DIL_KV_HEADS),
        in_specs=in_specs,
        out_specs=pl.BlockSpec((None, s, width), lambda bi, h: (bi, 0, h)),
        out_shape=jax.ShapeDtypeStruct((b, s, DIL_Q_DIM), BF16),
        scratch_shapes=[pltpu.VMEM((width // LANES, s, LANES), F32),
                        pltpu.VMEM((width // LANES, s, LANES), F32),
                        pltpu.VMEM((2, (width // LANES) * ATT_BLOCK, 2 * ATT_BLOCK), F32)],
        compiler_params=_params(("parallel", "parallel"), vmem=VMEM_LIMIT_BIG),
        name="dilated_attention",
    )(*qs, *kvs)
    return out.reshape(b * s, DIL_Q_DIM)


def _to_slabs(ref, val, base=0):
    rows, d = val.shape
    per = d // LANES
    for c in range(per):
        ref[pl.ds(base + c, rows, stride=per), :] = val[:, c * LANES:(c + 1) * LANES]


def _from_slabs(ref, rows, base=0, per=ROW_SLAB):
    return jnp.concatenate([ref[pl.ds(base + c, rows, stride=per), :] for c in range(per)], axis=1)


def _route_tile(x, wr, e_ref, g_ref, r_ref, cnt_ref, run_ref, tm):
    i = pl.program_id(0)

    @pl.when(i == 0)
    def _():
        run_ref[...] = jnp.zeros_like(run_ref)

    logits = lax.dot_general(wr, x, (((1,), (1,)), ((), ())),
                             precision=lax.Precision.HIGHEST, preferred_element_type=F32)
    eid = lax.broadcasted_iota(I32, (N_EXPERTS, tm), 0)
    m1 = jnp.max(logits, axis=0, keepdims=True)
    i1 = jnp.min(jnp.where(logits == m1, eid, N_EXPERTS), axis=0, keepdims=True)
    rest = jnp.where(eid == i1, -jnp.inf, logits)
    m2 = jnp.max(rest, axis=0, keepdims=True)
    i2 = jnp.min(jnp.where(rest == m2, eid, N_EXPERTS), axis=0, keepdims=True)
    t2 = jnp.exp(m2 - m1)
    g1 = 1.0 / (1.0 + t2)
    g2 = t2 / (1.0 + t2)
    oh1 = eid == i1
    oh2 = eid == i2
    oh = jnp.where(oh1 | oh2, 1.0, 0.0)
    ri = lax.broadcasted_iota(I32, (tm, tm), 0)
    ci = lax.broadcasted_iota(I32, (tm, tm), 1)
    tri = jnp.where(ri <= ci, 1.0, 0.0).astype(BF16)
    incl = jnp.dot(oh.astype(BF16), tri, preferred_element_type=F32)
    rank = run_ref[:, 0:1] + incl - oh
    r1 = jnp.sum(jnp.where(oh1, rank, 0.0), axis=0, keepdims=True)
    r2 = jnp.sum(jnp.where(oh2, rank, 0.0), axis=0, keepdims=True)
    e_ref[0:1, :] = i1
    e_ref[1:2, :] = i2
    g_ref[0:1, :] = g1
    g_ref[1:2, :] = g2
    r_ref[0:1, :] = r1.astype(I32)
    r_ref[1:2, :] = r2.astype(I32)
    run_ref[...] = run_ref[...] + incl[:, tm - 1:tm]
    cnt_ref[...] = run_ref[...]


def _outproj1_kernel(c_ref, d_ref, x_ref, wa_ref, wb_ref, lw_ref, lb_ref, wr_ref,
                     slab_ref, e_ref, g_ref, r_ref, cnt_ref, run_ref, *, tm):
    mix = jnp.dot(c_ref[...], wa_ref[...], preferred_element_type=F32)
    mix = mix + jnp.dot(d_ref[...], wb_ref[...], preferred_element_type=F32)
    y = _layer_norm(ALPHA * x_ref[...] + mix, lw_ref[...], lb_ref[...])
    _to_slabs(slab_ref, y)
    _route_tile(y, wr_ref[...], e_ref, g_ref, r_ref, cnt_ref, run_ref, tm)


def _outproj1(c, dconv, x2d, wa, wb, ln_w, ln_b, wr_t, tm=512):
    t, d = x2d.shape
    row = lambda i: (i, 0)
    col = lambda i: (0, i)
    const = lambda i: (0, 0)
    return pl.pallas_call(
        functools.partial(_outproj1_kernel, tm=tm),
        grid=(t // tm,),
        in_specs=[pl.BlockSpec((tm, DIL_Q_DIM), row),
                  pl.BlockSpec((tm, CONV_WIDTH), row),
                  pl.BlockSpec((tm, d), row),
                  pl.BlockSpec((DIL_Q_DIM, d), const),
                  pl.BlockSpec((CONV_WIDTH, d), const),
                  pl.BlockSpec((1, d), const),
                  pl.BlockSpec((1, d), const),
                  pl.BlockSpec((N_EXPERTS, d), const)],
        out_specs=[pl.BlockSpec((tm * (d // LANES), LANES), row),
                   pl.BlockSpec((2, tm), col), pl.BlockSpec((2, tm), col), pl.BlockSpec((2, tm), col),
                   pl.BlockSpec((N_EXPERTS, LANES), const)],
        out_shape=[jax.ShapeDtypeStruct((t * (d // LANES), LANES), F32),
                   jax.ShapeDtypeStruct((2, t), I32), jax.ShapeDtypeStruct((2, t), F32),
                   jax.ShapeDtypeStruct((2, t), I32), jax.ShapeDtypeStruct((N_EXPERTS, LANES), F32)],
        scratch_shapes=[pltpu.VMEM((N_EXPERTS, LANES), F32)],
        compiler_params=_params(("arbitrary",)),
        name="outproj1_route",
    )(c, dconv, x2d, wa, wb, ln_w, ln_b, wr_t)


def _slab_copy(src_ref, dst_ref, sem, src_row, dst_row, n=1):
    src = src_ref.at[pl.ds(pl.multiple_of(src_row * ROW_SLAB, ROW_SLAB), n * ROW_SLAB)]
    dst = dst_ref.at[pl.ds(pl.multiple_of(dst_row * ROW_SLAB, ROW_SLAB), n * ROW_SLAB)]
    return pltpu.make_async_copy(src, dst, sem)


def _moe_kernel(be_ref, nv_ref, tok_ref, tokn_ref, dstp_ref, dstc_ref, x_hbm, wg_ref, wu_ref, wd_ref, y_hbm,
                xs_ref, xb_ref, acc_ref, stage_ref, sem_in, sem_out, *, tm, nj, dump_row0):
    i = pl.program_id(0)
    j = pl.program_id(1)
    n_valid = nv_ref[0]
    slot = i & 1
    per_step = tm // nj

    def row_in(tok, dst_slot, r):
        return _slab_copy(x_hbm, xs_ref, sem_in.at[dst_slot], tok, dst_slot * tm + r)

    def row_out(src_slot, r, dst):
        return _slab_copy(stage_ref, y_hbm, sem_out.at[src_slot], src_slot * tm + r, dst)

    @pl.when(i < n_valid)
    def _():
        @pl.when(j == 0)
        def _():
            @pl.when(i == 0)
            def _():
                def start(r, c):
                    row_in(tok_ref[0, 0, r], 0, r).start()
                    return c
                lax.fori_loop(0, tm, start, 0)
                stage_ref[...] = jnp.zeros_like(stage_ref)
                fill = _slab_copy(stage_ref, y_hbm, sem_out.at[0], 0, dump_row0, n=2 * tm)
                fill.start()
                fill.wait()

            _slab_copy(x_hbm, xs_ref, sem_in.at[slot], 0, slot * tm, n=tm).wait()
            xb_ref[...] = _from_slabs(xs_ref, tm, base=slot * (tm * ROW_SLAB)).astype(BF16)

        for r in range(per_step):
            rr = j * per_step + r
            row_in(tokn_ref[0, 0, rr], 1 - slot, rr).start(priority=1)
            row_out(1 - slot, rr, dstp_ref[0, 0, rr]).start(priority=1)

        part = _swiglu_tile(xb_ref[...], wg_ref[...], wu_ref[...], wd_ref[...])
        if nj > 1:
            @pl.when(j == 0)
            def _():
                acc_ref[...] = part

            @pl.when(j > 0)
            def _():
                acc_ref[...] += part

        @pl.when(j == nj - 1)
        def _():
            _slab_copy(stage_ref, y_hbm, sem_out.at[1 - slot], (1 - slot) * tm, 0, n=tm).wait()
            _to_slabs(stage_ref, acc_ref[...] if nj > 1 else part, base=slot * (tm * ROW_SLAB))

            @pl.when(i + 1 >= n_valid)
            def _():
                _slab_copy(x_hbm, xs_ref, sem_in.at[1 - slot], 0, (1 - slot) * tm, n=tm).wait()

                def start(r, c):
                    row_out(slot, r, dstc_ref[0, 0, r]).start()
                    return c
                lax.fori_loop(0, tm, start, 0)
                _slab_copy(stage_ref, y_hbm, sem_out.at[slot], slot * tm, 0, n=tm).wait()


def _moe_experts(x_slabs, row_tok, row_dst, dump_row0, block_e, n_valid, wg, wu, wd, tm, tf=None):
    d = wg.shape[1]
    n_rows = row_tok.shape[0]
    n_blocks = n_rows // tm
    f = wg.shape[2]
    tf = f if tf is None else tf
    nj = f // tf
    y_rows = dump_row0 + 2 * tm
    w_mode = dict(pipeline_mode=pl.Buffered(1)) if nj == 1 else {}

    def live(i, j, be, nv):
        ok = i < nv[0]
        return jnp.where(ok, i, nv[0] - 1), jnp.where(ok, j, nj - 1)

    def w_map(i, j, be, nv):
        ii, jj = live(i, j, be, nv)
        return be[ii], 0, jj

    def wd_map(i, j, be, nv):
        ii, jj = live(i, j, be, nv)
        return be[ii], jj, 0

    def smem(index_map):
        return pl.BlockSpec((1, 1, tm), index_map, memory_space=pltpu.SMEM)

    grid_spec = pltpu.PrefetchScalarGridSpec(
        num_scalar_prefetch=2,
        grid=(n_blocks, nj),
        in_specs=[smem(lambda i, j, be, nv: (i, 0, 0)),
                  smem(lambda i, j, be, nv: (jnp.minimum(i + 1, n_blocks - 1), 0, 0)),
                  smem(lambda i, j, be, nv: (jnp.where(i == 0, n_blocks, i - 1), 0, 0)),
                  smem(lambda i, j, be, nv: (i, 0, 0)),
                  pl.BlockSpec(memory_space=pl.ANY),
                  pl.BlockSpec((None, d, tf), w_map, **w_mode),
                  pl.BlockSpec((None, d, tf), w_map, **w_mode),
                  pl.BlockSpec((None, tf, d), wd_map, **w_mode)],
        out_specs=pl.BlockSpec(memory_space=pl.ANY),
        scratch_shapes=[pltpu.VMEM((2 * tm * ROW_SLAB, LANES), F32),
                        pltpu.VMEM((tm, d), BF16),
                        pltpu.VMEM((tm, d) if nj > 1 else (ROW_SLAB, LANES), F32),
                        pltpu.VMEM((2 * tm * ROW_SLAB, LANES), F32),
                        pltpu.SemaphoreType.DMA((2,)),
                        pltpu.SemaphoreType.DMA((2,))],
    )
    tok3 = row_tok.reshape(n_blocks, 1, tm)
    dst3 = row_dst.reshape(n_blocks + 1, 1, tm)
    return pl.pallas_call(
        functools.partial(_moe_kernel, tm=tm, nj=nj, dump_row0=dump_row0),
        grid_spec=grid_spec,
        out_shape=jax.ShapeDtypeStruct((y_rows * ROW_SLAB, LANES), F32),
        compiler_params=_params(("arbitrary", "arbitrary"), vmem=VMEM_LIMIT_BIG),
        name="moe_experts",
    )(block_e, n_valid, tok3, tok3, dst3, dst3, x_slabs, wg, wu, wd)


def _combine_kernel(ya_ref, yb_ref, x_ref, g1_ref, g2_ref, lw_ref, lb_ref, out_ref, *, tm):
    ffn = _from_slabs(ya_ref, tm) * g1_ref[...] + _from_slabs(yb_ref, tm) * g2_ref[...]
    out_ref[...] = _layer_norm(ALPHA * _from_slabs(x_ref, tm) + ffn, lw_ref[...], lb_ref[...])


def _combine(y, x_slabs, g1, g2, ln_w, ln_b, tm=512):
    t = g1.shape[0]
    d = ln_w.shape[1]
    row = lambda i: (i, 0)
    return pl.pallas_call(
        functools.partial(_combine_kernel, tm=tm),
        grid=(t // tm,),
        in_specs=[pl.BlockSpec((tm * ROW_SLAB, LANES), row),
                  pl.BlockSpec((tm * ROW_SLAB, LANES), lambda i: (t // tm + i, 0)),
                  pl.BlockSpec((tm * ROW_SLAB, LANES), row),
                  pl.BlockSpec((tm, 1), row),
                  pl.BlockSpec((tm, 1), row),
                  pl.BlockSpec((1, d), lambda i: (0, 0)),
                  pl.BlockSpec((1, d), lambda i: (0, 0))],
        out_specs=pl.BlockSpec((tm, d), row),
        out_shape=jax.ShapeDtypeStruct((t, d), F32),
        compiler_params=_params(("parallel",)),
        name="moe_combine",
    )(y, y, x_slabs, g1.reshape(t, 1), g2.reshape(t, 1), ln_w, ln_b)


def _moe_layer(x_slabs, e_sel, gates, ranks, counts, wg, wu, wd, ln_w, ln_b, tmoe=512):
    t = e_sel.shape[1]
    counts = counts[:, 0].astype(I32)
    padded = (counts + tmoe - 1) // tmoe * tmoe
    pend = jnp.cumsum(padded)
    pstart = pend - padded
    n_blocks = -(-(2 * t + N_EXPERTS * (tmoe - 1)) // tmoe)
    n_rows = n_blocks * tmoe
    first_row = sum(jnp.where(e_sel == e, pstart[e], 0) for e in range(N_EXPERTS))
    dest = first_row + ranks
    dump_row0 = 2 * t
    r = jnp.arange(n_rows, dtype=I32)
    dump = dump_row0 + ((r // tmoe) & 1) * tmoe + r % tmoe
    tok = jnp.arange(t, dtype=I32)
    row_dst = dump.at[dest.reshape(-1)].set(jnp.concatenate([tok, t + tok]))
    row_tok = jnp.where(row_dst < dump_row0, row_dst % t, 0)
    row_dst = jnp.concatenate([row_dst, dump_row0 + tmoe + jnp.arange(tmoe, dtype=I32)])
    block_row0 = jnp.arange(n_blocks, dtype=I32) * tmoe
    block_e = jnp.minimum(jnp.sum((block_row0[:, None] >= pend[None, :]).astype(I32), axis=1), N_EXPERTS - 1)
    n_valid = (pend[-1:] // tmoe).astype(I32)
    y = _moe_experts(x_slabs, row_tok, row_dst, dump_row0, block_e, n_valid, wg, wu, wd, tmoe)
    return _combine(y, x_slabs, gates[0], gates[1], ln_w, ln_b)


def kernel(x, positions, ln_w, ln_b, even_w_in, pool_w, pool_scale, swa_sinks, even_w_out, ffn_w_gate, ffn_w_up, ffn_w_down, odd_w_in, conv_w, odd_w_out, router_w, moe_w_gate, moe_w_up, moe_w_down):
    b, s, d = x.shape
    t = b * s
    x2d = x.reshape(t, d)
    cos_t, sin_t = _rope_tables(positions)
    ln = lambda layer, k: (ln_w[layer, k].reshape(1, d), ln_b[layer, k].reshape(1, d))

    u, q, k, v = _inproj0(x2d, even_w_in[0].astype(BF16), cos_t, sin_t)
    moe_w = (moe_w_gate[0], moe_w_up[0], moe_w_down[0])
    attn, moe_bf = _swa(q, k, v, swa_sinks[0], b, s, cast=[w.reshape(-1, w.shape[-1]) for w in moe_w])
    moe_bf = [wb.reshape(w.shape) for wb, w in zip(moe_bf, moe_w)]
    pool_bd = jax.scipy.linalg.block_diag(*[pool_w[0, gi] for gi in range(len(POOL_WINDOWS))]).astype(BF16)
    w_out0 = even_w_out[0].astype(BF16)
    x2d = _layer0_tail(u, attn, x2d, pool_bd, pool_scale[0].reshape(1, POOL_WIDTH),
                       w_out0[:POOL_WIDTH], w_out0[POOL_WIDTH:], ln(0, 0),
                       ffn_w_gate[0].astype(BF16), ffn_w_up[0].astype(BF16), ffn_w_down[0].astype(BF16),
                       ln(0, 1), b, s)

    w_in1 = odd_w_in[0].astype(BF16)
    c_in = len(DIL_PAIRS) * DIL_GROUP_IN
    wq, wkv = [], []
    for gi in range(len(DIL_PAIRS)):
        g0 = gi * DIL_GROUP_IN
        wq.append(w_in1[:, g0:g0 + DIL_Q_DIM])
        kcol = g0 + DIL_Q_DIM
        vcol = kcol + DIL_KV_DIM
        parts = []
        for h in range(DIL_KV_HEADS):
            parts += [w_in1[:, kcol + h * HEAD_DIM:kcol + (h + 1) * HEAD_DIM],
                      w_in1[:, vcol + h * HEAD_DIM:vcol + (h + 1) * HEAD_DIM]]
        wkv.append(jnp.concatenate(parts, axis=1))
    outs = _inproj1(x2d, jnp.stack(wq), jnp.stack(wkv), w_in1[:, c_in:], conv_w[0], cos_t, sin_t, b, s)
    c_out = _dilated(outs[0:3], outs[3:6], b, s)
    w_out1 = odd_w_out[0].astype(BF16)
    x_slabs, e_sel, gates, ranks, counts = _outproj1(
        c_out, outs[6].reshape(t, CONV_WIDTH), x2d, w_out1[:DIL_Q_DIM], w_out1[DIL_Q_DIM:], *ln(1, 0),
        router_w[0].T)
    x2d = _moe_layer(x_slabs, e_sel, gates, ranks, counts, *moe_bf, *ln(1, 1))
    return x2d.reshape(b, s, d)
```

```python
import functools
import math

import jax
import jax.numpy as jnp
from jax import lax
from jax.experimental import pallas as pl
from jax.experimental.pallas import tpu as pltpu

F32 = jnp.float32
BF16 = jnp.bfloat16
I32 = jnp.int32

HEAD_DIM = 64
ROPE_THETA = 10000.0
ATT_BLOCK = 128
LN_EPS = 1e-5
POOL_WINDOWS = (2, 4, 8, 16)
POOL_GROUP = 64
POOL_WIDTH = 256
POOL_HALO = 16
SWA_WINDOW = 128
SWA_Q_HEADS = 12
SWA_KV_HEADS = 4
SWA_Q_DIM = SWA_Q_HEADS * HEAD_DIM
SWA_KV_DIM = SWA_KV_HEADS * HEAD_DIM
DIL_PAIRS = ((128, 1), (512, 4), (2048, 16))
DIL_Q_HEADS = 8
DIL_KV_HEADS = 2
DIL_Q_DIM = DIL_Q_HEADS * HEAD_DIM
DIL_KV_DIM = DIL_KV_HEADS * HEAD_DIM
DIL_GROUP_IN = DIL_Q_DIM + 2 * DIL_KV_DIM
DIL_Q_PER_KV = DIL_Q_HEADS // DIL_KV_HEADS
CONV_WIDTH = 512
CONV_K = 3
CONV_HALO = 8
N_EXPERTS = 8
DEPTH = 2
ALPHA = (2 * DEPTH) ** 0.25
QK_SCALE = 1.0 / math.sqrt(HEAD_DIM)
NEG_BIG = -1e30
KV_PARTS = 4

LANES = 128
ROW_SLAB = 8
VMEM_LIMIT = 48 * 1024 * 1024
VMEM_LIMIT_BIG = 56 * 1024 * 1024


def _params(sem, vmem=VMEM_LIMIT):
    return pltpu.CompilerParams(dimension_semantics=sem, vmem_limit_bytes=vmem)


def _layer_norm(y, w, b):
    mu = jnp.mean(y, axis=-1, keepdims=True)
    yc = y - mu
    var = jnp.mean(yc * yc, axis=-1, keepdims=True)
    return yc * lax.rsqrt(var + LN_EPS) * w + b


def _rope_chunk(xc, cos, sin_signed, first_half):
    rot = jnp.where(first_half, pltpu.roll(xc, 96, 1), pltpu.roll(xc, 32, 1))
    return xc * cos + rot * sin_signed


def _rope(x, cos, sin_signed):
    tm, c = x.shape
    lane = lax.broadcasted_iota(I32, (tm, LANES), 1)
    first_half = (lane & 32) == 0
    chunks = [_rope_chunk(x[:, i * LANES:(i + 1) * LANES], cos, sin_signed, first_half)
              for i in range(c // LANES)]
    return chunks[0] if len(chunks) == 1 else jnp.concatenate(chunks, axis=1)


def _rope_tables(pos, inv):
    ang = inv * pos.astype(F32)
    c = jnp.cos(ang)
    s = jnp.sin(ang)
    c4 = jnp.concatenate([c, c, c, c], axis=0)
    s4 = jnp.concatenate([-s, s, -s, s], axis=0)
    return c4.T, s4.T


def _inproj0_kernel(x_ref, w_ref, pos_ref, inv_ref, u_ref, q_ref, k_ref, v_ref, cos_ref, sin_ref):
    xb = x_ref[...].astype(BF16)
    cos, sin = _rope_tables(pos_ref[...], inv_ref[...])
    cos_ref[...] = cos
    sin_ref[...] = sin
    q0 = POOL_WIDTH
    k0 = q0 + SWA_Q_DIM
    v0 = k0 + SWA_KV_DIM
    u_ref[...] = jnp.dot(xb, w_ref[:, :q0], preferred_element_type=F32)
    q = jnp.dot(xb, w_ref[:, q0:k0], preferred_element_type=F32)
    q_ref[...] = (_rope(q, cos, sin) * QK_SCALE).astype(BF16)
    k = jnp.dot(xb, w_ref[:, k0:v0], preferred_element_type=F32)
    k_ref[...] = _rope(k, cos, sin).astype(BF16)
    v_ref[...] = jnp.dot(xb, w_ref[:, v0:], preferred_element_type=F32).astype(BF16)


def _inproj0(x2d, w_bf, positions, tm=512):
    t, d = x2d.shape
    n_in = w_bf.shape[1]
    half = HEAD_DIM // 2
    inv = ROPE_THETA ** (-jnp.arange(half, dtype=F32) / half)
    row = lambda i: (i, 0)
    return pl.pallas_call(
        _inproj0_kernel,
        grid=(t // tm,),
        in_specs=[pl.BlockSpec((tm, d), row),
                  pl.BlockSpec((d, n_in), lambda i: (0, 0)),
                  pl.BlockSpec((1, tm), lambda i: (0, i)),
                  pl.BlockSpec((half, 1), lambda i: (0, 0))],
        out_specs=[pl.BlockSpec((tm, POOL_WIDTH), row),
                   pl.BlockSpec((tm, SWA_Q_DIM), row),
                   pl.BlockSpec((tm, SWA_KV_DIM), row),
                   pl.BlockSpec((tm, SWA_KV_DIM), row),
                   pl.BlockSpec((tm, LANES), row),
                   pl.BlockSpec((tm, LANES), row)],
        out_shape=[jax.ShapeDtypeStruct((t, POOL_WIDTH), F32),
                   jax.ShapeDtypeStruct((t, SWA_Q_DIM), BF16),
                   jax.ShapeDtypeStruct((t, SWA_KV_DIM), BF16),
                   jax.ShapeDtypeStruct((t, SWA_KV_DIM), BF16),
                   jax.ShapeDtypeStruct((t, LANES), F32),
                   jax.ShapeDtypeStruct((t, LANES), F32)],
        compiler_params=_params(("parallel",)),
        name="inproj0",
    )(x2d, w_bf, positions.reshape(1, t), inv.reshape(half, 1))


def _band_bias(rows, max_dist, key_lo):
    qi = lax.broadcasted_iota(I32, (rows, 2 * ATT_BLOCK), 0) & (ATT_BLOCK - 1)
    sj = lax.broadcasted_iota(I32, (rows, 2 * ATT_BLOCK), 1)
    dist = ATT_BLOCK + qi - sj
    return jnp.where((dist >= 0) & (dist <= max_dist) & (sj >= key_lo), 0.0, NEG_BIG)


def _band_mask(max_dist, key_lo):
    qi = lax.broadcasted_iota(I32, (ATT_BLOCK, 2 * ATT_BLOCK), 0)
    sj = lax.broadcasted_iota(I32, (ATT_BLOCK, 2 * ATT_BLOCK), 1)
    dist = ATT_BLOCK + qi - sj
    return (dist >= 0) & (dist <= max_dist) & (sj >= key_lo)


def _attn_block(q, kwin, vwin, valid, sink):
    s = lax.dot_general(q, kwin, (((1,), (1,)), ((), ())), preferred_element_type=F32)
    s = jnp.where(valid, s, NEG_BIG)
    m = jnp.maximum(jnp.max(s, axis=1, keepdims=True), sink)
    p = jnp.exp(s - m)
    den = jnp.sum(p, axis=1, keepdims=True) + jnp.exp(sink - m)
    o = jnp.dot(p.astype(BF16), vwin, preferred_element_type=F32)
    return o * (1.0 / den)


def _attn_stack(qs, kpart, vpart, bias):
    s = lax.dot_general(qs, kpart, (((1,), (1,)), ((), ())), preferred_element_type=F32) + bias
    m = jnp.max(s, axis=1, keepdims=True)
    p = jnp.exp(s - m)
    den = jnp.sum(p, axis=1, keepdims=True)
    return jnp.dot(p.astype(BF16), vpart, preferred_element_type=F32), m, den


def _kv_parts(kvx, h=0):
    base = h * KV_PARTS * LANES
    return [kvx[:, base + part * LANES:base + (part + 1) * LANES] for part in range(KV_PARTS)]


def _swa_kernel(sink_ref, q_ref, kp_ref, kc_ref, vp_ref, vc_ref, *rest, tq, n_cast):
    o_ref = rest[n_cast]
    for src, dst in zip(rest[:n_cast], rest[n_cast + 1:]):
        dst[...] = src[...].astype(BF16)
    i = pl.program_id(1)
    kfull = jnp.concatenate([kp_ref[...], kc_ref[...]], axis=0)
    vfull = jnp.concatenate([vp_ref[...], vc_ref[...]], axis=0)
    g = SWA_Q_HEADS // SWA_KV_HEADS
    for j in range(tq // ATT_BLOCK):
        key_lo = jnp.where(i == 0, ATT_BLOCK, 0) if j == 0 else 0
        valid = _band_mask(SWA_WINDOW - 1, key_lo)
        r0 = j * ATT_BLOCK
        outs = []
        for h in range(SWA_Q_HEADS):
            kv = h // g
            q = q_ref[r0:r0 + ATT_BLOCK, h * HEAD_DIM:(h + 1) * HEAD_DIM]
            kwin = kfull[r0:r0 + 2 * ATT_BLOCK, kv * HEAD_DIM:(kv + 1) * HEAD_DIM]
            vwin = vfull[r0:r0 + 2 * ATT_BLOCK, kv * HEAD_DIM:(kv + 1) * HEAD_DIM]
            outs.append(_attn_block(q, kwin, vwin, valid, sink_ref[h]))
        o_ref[r0:r0 + ATT_BLOCK, :] = jnp.concatenate(outs, axis=1).astype(BF16)


def _swa(q, k, v, sinks, b, s, cast=(), tq=256):
    per = tq // ATT_BLOCK
    steps = s // tq
    cur = lambda bi, i: (bi, i, 0)
    prev = lambda bi, i: (bi, jnp.maximum(i * per - 1, 0), 0)
    q3 = q.reshape(b, s, SWA_Q_DIM)
    k3 = k.reshape(b, s, SWA_KV_DIM)
    v3 = v.reshape(b, s, SWA_KV_DIM)
    cast_specs = []
    for w in cast:
        rows = w.shape[0] // (b * steps)
        assert rows * b * steps == w.shape[0] and rows % 16 == 0, w.shape
        cast_specs.append(pl.BlockSpec((rows, w.shape[1]), lambda bi, i: (bi * steps + i, 0)))
    outs = pl.pallas_call(
        functools.partial(_swa_kernel, tq=tq, n_cast=len(cast)),
        grid=(b, steps),
        in_specs=[pl.BlockSpec(memory_space=pltpu.SMEM),
                  pl.BlockSpec((None, tq, SWA_Q_DIM), cur),
                  pl.BlockSpec((None, ATT_BLOCK, SWA_KV_DIM), prev),
                  pl.BlockSpec((None, tq, SWA_KV_DIM), cur),
                  pl.BlockSpec((None, ATT_BLOCK, SWA_KV_DIM), prev),
                  pl.BlockSpec((None, tq, SWA_KV_DIM), cur)] + cast_specs,
        out_specs=[pl.BlockSpec((None, tq, SWA_Q_DIM), cur)] + cast_specs,
        out_shape=[jax.ShapeDtypeStruct((b, s, SWA_Q_DIM), BF16)]
                  + [jax.ShapeDtypeStruct(w.shape, BF16) for w in cast],
        compiler_params=_params(("parallel", "parallel")),
        name="swa_attention",
    )(sinks, q3, k3, k3, v3, v3, *cast)
    return outs[0].reshape(b * s, SWA_Q_DIM), outs[1:]


def _pool_mixer(u, halo, seq_row0):
    tm = u.shape[0]
    full = jnp.concatenate([halo, u], axis=0)
    sums = [full]
    for shift in (1, 2, 4, 8):
        prev = sums[-1]
        sums.append(prev + pltpu.roll(prev, shift, 0))
    lane = lax.broadcasted_iota(I32, (tm, POOL_WIDTH), 1)
    row = lax.broadcasted_iota(I32, (tm, POOL_WIDTH), 0)
    grp = lane // POOL_GROUP
    win = sums[4][POOL_HALO:]
    width = jnp.full((tm, POOL_WIDTH), POOL_WINDOWS[3], I32)
    for gi in (2, 1, 0):
        win = jnp.where(grp == gi, sums[gi + 1][POOL_HALO:], win)
        width = jnp.where(grp == gi, POOL_WINDOWS[gi], width)
    count = jnp.minimum(seq_row0 + row + 1, width).astype(F32)
    return win / count - u


def _swiglu_tile(xb, wg, wu, wd):
    g = jnp.dot(xb, wg, preferred_element_type=F32)
    u = jnp.dot(xb, wu, preferred_element_type=F32)
    h = (g * jax.nn.sigmoid(g)) * u
    return jnp.dot(h.astype(BF16), wd, preferred_element_type=F32)


def _layer0_tail_kernel(u_ref, uh_ref, o_ref, x_ref, pw_ref, ps_ref, wa_ref, wb_ref, lw0_ref, lb0_ref,
                        wg_ref, wu_ref, wd_ref, lw1_ref, lb1_ref, out_ref, *, tm):
    i = pl.program_id(1)
    halo = jnp.where(i == 0, 0.0, uh_ref[...])
    d = _pool_mixer(u_ref[...], halo, i * tm)
    a = jnp.dot(d.astype(BF16), pw_ref[...], preferred_element_type=F32) * ps_ref[...]
    mix = jnp.dot(a.astype(BF16), wa_ref[...], preferred_element_type=F32)
    mix = mix + jnp.dot(o_ref[...], wb_ref[...], preferred_element_type=F32)
    x1 = _layer_norm(ALPHA * x_ref[...] + mix, lw0_ref[...], lb0_ref[...])
    ffn = _swiglu_tile(x1.astype(BF16), wg_ref[...], wu_ref[...], wd_ref[...])
    out_ref[...] = _layer_norm(ALPHA * x1 + ffn, lw1_ref[...], lb1_ref[...])


def _layer0_tail(u, attn, x2d, pool_bd, pool_scale, wa, wb, ln0, wg, wu, wd, ln1, b, s, tm=512):
    d = x2d.shape[1]
    f = wg.shape[1]
    per = tm // POOL_HALO
    cur = lambda bi, i: (bi, i, 0)
    prev = lambda bi, i: (bi, jnp.maximum(i * per - 1, 0), 0)
    const = lambda bi, i: (0, 0)
    out = pl.pallas_call(
        functools.partial(_layer0_tail_kernel, tm=tm),
        grid=(b, s // tm),
        in_specs=[pl.BlockSpec((None, tm, POOL_WIDTH), cur),
                  pl.BlockSpec((None, POOL_HALO, POOL_WIDTH), prev),
                  pl.BlockSpec((None, tm, SWA_Q_DIM), cur),
                  pl.BlockSpec((None, tm, d), cur),
                  pl.BlockSpec((POOL_WIDTH, POOL_WIDTH), const),
                  pl.BlockSpec((1, POOL_WIDTH), const),
                  pl.BlockSpec((POOL_WIDTH, d), const),
                  pl.BlockSpec((SWA_Q_DIM, d), const),
                  pl.BlockSpec((1, d), const),
                  pl.BlockSpec((1, d), const),
                  pl.BlockSpec((d, f), const),
                  pl.BlockSpec((d, f), const),
                  pl.BlockSpec((f, d), const),
                  pl.BlockSpec((1, d), const),
                  pl.BlockSpec((1, d), const)],
        out_specs=pl.BlockSpec((None, tm, d), cur),
        out_shape=jax.ShapeDtypeStruct((b, s, d), F32),
        compiler_params=_params(("parallel", "parallel"), vmem=VMEM_LIMIT_BIG),
        name="layer0_tail",
    )(u.reshape(b, s, POOL_WIDTH), u.reshape(b, s, POOL_WIDTH), attn.reshape(b, s, SWA_Q_DIM),
      x2d.reshape(b, s, d), pool_bd, pool_scale, wa, wb, *ln0, wg, wu, wd, *ln1)
    return out.reshape(b * s, d)


def _kv_operands(kv, low):
    swapped = pltpu.roll(kv, HEAD_DIM, 1)
    return (jnp.where(low, kv, 0.0), jnp.where(low, 0.0, swapped),
            jnp.where(low, swapped, 0.0), jnp.where(low, 0.0, kv))


def _inproj1_kernel(x_ref, wq_ref, wkv_ref, wc_ref, cw_ref, cos_ref, sin_ref,
                    q0_ref, q1_ref, q2_ref, kv0_ref, kv1_ref, kv2_ref, d_ref,
                    zc_ref, sq_ref, skv_ref, *, tm):
    i = pl.program_id(1)
    xb = x_ref[...].astype(BF16)
    cos = cos_ref[...]
    sin = sin_ref[...]
    lane = lax.broadcasted_iota(I32, (tm, LANES), 1)
    low = lane < HEAD_DIM
    cos_k = jnp.where(low, cos, 1.0)
    sin_k = jnp.where(low, sin, 0.0)
    q_refs = (q0_ref, q1_ref, q2_ref)
    kv_refs = (kv0_ref, kv1_ref, kv2_ref)
    for gi, (_, dil) in enumerate(DIL_PAIRS):
        q = jnp.dot(xb, wq_ref[gi], preferred_element_type=F32)
        q = _rope(q, cos, sin) * QK_SCALE
        kv = jnp.dot(xb, wkv_ref[gi], preferred_element_type=F32)
        kv = _rope(kv, cos_k, sin_k)
        kvx = [_kv_operands(kv[:, h * LANES:(h + 1) * LANES], low) for h in range(DIL_KV_HEADS)]
        if dil == 1:
            q_refs[gi][0] = q.astype(BF16)
            for h in range(DIL_KV_HEADS):
                kv_refs[gi][h, 0] = jnp.concatenate(kvx[h], axis=1).astype(BF16)
        else:
            n = tm // dil
            for c in range(DIL_Q_DIM // LANES):
                sq_ref[c] = q[:, c * LANES:(c + 1) * LANES]
            for h in range(DIL_KV_HEADS):
                for part in range(KV_PARTS):
                    skv_ref[h * KV_PARTS + part] = kvx[h][part]
            for r in range(dil):
                rows = pl.ds(r, n, stride=dil)
                q_refs[gi][r] = jnp.concatenate(
                    [sq_ref[c, rows, :] for c in range(DIL_Q_DIM // LANES)], axis=1).astype(BF16)
                for h in range(DIL_KV_HEADS):
                    kv_refs[gi][h, r] = jnp.concatenate(
                        [skv_ref[h * KV_PARTS + part, rows, :] for part in range(KV_PARTS)], axis=1).astype(BF16)

    hc = jnp.dot(xb, wc_ref[...], preferred_element_type=F32)
    z = hc[:, 2 * CONV_WIDTH:] * hc[:, :CONV_WIDTH]
    zprev = jnp.where(i == 0, 0.0, zc_ref[...])
    zfull = jnp.concatenate([zprev, z], axis=0)
    z1 = pltpu.roll(zfull, 1, 0)[CONV_HALO:]
    z2 = pltpu.roll(zfull, 2, 0)[CONV_HALO:]
    cw = cw_ref[...]
    y = cw[0:1] * z2 + cw[1:2] * z1 + cw[2:3] * z
    d_ref[...] = (hc[:, CONV_WIDTH:2 * CONV_WIDTH] * y).astype(BF16)
    zc_ref[...] = z[tm - CONV_HALO:]


def _inproj1(x2d, wq, wkv, wc, conv_w, cos_t, sin_t, b, s, tm=512):
    d = x2d.shape[1]
    cur = lambda bi, i: (bi, i, 0)
    row = lambda bi, i: (bi * (s // tm) + i, 0)
    in_specs = [pl.BlockSpec((None, tm, d), cur),
                pl.BlockSpec(wq.shape, lambda bi, i: (0, 0, 0)),
                pl.BlockSpec(wkv.shape, lambda bi, i: (0, 0, 0)),
                pl.BlockSpec(wc.shape, lambda bi, i: (0, 0)),
                pl.BlockSpec(conv_w.shape, lambda bi, i: (0, 0)),
                pl.BlockSpec((tm, LANES), row),
                pl.BlockSpec((tm, LANES), row)]
    out_specs, out_shape = [], []
    for _, dil in DIL_PAIRS:
        out_specs.append(pl.BlockSpec((None, dil, tm // dil, DIL_Q_DIM), lambda bi, i: (bi, 0, i, 0)))
        out_shape.append(jax.ShapeDtypeStruct((b, dil, s // dil, DIL_Q_DIM), BF16))
    for _, dil in DIL_PAIRS:
        out_specs.append(pl.BlockSpec((None, DIL_KV_HEADS, dil, tm // dil, KV_PARTS * LANES),
                                      lambda bi, i: (bi, 0, 0, i, 0)))
        out_shape.append(jax.ShapeDtypeStruct((b, DIL_KV_HEADS, dil, s // dil, KV_PARTS * LANES), BF16))
    out_specs.append(pl.BlockSpec((None, tm, CONV_WIDTH), cur))
    out_shape.append(jax.ShapeDtypeStruct((b, s, CONV_WIDTH), BF16))
    return pl.pallas_call(
        functools.partial(_inproj1_kernel, tm=tm),
        grid=(b, s // tm),
        in_specs=in_specs,
        out_specs=out_specs,
        out_shape=out_shape,
        scratch_shapes=[pltpu.VMEM((CONV_HALO, CONV_WIDTH), F32),
                        pltpu.VMEM((DIL_Q_DIM // LANES, tm, LANES), F32),
                        pltpu.VMEM((DIL_KV_HEADS * KV_PARTS, tm, LANES), F32)],
        compiler_params=_params(("parallel", "arbitrary")),
        name="inproj1",
    )(x2d.reshape(b, s, d), wq, wkv, wc, conv_w, cos_t, sin_t)


def _dil_kernel(q0_ref, q1_ref, q2_ref, kv0_ref, kv1_ref, kv2_ref, out_ref, o_run, l_run, bias_ref, *, s):
    q_refs = (q0_ref, q1_ref, q2_ref)
    kv_refs = (kv0_ref, kv1_ref, kv2_ref)
    n_blocks = s // ATT_BLOCK
    width = DIL_Q_PER_KV * HEAD_DIM
    n_chunks = width // LANES
    max_dist = DIL_PAIRS[0][0] // DIL_PAIRS[0][1]
    assert all(w // d == max_dist for w, d in DIL_PAIRS)
    stacked = n_chunks * ATT_BLOCK
    bias_ref[0] = _band_bias(stacked, max_dist, 0)
    bias_ref[1] = _band_bias(stacked, max_dist, ATT_BLOCK)
    for gi, (_, dil) in enumerate(DIL_PAIRS):
        q_ref, kv_ref = q_refs[gi], kv_refs[gi]
        blocks_per_sub = (s // dil) // ATT_BLOCK
        shift = blocks_per_sub.bit_length() - 1

        def body(n, carry, q_ref=q_ref, kv_ref=kv_ref, gi=gi, dil=dil, blocks_per_sub=blocks_per_sub, shift=shift):
            res = lax.shift_right_logical(n, shift)
            bi = n & (blocks_per_sub - 1)
            rows = pl.ds(pl.multiple_of(n * ATT_BLOCK, ATT_BLOCK), ATT_BLOCK)
            prow = pl.ds(pl.multiple_of(jnp.maximum(n - 1, 0) * ATT_BLOCK, ATT_BLOCK), ATT_BLOCK)
            kvx = jnp.concatenate([kv_ref[prow, :], kv_ref[rows, :]], axis=0)
            kz, zk, vz, zv = _kv_parts(kvx)
            q4 = q_ref[rows, :]
            qs = jnp.concatenate([q4[:, c * LANES:(c + 1) * LANES] for c in range(n_chunks)], axis=0)
            bias = bias_ref[jnp.where(bi == 0, 1, 0)]
            oa, ma, da = _attn_stack(qs, kz, vz, bias)
            ob, mb, db = _attn_stack(qs, zk, zv, bias)
            low = lax.broadcasted_iota(I32, (n_chunks * ATT_BLOCK, LANES), 1) < HEAD_DIM
            o_st = (oa + ob) * jnp.where(low, 1.0 / da, 1.0 / db)
            l_st = jnp.where(low, ma + jnp.log(da), mb + jnp.log(db))
            start = res + dil * ATT_BLOCK * bi
            if dil == 1:
                tok = pl.ds(pl.multiple_of(start, ATT_BLOCK), ATT_BLOCK)
            else:
                tok = pl.ds(start, ATT_BLOCK, stride=dil)
            for c in range(n_chunks):
                o_c = o_st[c * ATT_BLOCK:(c + 1) * ATT_BLOCK]
                l_c = l_st[c * ATT_BLOCK:(c + 1) * ATT_BLOCK]
                if gi == 0:
                    o_run[c, tok, :] = o_c
                    l_run[c, tok, :] = l_c
                else:
                    o_old = o_run[c, tok, :]
                    l_old = l_run[c, tok, :]
                    m = jnp.maximum(l_old, l_c)
                    a = jnp.exp(l_old - m)
                    bb = jnp.exp(l_c - m)
                    tot = a + bb
                    o_run[c, tok, :] = (o_old * a + o_c * bb) * (1.0 / tot)
                    if gi < len(DIL_PAIRS) - 1:
                        l_run[c, tok, :] = m + jnp.log(tot)
            return carry

        lax.fori_loop(0, n_blocks, body, 0, unroll=8)
    for c in range(n_chunks):
        out_ref[:, c * LANES:(c + 1) * LANES] = o_run[c].astype(BF16)


def _dilated(qs, kvs, b, s):
    width = DIL_Q_PER_KV * HEAD_DIM
    in_specs = [pl.BlockSpec((None, s, width), lambda bi, h: (bi, 0, h)) for _ in DIL_PAIRS]
    in_specs += [pl.BlockSpec((None, None, s, KV_PARTS * LANES), lambda bi, h: (bi, h, 0, 0)) for _ in DIL_PAIRS]
    qs = [q.reshape(b, s, DIL_Q_DIM) for q in qs]
    kvs = [kv.reshape(b, DIL_KV_HEADS, s, KV_PARTS * LANES) for kv in kvs]
    out = pl.pallas_call(
        functools.partial(_dil_kernel, s=s),
        grid=(b, DIL_KV_HEADS),
        in_specs=in_specs,
        out_specs=pl.BlockSpec((None, s, width), lambda bi, h: (bi, 0, h)),
        out_shape=jax.ShapeDtypeStruct((b, s, DIL_Q_DIM), BF16),
        scratch_shapes=[pltpu.VMEM((width // LANES, s, LANES), F32),
                        pltpu.VMEM((width // LANES, s, LANES), F32),
                        pltpu.VMEM((2, (width // LANES) * ATT_BLOCK, 2 * ATT_BLOCK), F32)],
        compiler_params=_params(("parallel", "parallel"), vmem=VMEM_LIMIT_BIG),
        name="dilated_attention",
    )(*qs, *kvs)
    return out.reshape(b * s, DIL_Q_DIM)


def _to_slabs(ref, val, base=0):
    rows, d = val.shape
    per = d // LANES
    for c in range(per):
        ref[pl.ds(base + c, rows, stride=per), :] = val[:, c * LANES:(c + 1) * LANES]


def _from_slabs(ref, rows, base=0, per=ROW_SLAB):
    return jnp.concatenate([ref[pl.ds(base + c, rows, stride=per), :] for c in range(per)], axis=1)


def _route_tile(x, wr, e_ref, g_ref, r_ref, cnt_ref, run_ref, tm, col0):
    cols = slice(col0, col0 + tm)
    logits = lax.dot_general(wr, x, (((1,), (1,)), ((), ())),
                             precision=lax.Precision.HIGHEST, preferred_element_type=F32)
    eid = lax.broadcasted_iota(I32, (N_EXPERTS, tm), 0)
    m1 = jnp.max(logits, axis=0, keepdims=True)
    i1 = jnp.min(jnp.where(logits == m1, eid, N_EXPERTS), axis=0, keepdims=True)
    rest = jnp.where(eid == i1, -jnp.inf, logits)
    m2 = jnp.max(rest, axis=0, keepdims=True)
    i2 = jnp.min(jnp.where(rest == m2, eid, N_EXPERTS), axis=0, keepdims=True)
    t2 = jnp.exp(m2 - m1)
    g1 = 1.0 / (1.0 + t2)
    g2 = t2 / (1.0 + t2)
    oh1 = eid == i1
    oh2 = eid == i2
    oh = jnp.where(oh1 | oh2, 1.0, 0.0)
    ri = lax.broadcasted_iota(I32, (tm, tm), 0)
    ci = lax.broadcasted_iota(I32, (tm, tm), 1)
    tri = jnp.where(ri <= ci, 1.0, 0.0).astype(BF16)
    incl = jnp.dot(oh.astype(BF16), tri, preferred_element_type=F32)
    rank = run_ref[:, 0:1] + incl - oh
    r1 = jnp.sum(jnp.where(oh1, rank, 0.0), axis=0, keepdims=True)
    r2 = jnp.sum(jnp.where(oh2, rank, 0.0), axis=0, keepdims=True)
    e_ref[0:1, cols] = i1
    e_ref[1:2, cols] = i2
    g_ref[0:1, cols] = g1
    g_ref[1:2, cols] = g2
    r_ref[0:1, cols] = r1.astype(I32)
    r_ref[1:2, cols] = r2.astype(I32)
    run_ref[...] = run_ref[...] + incl[:, tm - 1:tm]
    cnt_ref[...] = run_ref[...]


def _outproj1_kernel(c_ref, d_ref, x_ref, wa_ref, wb_ref, lw_ref, lb_ref, wr_ref,
                     slab_ref, e_ref, g_ref, r_ref, cnt_ref, run_ref, *, tm, sub):
    @pl.when(pl.program_id(0) == 0)
    def _():
        run_ref[...] = jnp.zeros_like(run_ref)

    for h in range(tm // sub):
        rows = slice(h * sub, (h + 1) * sub)
        mix = jnp.dot(c_ref[rows, :], wa_ref[...], preferred_element_type=F32)
        mix = mix + jnp.dot(d_ref[rows, :], wb_ref[...], preferred_element_type=F32)
        y = _layer_norm(ALPHA * x_ref[rows, :] + mix, lw_ref[...], lb_ref[...])
        _to_slabs(slab_ref, y, base=h * sub * ROW_SLAB)
        _route_tile(y, wr_ref[...], e_ref, g_ref, r_ref, cnt_ref, run_ref, sub, h * sub)


def _outproj1(c, dconv, x2d, wa, wb, ln_w, ln_b, wr_t, tm=1024, sub=512):
    t, d = x2d.shape
    row = lambda i: (i, 0)
    col = lambda i: (0, i)
    const = lambda i: (0, 0)
    return pl.pallas_call(
        functools.partial(_outproj1_kernel, tm=tm, sub=sub),
        grid=(t // tm,),
        in_specs=[pl.BlockSpec((tm, DIL_Q_DIM), row),
                  pl.BlockSpec((tm, CONV_WIDTH), row),
                  pl.BlockSpec((tm, d), row),
                  pl.BlockSpec((DIL_Q_DIM, d), const),
                  pl.BlockSpec((CONV_WIDTH, d), const),
                  pl.BlockSpec((1, d), const),
                  pl.BlockSpec((1, d), const),
                  pl.BlockSpec((N_EXPERTS, d), const)],
        out_specs=[pl.BlockSpec((tm * (d // LANES), LANES), row),
                   pl.BlockSpec((2, tm), col), pl.BlockSpec((2, tm), col), pl.BlockSpec((2, tm), col),
                   pl.BlockSpec((N_EXPERTS, LANES), const)],
        out_shape=[jax.ShapeDtypeStruct((t * (d // LANES), LANES), F32),
                   jax.ShapeDtypeStruct((2, t), I32), jax.ShapeDtypeStruct((2, t), F32),
                   jax.ShapeDtypeStruct((2, t), I32), jax.ShapeDtypeStruct((N_EXPERTS, LANES), F32)],
        scratch_shapes=[pltpu.VMEM((N_EXPERTS, LANES), F32)],
        compiler_params=_params(("arbitrary",)),
        name="outproj1_route",
    )(c, dconv, x2d, wa, wb, ln_w, ln_b, wr_t)


def _slab_copy(src_ref, dst_ref, sem, src_row, dst_row, n=1):
    src = src_ref.at[pl.ds(pl.multiple_of(src_row * ROW_SLAB, ROW_SLAB), n * ROW_SLAB)]
    dst = dst_ref.at[pl.ds(pl.multiple_of(dst_row * ROW_SLAB, ROW_SLAB), n * ROW_SLAB)]
    return pltpu.make_async_copy(src, dst, sem)


def _moe_kernel(be_ref, nv_ref, tok_ref, tokn_ref, dstp_ref, dstc_ref, x_hbm, wg_ref, wu_ref, wd_ref, y_hbm,
                xs_ref, xb_ref, acc_ref, stage_ref, sem_in, sem_out, *, tm, nj, dump_row0):
    i = pl.program_id(0)
    j = pl.program_id(1)
    n_valid = nv_ref[0]
    slot = i & 1
    per_step = tm // nj

    def row_in(tok, dst_slot, r):
        return _slab_copy(x_hbm, xs_ref, sem_in.at[dst_slot], tok, dst_slot * tm + r)

    def row_out(src_slot, r, dst):
        return _slab_copy(stage_ref, y_hbm, sem_out.at[src_slot], src_slot * tm + r, dst)

    @pl.when(i < n_valid)
    def _():
        @pl.when(j == 0)
        def _():
            @pl.when(i == 0)
            def _():
                def start(r, c):
                    row_in(tok_ref[0, 0, r], 0, r).start()
                    return c
                lax.fori_loop(0, tm, start, 0)
                stage_ref[...] = jnp.zeros_like(stage_ref)
                fill = _slab_copy(stage_ref, y_hbm, sem_out.at[0], 0, dump_row0, n=2 * tm)
                fill.start()
                fill.wait()

            _slab_copy(x_hbm, xs_ref, sem_in.at[slot], 0, slot * tm, n=tm).wait()
            xb_ref[...] = _from_slabs(xs_ref, tm, base=slot * (tm * ROW_SLAB)).astype(BF16)

        for r in range(per_step):
            rr = j * per_step + r
            row_in(tokn_ref[0, 0, rr], 1 - slot, rr).start(priority=1)
            row_out(1 - slot, rr, dstp_ref[0, 0, rr]).start(priority=1)

        part = _swiglu_tile(xb_ref[...], wg_ref[...], wu_ref[...], wd_ref[...])
        if nj > 1:
            @pl.when(j == 0)
            def _():
                acc_ref[...] = part

            @pl.when(j > 0)
            def _():
                acc_ref[...] += part

        @pl.when(j == nj - 1)
        def _():
            _slab_copy(stage_ref, y_hbm, sem_out.at[1 - slot], (1 - slot) * tm, 0, n=tm).wait()
            _to_slabs(stage_ref, acc_ref[...] if nj > 1 else part, base=slot * (tm * ROW_SLAB))

            @pl.when(i + 1 >= n_valid)
            def _():
                _slab_copy(x_hbm, xs_ref, sem_in.at[1 - slot], 0, (1 - slot) * tm, n=tm).wait()

                def start(r, c):
                    row_out(slot, r, dstc_ref[0, 0, r]).start()
                    return c
                lax.fori_loop(0, tm, start, 0)
                _slab_copy(stage_ref, y_hbm, sem_out.at[slot], slot * tm, 0, n=tm).wait()


def _moe_experts(x_slabs, row_tok, row_dst, dump_row0, block_e, n_valid, wg, wu, wd, tm, tf=None):
    d = wg.shape[1]
    n_rows = row_tok.shape[0]
    n_blocks = n_rows // tm
    f = wg.shape[2]
    tf = f if tf is None else tf
    nj = f // tf
    y_rows = dump_row0 + 2 * tm
    w_mode = dict(pipeline_mode=pl.Buffered(1)) if nj == 1 else {}

    def live(i, j, be, nv):
        ok = i < nv[0]
        return jnp.where(ok, i, nv[0] - 1), jnp.where(ok, j, nj - 1)

    def w_map(i, j, be, nv):
        ii, jj = live(i, j, be, nv)
        return be[ii], 0, jj

    def wd_map(i, j, be, nv):
        ii, jj = live(i, j, be, nv)
        return be[ii], jj, 0

    def smem(index_map):
        return pl.BlockSpec((1, 1, tm), index_map, memory_space=pltpu.SMEM)

    grid_spec = pltpu.PrefetchScalarGridSpec(
        num_scalar_prefetch=2,
        grid=(n_blocks, nj),
        in_specs=[smem(lambda i, j, be, nv: (i, 0, 0)),
                  smem(lambda i, j, be, nv: (jnp.minimum(i + 1, n_blocks - 1), 0, 0)),
                  smem(lambda i, j, be, nv: (jnp.where(i == 0, n_blocks, i - 1), 0, 0)),
                  smem(lambda i, j, be, nv: (i, 0, 0)),
                  pl.BlockSpec(memory_space=pl.ANY),
                  pl.BlockSpec((None, d, tf), w_map, **w_mode),
                  pl.BlockSpec((None, d, tf), w_map, **w_mode),
                  pl.BlockSpec((None, tf, d), wd_map, **w_mode)],
        out_specs=pl.BlockSpec(memory_space=pl.ANY),
        scratch_shapes=[pltpu.VMEM((2 * tm * ROW_SLAB, LANES), F32),
                        pltpu.VMEM((tm, d), BF16),
                        pltpu.VMEM((tm, d) if nj > 1 else (ROW_SLAB, LANES), F32),
                        pltpu.VMEM((2 * tm * ROW_SLAB, LANES), F32),
                        pltpu.SemaphoreType.DMA((2,)),
                        pltpu.SemaphoreType.DMA((2,))],
    )
    tok3 = row_tok.reshape(n_blocks, 1, tm)
    dst3 = row_dst.reshape(n_blocks + 1, 1, tm)
    return pl.pallas_call(
        functools.partial(_moe_kernel, tm=tm, nj=nj, dump_row0=dump_row0),
        grid_spec=grid_spec,
        out_shape=jax.ShapeDtypeStruct((y_rows * ROW_SLAB, LANES), F32),
        compiler_params=_params(("arbitrary", "arbitrary"), vmem=VMEM_LIMIT_BIG),
        name="moe_experts",
    )(block_e, n_valid, tok3, tok3, dst3, dst3, x_slabs, wg, wu, wd)


def _combine_kernel(ya_ref, yb_ref, x_ref, g1_ref, g2_ref, lw_ref, lb_ref, out_ref, *, tm):
    ffn = _from_slabs(ya_ref, tm) * g1_ref[...] + _from_slabs(yb_ref, tm) * g2_ref[...]
    out_ref[...] = _layer_norm(ALPHA * _from_slabs(x_ref, tm) + ffn, lw_ref[...], lb_ref[...])


def _combine(y, x_slabs, g1, g2, ln_w, ln_b, tm=512):
    t = g1.shape[0]
    d = ln_w.shape[1]
    row = lambda i: (i, 0)
    return pl.pallas_call(
        functools.partial(_combine_kernel, tm=tm),
        grid=(t // tm,),
        in_specs=[pl.BlockSpec((tm * ROW_SLAB, LANES), row),
                  pl.BlockSpec((tm * ROW_SLAB, LANES), lambda i: (t // tm + i, 0)),
                  pl.BlockSpec((tm * ROW_SLAB, LANES), row),
                  pl.BlockSpec((tm, 1), row),
                  pl.BlockSpec((tm, 1), row),
                  pl.BlockSpec((1, d), lambda i: (0, 0)),
                  pl.BlockSpec((1, d), lambda i: (0, 0))],
        out_specs=pl.BlockSpec((tm, d), row),
        out_shape=jax.ShapeDtypeStruct((t, d), F32),
        compiler_params=_params(("parallel",)),
        name="moe_combine",
    )(y, y, x_slabs, g1.reshape(t, 1), g2.reshape(t, 1), ln_w, ln_b)


def _moe_layer(x_slabs, e_sel, gates, ranks, counts, wg, wu, wd, ln_w, ln_b, tmoe=512):
    t = e_sel.shape[1]
    counts = counts[:, 0].astype(I32)
    padded = (counts + tmoe - 1) // tmoe * tmoe
    pend = jnp.cumsum(padded)
    pstart = pend - padded
    n_blocks = -(-(2 * t + N_EXPERTS * (tmoe - 1)) // tmoe)
    n_rows = n_blocks * tmoe
    first_row = sum(jnp.where(e_sel == e, pstart[e], 0) for e in range(N_EXPERTS))
    dest = first_row + ranks
    dump_row0 = 2 * t
    r = jnp.arange(n_rows, dtype=I32)
    dump = dump_row0 + ((r // tmoe) & 1) * tmoe + r % tmoe
    tok = jnp.arange(t, dtype=I32)
    row_dst = dump.at[dest.reshape(-1)].set(jnp.concatenate([tok, t + tok]), unique_indices=True,
                                            mode='promise_in_bounds')
    row_tok = jnp.where(row_dst < dump_row0, row_dst % t, 0)
    row_dst = jnp.concatenate([row_dst, dump_row0 + tmoe + jnp.arange(tmoe, dtype=I32)])
    block_row0 = jnp.arange(n_blocks, dtype=I32) * tmoe
    block_e = jnp.minimum(jnp.sum((block_row0[:, None] >= pend[None, :]).astype(I32), axis=1), N_EXPERTS - 1)
    n_valid = (pend[-1:] // tmoe).astype(I32)
    y = _moe_experts(x_slabs, row_tok, row_dst, dump_row0, block_e, n_valid, wg, wu, wd, tmoe)
    return _combine(y, x_slabs, gates[0], gates[1], ln_w, ln_b)


def kernel(x, positions, ln_w, ln_b, even_w_in, pool_w, pool_scale, swa_sinks, even_w_out, ffn_w_gate, ffn_w_up, ffn_w_down, odd_w_in, conv_w, odd_w_out, router_w, moe_w_gate, moe_w_up, moe_w_down):
    b, s, d = x.shape
    t = b * s
    x2d = x.reshape(t, d)
    ln = lambda layer, k: (ln_w[layer, k].reshape(1, d), ln_b[layer, k].reshape(1, d))

    u, q, k, v, cos_t, sin_t = _inproj0(x2d, even_w_in[0].astype(BF16), positions)
    moe_w = (moe_w_gate[0], moe_w_up[0], moe_w_down[0])
    attn, moe_bf = _swa(q, k, v, swa_sinks[0], b, s, cast=[w.reshape(-1, w.shape[-1]) for w in moe_w])
    moe_bf = [wb.reshape(w.shape) for wb, w in zip(moe_bf, moe_w)]
    pool_bd = jax.scipy.linalg.block_diag(*[pool_w[0, gi] for gi in range(len(POOL_WINDOWS))]).astype(BF16)
    w_out0 = even_w_out[0].astype(BF16)
    x2d = _layer0_tail(u, attn, x2d, pool_bd, pool_scale[0].reshape(1, POOL_WIDTH),
                       w_out0[:POOL_WIDTH], w_out0[POOL_WIDTH:], ln(0, 0),
                       ffn_w_gate[0].astype(BF16), ffn_w_up[0].astype(BF16), ffn_w_down[0].astype(BF16),
                       ln(0, 1), b, s)

    w_in1 = odd_w_in[0].astype(BF16)
    c_in = len(DIL_PAIRS) * DIL_GROUP_IN
    wq, wkv = [], []
    for gi in range(len(DIL_PAIRS)):
        g0 = gi * DIL_GROUP_IN
        wq.append(w_in1[:, g0:g0 + DIL_Q_DIM])
        kcol = g0 + DIL_Q_DIM
        vcol = kcol + DIL_KV_DIM
        parts = []
        for h in range(DIL_KV_HEADS):
            parts += [w_in1[:, kcol + h * HEAD_DIM:kcol + (h + 1) * HEAD_DIM],
                      w_in1[:, vcol + h * HEAD_DIM:vcol + (h + 1) * HEAD_DIM]]
        wkv.append(jnp.concatenate(parts, axis=1))
    outs = _inproj1(x2d, jnp.stack(wq), jnp.stack(wkv), w_in1[:, c_in:], conv_w[0], cos_t, sin_t, b, s)
    c_out = _dilated(outs[0:3], outs[3:6], b, s)
    w_out1 = odd_w_out[0].astype(BF16)
    x_slabs, e_sel, gates, ranks, counts = _outproj1(
        c_out, outs[6].reshape(t, CONV_WIDTH), x2d, w_out1[:DIL_Q_DIM], w_out1[DIL_Q_DIM:], *ln(1, 0),
        router_w[0].T)
    x2d = _moe_layer(x_slabs, e_sel, gates, ranks, counts, *moe_bf, *ln(1, 1))
    return x2d.reshape(b, s, d)
```

```python
import functools
import math

import jax
import jax.numpy as jnp
from jax import lax
from jax.experimental import pallas as pl
from jax.experimental.pallas import tpu as pltpu
from jax.experimental.pallas import tpu_sc as plsc

F32 = jnp.float32
BF16 = jnp.bfloat16
I32 = jnp.int32

HEAD_DIM = 64
ROPE_THETA = 10000.0
ATT_BLOCK = 128
LN_EPS = 1e-5
POOL_WINDOWS = (2, 4, 8, 16)
POOL_GROUP = 64
POOL_WIDTH = 256
POOL_HALO = 16
SWA_WINDOW = 128
SWA_Q_HEADS = 12
SWA_KV_HEADS = 4
SWA_Q_DIM = SWA_Q_HEADS * HEAD_DIM
SWA_KV_DIM = SWA_KV_HEADS * HEAD_DIM
DIL_PAIRS = ((128, 1), (512, 4), (2048, 16))
DIL_Q_HEADS = 8
DIL_KV_HEADS = 2
DIL_Q_DIM = DIL_Q_HEADS * HEAD_DIM
DIL_KV_DIM = DIL_KV_HEADS * HEAD_DIM
DIL_GROUP_IN = DIL_Q_DIM + 2 * DIL_KV_DIM
DIL_Q_PER_KV = DIL_Q_HEADS // DIL_KV_HEADS
CONV_WIDTH = 512
CONV_K = 3
CONV_HALO = 8
N_EXPERTS = 8
DEPTH = 2
ALPHA = (2 * DEPTH) ** 0.25
QK_SCALE = 1.0 / math.sqrt(HEAD_DIM)
NEG_BIG = -1e30
KV_PARTS = 4

LANES = 128
ROW_SLAB = 8
VMEM_LIMIT = 48 * 1024 * 1024
VMEM_LIMIT_BIG = 56 * 1024 * 1024


def _params(sem, vmem=VMEM_LIMIT):
    return pltpu.CompilerParams(dimension_semantics=sem, vmem_limit_bytes=vmem)


def _layer_norm(y, w, b):
    mu = jnp.mean(y, axis=-1, keepdims=True)
    yc = y - mu
    var = jnp.mean(yc * yc, axis=-1, keepdims=True)
    return yc * lax.rsqrt(var + LN_EPS) * w + b


def _rope_chunk(xc, cos, sin_signed, first_half):
    rot = jnp.where(first_half, pltpu.roll(xc, 96, 1), pltpu.roll(xc, 32, 1))
    return xc * cos + rot * sin_signed


def _rope(x, cos, sin_signed):
    tm, c = x.shape
    lane = lax.broadcasted_iota(I32, (tm, LANES), 1)
    first_half = (lane & 32) == 0
    chunks = [_rope_chunk(x[:, i * LANES:(i + 1) * LANES], cos, sin_signed, first_half)
              for i in range(c // LANES)]
    return chunks[0] if len(chunks) == 1 else jnp.concatenate(chunks, axis=1)


def _rope_tables(pos, inv):
    ang = inv * pos.astype(F32)
    c = jnp.cos(ang)
    s = jnp.sin(ang)
    c4 = jnp.concatenate([c, c, c, c], axis=0)
    s4 = jnp.concatenate([-s, s, -s, s], axis=0)
    return c4.T, s4.T


def _inproj0_kernel(x_ref, w_ref, pos_ref, inv_ref, u_ref, q_ref, k_ref, v_ref, cos_ref, sin_ref):
    xb = x_ref[...].astype(BF16)
    cos, sin = _rope_tables(pos_ref[...], inv_ref[...])
    cos_ref[...] = cos
    sin_ref[...] = sin
    q0 = POOL_WIDTH
    k0 = q0 + SWA_Q_DIM
    v0 = k0 + SWA_KV_DIM
    u_ref[...] = jnp.dot(xb, w_ref[:, :q0], preferred_element_type=F32)
    q = jnp.dot(xb, w_ref[:, q0:k0], preferred_element_type=F32)
    q_ref[...] = (_rope(q, cos, sin) * QK_SCALE).astype(BF16)
    k = jnp.dot(xb, w_ref[:, k0:v0], preferred_element_type=F32)
    k_ref[...] = _rope(k, cos, sin).astype(BF16)
    v_ref[...] = jnp.dot(xb, w_ref[:, v0:], preferred_element_type=F32).astype(BF16)


def _inproj0(x2d, w_bf, positions, tm=512):
    t, d = x2d.shape
    n_in = w_bf.shape[1]
    half = HEAD_DIM // 2
    inv = ROPE_THETA ** (-jnp.arange(half, dtype=F32) / half)
    row = lambda i: (i, 0)
    return pl.pallas_call(
        _inproj0_kernel,
        grid=(t // tm,),
        in_specs=[pl.BlockSpec((tm, d), row),
                  pl.BlockSpec((d, n_in), lambda i: (0, 0)),
                  pl.BlockSpec((1, tm), lambda i: (0, i)),
                  pl.BlockSpec((half, 1), lambda i: (0, 0))],
        out_specs=[pl.BlockSpec((tm, POOL_WIDTH), row),
                   pl.BlockSpec((tm, SWA_Q_DIM), row),
                   pl.BlockSpec((tm, SWA_KV_DIM), row),
                   pl.BlockSpec((tm, SWA_KV_DIM), row),
                   pl.BlockSpec((tm, LANES), row),
                   pl.BlockSpec((tm, LANES), row)],
        out_shape=[jax.ShapeDtypeStruct((t, POOL_WIDTH), F32),
                   jax.ShapeDtypeStruct((t, SWA_Q_DIM), BF16),
                   jax.ShapeDtypeStruct((t, SWA_KV_DIM), BF16),
                   jax.ShapeDtypeStruct((t, SWA_KV_DIM), BF16),
                   jax.ShapeDtypeStruct((t, LANES), F32),
                   jax.ShapeDtypeStruct((t, LANES), F32)],
        compiler_params=_params(("parallel",)),
        name="inproj0",
    )(x2d, w_bf, positions.reshape(1, t), inv.reshape(half, 1))


def _band_bias(rows, max_dist, key_lo):
    qi = lax.broadcasted_iota(I32, (rows, 2 * ATT_BLOCK), 0) & (ATT_BLOCK - 1)
    sj = lax.broadcasted_iota(I32, (rows, 2 * ATT_BLOCK), 1)
    dist = ATT_BLOCK + qi - sj
    return jnp.where((dist >= 0) & (dist <= max_dist) & (sj >= key_lo), 0.0, NEG_BIG)


def _band_mask(max_dist, key_lo):
    qi = lax.broadcasted_iota(I32, (ATT_BLOCK, 2 * ATT_BLOCK), 0)
    sj = lax.broadcasted_iota(I32, (ATT_BLOCK, 2 * ATT_BLOCK), 1)
    dist = ATT_BLOCK + qi - sj
    return (dist >= 0) & (dist <= max_dist) & (sj >= key_lo)


def _attn_block(q, kwin, vwin, valid, sink):
    s = lax.dot_general(q, kwin, (((1,), (1,)), ((), ())), preferred_element_type=F32)
    s = jnp.where(valid, s, NEG_BIG)
    m = jnp.maximum(jnp.max(s, axis=1, keepdims=True), sink)
    p = jnp.exp(s - m)
    den = jnp.sum(p, axis=1, keepdims=True) + jnp.exp(sink - m)
    o = jnp.dot(p.astype(BF16), vwin, preferred_element_type=F32)
    return o * (1.0 / den)


def _attn_stack(qs, kpart, vpart, bias):
    s = lax.dot_general(qs, kpart, (((1,), (1,)), ((), ())), preferred_element_type=F32) + bias
    m = jnp.max(s, axis=1, keepdims=True)
    p = jnp.exp(s - m)
    den = jnp.sum(p, axis=1, keepdims=True)
    return jnp.dot(p.astype(BF16), vpart, preferred_element_type=F32), m, den


def _kv_parts(kvx, h=0):
    base = h * KV_PARTS * LANES
    return [kvx[:, base + part * LANES:base + (part + 1) * LANES] for part in range(KV_PARTS)]


def _swa_kernel(sink_ref, q_ref, kp_ref, kc_ref, vp_ref, vc_ref, *rest, tq, n_cast):
    o_ref = rest[n_cast]
    for src, dst in zip(rest[:n_cast], rest[n_cast + 1:]):
        dst[...] = src[...].astype(BF16)
    i = pl.program_id(1)
    kfull = jnp.concatenate([kp_ref[...], kc_ref[...]], axis=0)
    vfull = jnp.concatenate([vp_ref[...], vc_ref[...]], axis=0)
    g = SWA_Q_HEADS // SWA_KV_HEADS
    for j in range(tq // ATT_BLOCK):
        key_lo = jnp.where(i == 0, ATT_BLOCK, 0) if j == 0 else 0
        valid = _band_mask(SWA_WINDOW - 1, key_lo)
        r0 = j * ATT_BLOCK
        outs = []
        for h in range(SWA_Q_HEADS):
            kv = h // g
            q = q_ref[r0:r0 + ATT_BLOCK, h * HEAD_DIM:(h + 1) * HEAD_DIM]
            kwin = kfull[r0:r0 + 2 * ATT_BLOCK, kv * HEAD_DIM:(kv + 1) * HEAD_DIM]
            vwin = vfull[r0:r0 + 2 * ATT_BLOCK, kv * HEAD_DIM:(kv + 1) * HEAD_DIM]
            outs.append(_attn_block(q, kwin, vwin, valid, sink_ref[h]))
        o_ref[r0:r0 + ATT_BLOCK, :] = jnp.concatenate(outs, axis=1).astype(BF16)


def _swa(q, k, v, sinks, b, s, cast=(), tq=256):
    per = tq // ATT_BLOCK
    steps = s // tq
    cur = lambda bi, i: (bi, i, 0)
    prev = lambda bi, i: (bi, jnp.maximum(i * per - 1, 0), 0)
    q3 = q.reshape(b, s, SWA_Q_DIM)
    k3 = k.reshape(b, s, SWA_KV_DIM)
    v3 = v.reshape(b, s, SWA_KV_DIM)
    cast_specs = []
    for w in cast:
        rows = w.shape[0] // (b * steps)
        assert rows * b * steps == w.shape[0] and rows % 16 == 0, w.shape
        cast_specs.append(pl.BlockSpec((rows, w.shape[1]), lambda bi, i: (bi * steps + i, 0)))
    outs = pl.pallas_call(
        functools.partial(_swa_kernel, tq=tq, n_cast=len(cast)),
        grid=(b, steps),
        in_specs=[pl.BlockSpec(memory_space=pltpu.SMEM),
                  pl.BlockSpec((None, tq, SWA_Q_DIM), cur),
                  pl.BlockSpec((None, ATT_BLOCK, SWA_KV_DIM), prev),
                  pl.BlockSpec((None, tq, SWA_KV_DIM), cur),
                  pl.BlockSpec((None, ATT_BLOCK, SWA_KV_DIM), prev),
                  pl.BlockSpec((None, tq, SWA_KV_DIM), cur)] + cast_specs,
        out_specs=[pl.BlockSpec((None, tq, SWA_Q_DIM), cur)] + cast_specs,
        out_shape=[jax.ShapeDtypeStruct((b, s, SWA_Q_DIM), BF16)]
                  + [jax.ShapeDtypeStruct(w.shape, BF16) for w in cast],
        compiler_params=_params(("parallel", "parallel")),
        name="swa_attention",
    )(sinks, q3, k3, k3, v3, v3, *cast)
    return outs[0].reshape(b * s, SWA_Q_DIM), outs[1:]


def _pool_mixer(u, halo, seq_row0):
    tm = u.shape[0]
    full = jnp.concatenate([halo, u], axis=0)
    sums = [full]
    for shift in (1, 2, 4, 8):
        prev = sums[-1]
        sums.append(prev + pltpu.roll(prev, shift, 0))
    lane = lax.broadcasted_iota(I32, (tm, POOL_WIDTH), 1)
    row = lax.broadcasted_iota(I32, (tm, POOL_WIDTH), 0)
    grp = lane // POOL_GROUP
    win = sums[4][POOL_HALO:]
    width = jnp.full((tm, POOL_WIDTH), POOL_WINDOWS[3], I32)
    for gi in (2, 1, 0):
        win = jnp.where(grp == gi, sums[gi + 1][POOL_HALO:], win)
        width = jnp.where(grp == gi, POOL_WINDOWS[gi], width)
    count = jnp.minimum(seq_row0 + row + 1, width).astype(F32)
    return win / count - u


def _swiglu_tile(xb, wg, wu, wd):
    g = jnp.dot(xb, wg, preferred_element_type=F32)
    u = jnp.dot(xb, wu, preferred_element_type=F32)
    h = (g * jax.nn.sigmoid(g)) * u
    return jnp.dot(h.astype(BF16), wd, preferred_element_type=F32)


def _layer0_tail_kernel(u_ref, uh_ref, o_ref, x_ref, pw_ref, ps_ref, wa_ref, wb_ref, lw0_ref, lb0_ref,
                        wg_ref, wu_ref, wd_ref, lw1_ref, lb1_ref, out_ref, *, tm):
    i = pl.program_id(1)
    halo = jnp.where(i == 0, 0.0, uh_ref[...])
    d = _pool_mixer(u_ref[...], halo, i * tm)
    a = jnp.dot(d.astype(BF16), pw_ref[...], preferred_element_type=F32) * ps_ref[...]
    mix = jnp.dot(a.astype(BF16), wa_ref[...], preferred_element_type=F32)
    mix = mix + jnp.dot(o_ref[...], wb_ref[...], preferred_element_type=F32)
    x1 = _layer_norm(ALPHA * x_ref[...] + mix, lw0_ref[...], lb0_ref[...])
    ffn = _swiglu_tile(x1.astype(BF16), wg_ref[...], wu_ref[...], wd_ref[...])
    out_ref[...] = _layer_norm(ALPHA * x1 + ffn, lw1_ref[...], lb1_ref[...])


def _layer0_tail(u, attn, x2d, pool_bd, pool_scale, wa, wb, ln0, wg, wu, wd, ln1, b, s, tm=512):
    d = x2d.shape[1]
    f = wg.shape[1]
    per = tm // POOL_HALO
    cur = lambda bi, i: (bi, i, 0)
    prev = lambda bi, i: (bi, jnp.maximum(i * per - 1, 0), 0)
    const = lambda bi, i: (0, 0)
    out = pl.pallas_call(
        functools.partial(_layer0_tail_kernel, tm=tm),
        grid=(b, s // tm),
        in_specs=[pl.BlockSpec((None, tm, POOL_WIDTH), cur),
                  pl.BlockSpec((None, POOL_HALO, POOL_WIDTH), prev),
                  pl.BlockSpec((None, tm, SWA_Q_DIM), cur),
                  pl.BlockSpec((None, tm, d), cur),
                  pl.BlockSpec((POOL_WIDTH, POOL_WIDTH), const),
                  pl.BlockSpec((1, POOL_WIDTH), const),
                  pl.BlockSpec((POOL_WIDTH, d), const),
                  pl.BlockSpec((SWA_Q_DIM, d), const),
                  pl.BlockSpec((1, d), const),
                  pl.BlockSpec((1, d), const),
                  pl.BlockSpec((d, f), const),
                  pl.BlockSpec((d, f), const),
                  pl.BlockSpec((f, d), const),
                  pl.BlockSpec((1, d), const),
                  pl.BlockSpec((1, d), const)],
        out_specs=pl.BlockSpec((None, tm, d), cur),
        out_shape=jax.ShapeDtypeStruct((b, s, d), F32),
        compiler_params=_params(("parallel", "parallel"), vmem=VMEM_LIMIT_BIG),
        name="layer0_tail",
    )(u.reshape(b, s, POOL_WIDTH), u.reshape(b, s, POOL_WIDTH), attn.reshape(b, s, SWA_Q_DIM),
      x2d.reshape(b, s, d), pool_bd, pool_scale, wa, wb, *ln0, wg, wu, wd, *ln1)
    return out.reshape(b * s, d)


def _kv_operands(kv, low):
    swapped = pltpu.roll(kv, HEAD_DIM, 1)
    return (jnp.where(low, kv, 0.0), jnp.where(low, 0.0, swapped),
            jnp.where(low, swapped, 0.0), jnp.where(low, 0.0, kv))


def _inproj1_kernel(x_ref, wq_ref, wkv_ref, wc_ref, cw_ref, cos_ref, sin_ref,
                    q0_ref, q1_ref, q2_ref, kv0_ref, kv1_ref, kv2_ref, d_ref,
                    zc_ref, sq_ref, skv_ref, *, tm):
    i = pl.program_id(1)
    xb = x_ref[...].astype(BF16)
    cos = cos_ref[...]
    sin = sin_ref[...]
    lane = lax.broadcasted_iota(I32, (tm, LANES), 1)
    low = lane < HEAD_DIM
    cos_k = jnp.where(low, cos, 1.0)
    sin_k = jnp.where(low, sin, 0.0)
    q_refs = (q0_ref, q1_ref, q2_ref)
    kv_refs = (kv0_ref, kv1_ref, kv2_ref)
    for gi, (_, dil) in enumerate(DIL_PAIRS):
        q = jnp.dot(xb, wq_ref[gi], preferred_element_type=F32)
        q = _rope(q, cos, sin) * QK_SCALE
        kv = jnp.dot(xb, wkv_ref[gi], preferred_element_type=F32)
        kv = _rope(kv, cos_k, sin_k)
        kvx = [_kv_operands(kv[:, h * LANES:(h + 1) * LANES], low) for h in range(DIL_KV_HEADS)]
        if dil == 1:
            q_refs[gi][0] = q.astype(BF16)
            for h in range(DIL_KV_HEADS):
                kv_refs[gi][h, 0] = jnp.concatenate(kvx[h], axis=1).astype(BF16)
        else:
            n = tm // dil
            for c in range(DIL_Q_DIM // LANES):
                sq_ref[c] = q[:, c * LANES:(c + 1) * LANES]
            for h in range(DIL_KV_HEADS):
                for part in range(KV_PARTS):
                    skv_ref[h * KV_PARTS + part] = kvx[h][part]
            for r in range(dil):
                rows = pl.ds(r, n, stride=dil)
                q_refs[gi][r] = jnp.concatenate(
                    [sq_ref[c, rows, :] for c in range(DIL_Q_DIM // LANES)], axis=1).astype(BF16)
                for h in range(DIL_KV_HEADS):
                    kv_refs[gi][h, r] = jnp.concatenate(
                        [skv_ref[h * KV_PARTS + part, rows, :] for part in range(KV_PARTS)], axis=1).astype(BF16)

    hc = jnp.dot(xb, wc_ref[...], preferred_element_type=F32)
    z = hc[:, 2 * CONV_WIDTH:] * hc[:, :CONV_WIDTH]
    zprev = jnp.where(i == 0, 0.0, zc_ref[...])
    zfull = jnp.concatenate([zprev, z], axis=0)
    z1 = pltpu.roll(zfull, 1, 0)[CONV_HALO:]
    z2 = pltpu.roll(zfull, 2, 0)[CONV_HALO:]
    cw = cw_ref[...]
    y = cw[0:1] * z2 + cw[1:2] * z1 + cw[2:3] * z
    d_ref[...] = (hc[:, CONV_WIDTH:2 * CONV_WIDTH] * y).astype(BF16)
    zc_ref[...] = z[tm - CONV_HALO:]


def _inproj1(x2d, wq, wkv, wc, conv_w, cos_t, sin_t, b, s, tm=512):
    d = x2d.shape[1]
    cur = lambda bi, i: (bi, i, 0)
    row = lambda bi, i: (bi * (s // tm) + i, 0)
    in_specs = [pl.BlockSpec((None, tm, d), cur),
                pl.BlockSpec(wq.shape, lambda bi, i: (0, 0, 0)),
                pl.BlockSpec(wkv.shape, lambda bi, i: (0, 0, 0)),
                pl.BlockSpec(wc.shape, lambda bi, i: (0, 0)),
                pl.BlockSpec(conv_w.shape, lambda bi, i: (0, 0)),
                pl.BlockSpec((tm, LANES), row),
                pl.BlockSpec((tm, LANES), row)]
    out_specs, out_shape = [], []
    for _, dil in DIL_PAIRS:
        out_specs.append(pl.BlockSpec((None, dil, tm // dil, DIL_Q_DIM), lambda bi, i: (bi, 0, i, 0)))
        out_shape.append(jax.ShapeDtypeStruct((b, dil, s // dil, DIL_Q_DIM), BF16))
    for _, dil in DIL_PAIRS:
        out_specs.append(pl.BlockSpec((None, DIL_KV_HEADS, dil, tm // dil, KV_PARTS * LANES),
                                      lambda bi, i: (bi, 0, 0, i, 0)))
        out_shape.append(jax.ShapeDtypeStruct((b, DIL_KV_HEADS, dil, s // dil, KV_PARTS * LANES), BF16))
    out_specs.append(pl.BlockSpec((None, tm, CONV_WIDTH), cur))
    out_shape.append(jax.ShapeDtypeStruct((b, s, CONV_WIDTH), BF16))
    return pl.pallas_call(
        functools.partial(_inproj1_kernel, tm=tm),
        grid=(b, s // tm),
        in_specs=in_specs,
        out_specs=out_specs,
        out_shape=out_shape,
        scratch_shapes=[pltpu.VMEM((CONV_HALO, CONV_WIDTH), F32),
                        pltpu.VMEM((DIL_Q_DIM // LANES, tm, LANES), F32),
                        pltpu.VMEM((DIL_KV_HEADS * KV_PARTS, tm, LANES), F32)],
        compiler_params=_params(("parallel", "arbitrary")),
        name="inproj1",
    )(x2d.reshape(b, s, d), wq, wkv, wc, conv_w, cos_t, sin_t)


def _dil_kernel(q0_ref, q1_ref, q2_ref, kv0_ref, kv1_ref, kv2_ref, out_ref, o_run, l_run, bias_ref, *, s):
    q_refs = (q0_ref, q1_ref, q2_ref)
    kv_refs = (kv0_ref, kv1_ref, kv2_ref)
    n_blocks = s // ATT_BLOCK
    width = DIL_Q_PER_KV * HEAD_DIM
    n_chunks = width // LANES
    max_dist = DIL_PAIRS[0][0] // DIL_PAIRS[0][1]
    assert all(w // d == max_dist for w, d in DIL_PAIRS)
    stacked = n_chunks * ATT_BLOCK
    bias_ref[0] = _band_bias(stacked, max_dist, 0)
    bias_ref[1] = _band_bias(stacked, max_dist, ATT_BLOCK)
    for gi, (_, dil) in enumerate(DIL_PAIRS):
        q_ref, kv_ref = q_refs[gi], kv_refs[gi]
        blocks_per_sub = (s // dil) // ATT_BLOCK
        shift = blocks_per_sub.bit_length() - 1

        def body(n, carry, q_ref=q_ref, kv_ref=kv_ref, gi=gi, dil=dil, blocks_per_sub=blocks_per_sub, shift=shift):
            res = lax.shift_right_logical(n, shift)
            bi = n & (blocks_per_sub - 1)
            rows = pl.ds(pl.multiple_of(n * ATT_BLOCK, ATT_BLOCK), ATT_BLOCK)
            prow = pl.ds(pl.multiple_of(jnp.maximum(n - 1, 0) * ATT_BLOCK, ATT_BLOCK), ATT_BLOCK)
            kvx = jnp.concatenate([kv_ref[prow, :], kv_ref[rows, :]], axis=0)
            kz, zk, vz, zv = _kv_parts(kvx)
            q4 = q_ref[rows, :]
            qs = jnp.concatenate([q4[:, c * LANES:(c + 1) * LANES] for c in range(n_chunks)], axis=0)
            bias = bias_ref[jnp.where(bi == 0, 1, 0)]
            oa, ma, da = _attn_stack(qs, kz, vz, bias)
            ob, mb, db = _attn_stack(qs, zk, zv, bias)
            low = lax.broadcasted_iota(I32, (n_chunks * ATT_BLOCK, LANES), 1) < HEAD_DIM
            o_st = (oa + ob) * jnp.where(low, 1.0 / da, 1.0 / db)
            l_st = jnp.where(low, ma + jnp.log(da), mb + jnp.log(db))
            start = res + dil * ATT_BLOCK * bi
            if dil == 1:
                tok = pl.ds(pl.multiple_of(start, ATT_BLOCK), ATT_BLOCK)
            else:
                tok = pl.ds(start, ATT_BLOCK, stride=dil)
            for c in range(n_chunks):
                o_c = o_st[c * ATT_BLOCK:(c + 1) * ATT_BLOCK]
                l_c = l_st[c * ATT_BLOCK:(c + 1) * ATT_BLOCK]
                if gi == 0:
                    o_run[c, tok, :] = o_c
                    l_run[c, tok, :] = l_c
                else:
                    o_old = o_run[c, tok, :]
                    l_old = l_run[c, tok, :]
                    m = jnp.maximum(l_old, l_c)
                    a = jnp.exp(l_old - m)
                    bb = jnp.exp(l_c - m)
                    tot = a + bb
                    o_run[c, tok, :] = (o_old * a + o_c * bb) * (1.0 / tot)
                    if gi < len(DIL_PAIRS) - 1:
                        l_run[c, tok, :] = m + jnp.log(tot)
            return carry

        lax.fori_loop(0, n_blocks, body, 0, unroll=8)
    for c in range(n_chunks):
        out_ref[:, c * LANES:(c + 1) * LANES] = o_run[c].astype(BF16)


def _dilated(qs, kvs, b, s):
    width = DIL_Q_PER_KV * HEAD_DIM
    in_specs = [pl.BlockSpec((None, s, width), lambda bi, h: (bi, 0, h)) for _ in DIL_PAIRS]
    in_specs += [pl.BlockSpec((None, None, s, KV_PARTS * LANES), lambda bi, h: (bi, h, 0, 0)) for _ in DIL_PAIRS]
    qs = [q.reshape(b, s, DIL_Q_DIM) for q in qs]
    kvs = [kv.reshape(b, DIL_KV_HEADS, s, KV_PARTS * LANES) for kv in kvs]
    out = pl.pallas_call(
        functools.partial(_dil_kernel, s=s),
        grid=(b, DIL_KV_HEADS),
        in_specs=in_specs,
        out_specs=pl.BlockSpec((None, s, width), lambda bi, h: (bi, 0, h)),
        out_shape=jax.ShapeDtypeStruct((b, s, DIL_Q_DIM), BF16),
        scratch_shapes=[pltpu.VMEM((width // LANES, s, LANES), F32),
                        pltpu.VMEM((width // LANES, s, LANES), F32),
                        pltpu.VMEM((2, (width // LANES) * ATT_BLOCK, 2 * ATT_BLOCK), F32)],
        compiler_params=_params(("parallel", "parallel"), vmem=VMEM_LIMIT_BIG),
        name="dilated_attention",
    )(*qs, *kvs)
    return out.reshape(b * s, DIL_Q_DIM)


def _to_slabs(ref, val, base=0):
    rows, d = val.shape
    per = d // LANES
    for c in range(per):
        ref[pl.ds(base + c, rows, stride=per), :] = val[:, c * LANES:(c + 1) * LANES]


def _from_slabs(ref, rows, base=0, per=ROW_SLAB):
    return jnp.concatenate([ref[pl.ds(base + c, rows, stride=per), :] for c in range(per)], axis=1)


def _route_tile(x, wr, e_ref, g_ref, r_ref, cnt_ref, run_ref, tm, col0):
    cols = slice(col0, col0 + tm)
    logits = lax.dot_general(wr, x, (((1,), (1,)), ((), ())),
                             precision=lax.Precision.HIGHEST, preferred_element_type=F32)
    eid = lax.broadcasted_iota(I32, (N_EXPERTS, tm), 0)
    m1 = jnp.max(logits, axis=0, keepdims=True)
    i1 = jnp.min(jnp.where(logits == m1, eid, N_EXPERTS), axis=0, keepdims=True)
    rest = jnp.where(eid == i1, -jnp.inf, logits)
    m2 = jnp.max(rest, axis=0, keepdims=True)
    i2 = jnp.min(jnp.where(rest == m2, eid, N_EXPERTS), axis=0, keepdims=True)
    t2 = jnp.exp(m2 - m1)
    g1 = 1.0 / (1.0 + t2)
    g2 = t2 / (1.0 + t2)
    oh1 = eid == i1
    oh2 = eid == i2
    oh = jnp.where(oh1 | oh2, 1.0, 0.0)
    ri = lax.broadcasted_iota(I32, (tm, tm), 0)
    ci = lax.broadcasted_iota(I32, (tm, tm), 1)
    tri = jnp.where(ri <= ci, 1.0, 0.0).astype(BF16)
    incl = jnp.dot(oh.astype(BF16), tri, preferred_element_type=F32)
    rank = run_ref[:, 0:1] + incl - oh
    r1 = jnp.sum(jnp.where(oh1, rank, 0.0), axis=0, keepdims=True)
    r2 = jnp.sum(jnp.where(oh2, rank, 0.0), axis=0, keepdims=True)
    e_ref[0:1, cols] = i1
    e_ref[1:2, cols] = i2
    g_ref[0:1, cols] = g1
    g_ref[1:2, cols] = g2
    r_ref[0:1, cols] = r1.astype(I32)
    r_ref[1:2, cols] = r2.astype(I32)
    run_ref[...] = run_ref[...] + incl[:, tm - 1:tm]
    cnt_ref[...] = run_ref[...]


def _outproj1_kernel(c_ref, d_ref, x_ref, wa_ref, wb_ref, lw_ref, lb_ref, wr_ref,
                     slab_ref, e_ref, g_ref, r_ref, cnt_ref, run_ref, *, tm, sub):
    @pl.when(pl.program_id(0) == 0)
    def _():
        run_ref[...] = jnp.zeros_like(run_ref)

    for h in range(tm // sub):
        rows = slice(h * sub, (h + 1) * sub)
        mix = jnp.dot(c_ref[rows, :], wa_ref[...], preferred_element_type=F32)
        mix = mix + jnp.dot(d_ref[rows, :], wb_ref[...], preferred_element_type=F32)
        y = _layer_norm(ALPHA * x_ref[rows, :] + mix, lw_ref[...], lb_ref[...])
        _to_slabs(slab_ref, y, base=h * sub * ROW_SLAB)
        _route_tile(y, wr_ref[...], e_ref, g_ref, r_ref, cnt_ref, run_ref, sub, h * sub)


def _outproj1(c, dconv, x2d, wa, wb, ln_w, ln_b, wr_t, tm=1024, sub=512):
    t, d = x2d.shape
    row = lambda i: (i, 0)
    col = lambda i: (0, i)
    const = lambda i: (0, 0)
    return pl.pallas_call(
        functools.partial(_outproj1_kernel, tm=tm, sub=sub),
        grid=(t // tm,),
        in_specs=[pl.BlockSpec((tm, DIL_Q_DIM), row),
                  pl.BlockSpec((tm, CONV_WIDTH), row),
                  pl.BlockSpec((tm, d), row),
                  pl.BlockSpec((DIL_Q_DIM, d), const),
                  pl.BlockSpec((CONV_WIDTH, d), const),
                  pl.BlockSpec((1, d), const),
                  pl.BlockSpec((1, d), const),
                  pl.BlockSpec((N_EXPERTS, d), const)],
        out_specs=[pl.BlockSpec((tm * (d // LANES), LANES), row),
                   pl.BlockSpec((2, tm), col), pl.BlockSpec((2, tm), col), pl.BlockSpec((2, tm), col),
                   pl.BlockSpec((N_EXPERTS, LANES), const)],
        out_shape=[jax.ShapeDtypeStruct((t * (d // LANES), LANES), F32),
                   jax.ShapeDtypeStruct((2, t), I32), jax.ShapeDtypeStruct((2, t), F32),
                   jax.ShapeDtypeStruct((2, t), I32), jax.ShapeDtypeStruct((N_EXPERTS, LANES), F32)],
        scratch_shapes=[pltpu.VMEM((N_EXPERTS, LANES), F32)],
        compiler_params=_params(("arbitrary",)),
        name="outproj1_route",
    )(c, dconv, x2d, wa, wb, ln_w, ln_b, wr_t)


def _slab_copy(src_ref, dst_ref, sem, src_row, dst_row, n=1):
    src = src_ref.at[pl.ds(pl.multiple_of(src_row * ROW_SLAB, ROW_SLAB), n * ROW_SLAB)]
    dst = dst_ref.at[pl.ds(pl.multiple_of(dst_row * ROW_SLAB, ROW_SLAB), n * ROW_SLAB)]
    return pltpu.make_async_copy(src, dst, sem)


def _moe_kernel(be_ref, nv_ref, tok_ref, tokn_ref, dstp_ref, dstc_ref, x_hbm, wg_ref, wu_ref, wd_ref, y_hbm,
                xs_ref, stage_ref, sem_in, sem_out, *, tm, dump_row0):
    i = pl.program_id(0)
    n_valid = nv_ref[0]
    slot = i & 1

    def row_in(tok, dst_slot, r):
        return _slab_copy(x_hbm, xs_ref, sem_in.at[dst_slot], tok, dst_slot * tm + r)

    def row_out(src_slot, r, dst):
        return _slab_copy(stage_ref, y_hbm, sem_out.at[src_slot], src_slot * tm + r, dst)

    @pl.when(i < n_valid)
    def _():
        @pl.when(i == 0)
        def _():
            def start(r, c):
                row_in(tok_ref[0, 0, r], 0, r).start()
                return c
            lax.fori_loop(0, tm, start, 0)
            stage_ref[...] = jnp.zeros_like(stage_ref)
            fill = _slab_copy(stage_ref, y_hbm, sem_out.at[0], 0, dump_row0, n=2 * tm)
            fill.start()
            fill.wait()

        _slab_copy(x_hbm, xs_ref, sem_in.at[slot], 0, slot * tm, n=tm).wait()
        xb = _from_slabs(xs_ref, tm, base=slot * (tm * ROW_SLAB)).astype(BF16)

        for r in range(tm):
            row_in(tokn_ref[0, 0, r], 1 - slot, r).start(priority=1)
            row_out(1 - slot, r, dstp_ref[0, 0, r]).start(priority=1)

        part = _swiglu_tile(xb, wg_ref[...], wu_ref[...], wd_ref[...])
        _slab_copy(stage_ref, y_hbm, sem_out.at[1 - slot], (1 - slot) * tm, 0, n=tm).wait()
        _to_slabs(stage_ref, part, base=slot * (tm * ROW_SLAB))

        @pl.when(i + 1 >= n_valid)
        def _():
            _slab_copy(x_hbm, xs_ref, sem_in.at[1 - slot], 0, (1 - slot) * tm, n=tm).wait()

            def start(r, c):
                row_out(slot, r, dstc_ref[0, 0, r]).start()
                return c
            lax.fori_loop(0, tm, start, 0)
            _slab_copy(stage_ref, y_hbm, sem_out.at[slot], slot * tm, 0, n=tm).wait()


def _moe_experts(x_slabs, row_tok, row_dst, dump_row0, block_e, n_valid, wg, wu, wd, tm):
    d = wg.shape[1]
    n_rows = row_tok.shape[0]
    n_blocks = n_rows // tm
    f = wg.shape[2]
    y_rows = dump_row0 + 2 * tm

    def w_map(i, be, nv):
        return be[jnp.minimum(i, nv[0] - 1)], 0, 0

    def smem(index_map):
        return pl.BlockSpec((1, 1, tm), index_map, memory_space=pltpu.SMEM)

    resident = dict(pipeline_mode=pl.Buffered(1))
    grid_spec = pltpu.PrefetchScalarGridSpec(
        num_scalar_prefetch=2,
        grid=(n_blocks,),
        in_specs=[smem(lambda i, be, nv: (i, 0, 0)),
                  smem(lambda i, be, nv: (jnp.minimum(i + 1, n_blocks - 1), 0, 0)),
                  smem(lambda i, be, nv: (jnp.where(i == 0, n_blocks, i - 1), 0, 0)),
                  smem(lambda i, be, nv: (i, 0, 0)),
                  pl.BlockSpec(memory_space=pl.ANY),
                  pl.BlockSpec((None, d, f), w_map, **resident),
                  pl.BlockSpec((None, d, f), w_map, **resident),
                  pl.BlockSpec((None, f, d), w_map, **resident)],
        out_specs=pl.BlockSpec(memory_space=pl.ANY),
        scratch_shapes=[pltpu.VMEM((2 * tm * ROW_SLAB, LANES), F32),
                        pltpu.VMEM((2 * tm * ROW_SLAB, LANES), F32),
                        pltpu.SemaphoreType.DMA((2,)),
                        pltpu.SemaphoreType.DMA((2,))],
    )
    tok3 = row_tok.reshape(n_blocks, 1, tm)
    dst3 = row_dst.reshape(n_blocks + 1, 1, tm)
    return pl.pallas_call(
        functools.partial(_moe_kernel, tm=tm, dump_row0=dump_row0),
        grid_spec=grid_spec,
        out_shape=jax.ShapeDtypeStruct((y_rows * ROW_SLAB, LANES), F32),
        compiler_params=_params(("arbitrary",), vmem=VMEM_LIMIT_BIG),
        name="moe_experts",
    )(block_e, n_valid, tok3, tok3, dst3, dst3, x_slabs, wg, wu, wd)


def _combine_kernel(ya_ref, yb_ref, x_ref, g1_ref, g2_ref, lw_ref, lb_ref, out_ref, *, tm):
    ffn = _from_slabs(ya_ref, tm) * g1_ref[...] + _from_slabs(yb_ref, tm) * g2_ref[...]
    out_ref[...] = _layer_norm(ALPHA * _from_slabs(x_ref, tm) + ffn, lw_ref[...], lb_ref[...])


def _combine(y, x_slabs, g1, g2, ln_w, ln_b, tm=512):
    t = g1.shape[0]
    d = ln_w.shape[1]
    row = lambda i: (i, 0)
    return pl.pallas_call(
        functools.partial(_combine_kernel, tm=tm),
        grid=(t // tm,),
        in_specs=[pl.BlockSpec((tm * ROW_SLAB, LANES), row),
                  pl.BlockSpec((tm * ROW_SLAB, LANES), lambda i: (t // tm + i, 0)),
                  pl.BlockSpec((tm * ROW_SLAB, LANES), row),
                  pl.BlockSpec((tm, 1), row),
                  pl.BlockSpec((tm, 1), row),
                  pl.BlockSpec((1, d), lambda i: (0, 0)),
                  pl.BlockSpec((1, d), lambda i: (0, 0))],
        out_specs=pl.BlockSpec((tm, d), row),
        out_shape=jax.ShapeDtypeStruct((t, d), F32),
        compiler_params=_params(("parallel",)),
        name="moe_combine",
    )(y, y, x_slabs, g1.reshape(t, 1), g2.reshape(t, 1), ln_w, ln_b)


SC_CORES = 2
SC_SUBCORES = 16
SC_LANES = 16
SC_CHUNK = 8192


def _invert_rows(idx, vals, default):
    n_pairs = idx.shape[0]
    n_out = default.shape[0]
    n_workers = SC_CORES * SC_SUBCORES
    per = n_out // n_workers
    assert per * n_workers == n_out and per % 8 == 0 and n_pairs % SC_CHUNK == 0
    mesh = plsc.VectorSubcoreMesh(core_axis_name="c", subcore_axis_name="s",
                                  num_cores=SC_CORES, num_subcores=SC_SUBCORES)

    def body(idx_hbm, val_hbm, dflt_hbm, out_hbm, idx_v, val_v, loc_v):
        wid = lax.axis_index("s") * SC_CORES + lax.axis_index("c")
        lo = wid * per
        pltpu.sync_copy(dflt_hbm.at[pl.ds(lo, per)], loc_v)

        def chunk(c, carry):
            pltpu.sync_copy(idx_hbm.at[pl.ds(c * SC_CHUNK, SC_CHUNK)], idx_v)
            pltpu.sync_copy(val_hbm.at[pl.ds(c * SC_CHUNK, SC_CHUNK)], val_v)

            def step(j, carry):
                rel = idx_v[pl.ds(j * SC_LANES, SC_LANES)] - lo
                mine = (rel >= 0) & (rel < per)
                plsc.store_scatter(loc_v, [jnp.where(mine, rel, 0)], val_v[pl.ds(j * SC_LANES, SC_LANES)],
                                   mask=mine)
                return carry

            return lax.fori_loop(0, SC_CHUNK // SC_LANES, step, carry)

        lax.fori_loop(0, n_pairs // SC_CHUNK, chunk, 0)
        pltpu.sync_copy(loc_v, out_hbm.at[pl.ds(lo, per)])

    return pl.kernel(
        body,
        out_type=jax.ShapeDtypeStruct((n_out,), I32),
        mesh=mesh,
        scratch_types=[pltpu.VMEM((SC_CHUNK,), I32), pltpu.VMEM((SC_CHUNK,), I32), pltpu.VMEM((per,), I32)],
        compiler_params=pltpu.CompilerParams(needs_layout_passes=False),
        name="moe_invert_rows",
    )(idx, vals, default)


def _moe_layer(x_slabs, e_sel, gates, ranks, counts, wg, wu, wd, ln_w, ln_b, tmoe=512):
    t = e_sel.shape[1]
    counts = counts[:, 0].astype(I32)
    padded = (counts + tmoe - 1) // tmoe * tmoe
    pend = jnp.cumsum(padded)
    pstart = pend - padded
    n_blocks = -(-(2 * t + N_EXPERTS * (tmoe - 1)) // tmoe)
    n_rows = n_blocks * tmoe
    first_row = sum(jnp.where(e_sel == e, pstart[e], 0) for e in range(N_EXPERTS))
    dest = first_row + ranks
    dump_row0 = 2 * t
    r = jnp.arange(n_rows, dtype=I32)
    dump = dump_row0 + ((r // tmoe) & 1) * tmoe + r % tmoe
    tok = jnp.arange(t, dtype=I32)
    row_dst = _invert_rows(dest.reshape(-1), jnp.concatenate([tok, t + tok]), dump)
    row_tok = jnp.where(row_dst < dump_row0, row_dst % t, 0)
    row_dst = jnp.concatenate([row_dst, dump_row0 + tmoe + jnp.arange(tmoe, dtype=I32)])
    block_row0 = jnp.arange(n_blocks, dtype=I32) * tmoe
    block_e = jnp.minimum(jnp.sum((block_row0[:, None] >= pend[None, :]).astype(I32), axis=1), N_EXPERTS - 1)
    n_valid = (pend[-1:] // tmoe).astype(I32)
    y = _moe_experts(x_slabs, row_tok, row_dst, dump_row0, block_e, n_valid, wg, wu, wd, tmoe)
    return _combine(y, x_slabs, gates[0], gates[1], ln_w, ln_b)


def kernel(x, positions, ln_w, ln_b, even_w_in, pool_w, pool_scale, swa_sinks, even_w_out, ffn_w_gate, ffn_w_up, ffn_w_down, odd_w_in, conv_w, odd_w_out, router_w, moe_w_gate, moe_w_up, moe_w_down):
    b, s, d = x.shape
    t = b * s
    x2d = x.reshape(t, d)
    ln = lambda layer, k: (ln_w[layer, k].reshape(1, d), ln_b[layer, k].reshape(1, d))

    u, q, k, v, cos_t, sin_t = _inproj0(x2d, even_w_in[0].astype(BF16), positions)
    moe_w = (moe_w_gate[0], moe_w_up[0], moe_w_down[0])
    attn, moe_bf = _swa(q, k, v, swa_sinks[0], b, s, cast=[w.reshape(-1, w.shape[-1]) for w in moe_w])
    moe_bf = [wb.reshape(w.shape) for wb, w in zip(moe_bf, moe_w)]
    pool_bd = jax.scipy.linalg.block_diag(*[pool_w[0, gi] for gi in range(len(POOL_WINDOWS))]).astype(BF16)
    w_out0 = even_w_out[0].astype(BF16)
    x2d = _layer0_tail(u, attn, x2d, pool_bd, pool_scale[0].reshape(1, POOL_WIDTH),
                       w_out0[:POOL_WIDTH], w_out0[POOL_WIDTH:], ln(0, 0),
                       ffn_w_gate[0].astype(BF16), ffn_w_up[0].astype(BF16), ffn_w_down[0].astype(BF16),
                       ln(0, 1), b, s)

    w_in1 = odd_w_in[0].astype(BF16)
    c_in = len(DIL_PAIRS) * DIL_GROUP_IN
    wq, wkv = [], []
    for gi in range(len(DIL_PAIRS)):
        g0 = gi * DIL_GROUP_IN
        wq.append(w_in1[:, g0:g0 + DIL_Q_DIM])
        kcol = g0 + DIL_Q_DIM
        vcol = kcol + DIL_KV_DIM
        parts = []
        for h in range(DIL_KV_HEADS):
            parts += [w_in1[:, kcol + h * HEAD_DIM:kcol + (h + 1) * HEAD_DIM],
                      w_in1[:, vcol + h * HEAD_DIM:vcol + (h + 1) * HEAD_DIM]]
        wkv.append(jnp.concatenate(parts, axis=1))
    outs = _inproj1(x2d, jnp.stack(wq), jnp.stack(wkv), w_in1[:, c_in:], conv_w[0], cos_t, sin_t, b, s)
    c_out = _dilated(outs[0:3], outs[3:6], b, s)
    w_out1 = odd_w_out[0].astype(BF16)
    x_slabs, e_sel, gates, ranks, counts = _outproj1(
        c_out, outs[6].reshape(t, CONV_WIDTH), x2d, w_out1[:DIL_Q_DIM], w_out1[DIL_Q_DIM:], *ln(1, 0),
        router_w[0].T)
    x2d = _moe_layer(x_slabs, e_sel, gates, ranks, counts, *moe_bf, *ln(1, 1))
    return x2d.reshape(b, s, d)
```

```python
import functools
import math

import jax
import jax.numpy as jnp
from jax import lax
from jax.experimental import pallas as pl
from jax.experimental.pallas import tpu as pltpu
from jax.experimental.pallas import tpu_sc as plsc

F32 = jnp.float32
BF16 = jnp.bfloat16
I32 = jnp.int32

HEAD_DIM = 64
ROPE_THETA = 10000.0
ATT_BLOCK = 128
LN_EPS = 1e-5
POOL_WINDOWS = (2, 4, 8, 16)
POOL_GROUP = 64
POOL_WIDTH = 256
POOL_HALO = 16
SWA_WINDOW = 128
SWA_Q_HEADS = 12
SWA_KV_HEADS = 4
SWA_Q_DIM = SWA_Q_HEADS * HEAD_DIM
SWA_KV_DIM = SWA_KV_HEADS * HEAD_DIM
DIL_PAIRS = ((128, 1), (512, 4), (2048, 16))
DIL_Q_HEADS = 8
DIL_KV_HEADS = 2
DIL_Q_DIM = DIL_Q_HEADS * HEAD_DIM
DIL_KV_DIM = DIL_KV_HEADS * HEAD_DIM
DIL_GROUP_IN = DIL_Q_DIM + 2 * DIL_KV_DIM
DIL_Q_PER_KV = DIL_Q_HEADS // DIL_KV_HEADS
CONV_WIDTH = 512
CONV_K = 3
CONV_HALO = 8
N_EXPERTS = 8
DEPTH = 2
ALPHA = (2 * DEPTH) ** 0.25
QK_SCALE = 1.0 / math.sqrt(HEAD_DIM)
NEG_BIG = -1e30
KV_PARTS = 2

LANES = 128
ROW_SLAB = 8
VMEM_LIMIT = 48 * 1024 * 1024
VMEM_LIMIT_BIG = 56 * 1024 * 1024


def _params(sem, vmem=VMEM_LIMIT):
    return pltpu.CompilerParams(dimension_semantics=sem, vmem_limit_bytes=vmem)


def _layer_norm(y, w, b):
    mu = jnp.mean(y, axis=-1, keepdims=True)
    yc = y - mu
    var = jnp.mean(yc * yc, axis=-1, keepdims=True)
    return yc * lax.rsqrt(var + LN_EPS) * w + b


def _rope_chunk(xc, cos, sin_signed, first_half):
    rot = jnp.where(first_half, pltpu.roll(xc, 96, 1), pltpu.roll(xc, 32, 1))
    return xc * cos + rot * sin_signed


def _rope(x, cos, sin_signed):
    tm, c = x.shape
    lane = lax.broadcasted_iota(I32, (tm, LANES), 1)
    first_half = (lane & 32) == 0
    chunks = [_rope_chunk(x[:, i * LANES:(i + 1) * LANES], cos, sin_signed, first_half)
              for i in range(c // LANES)]
    return chunks[0] if len(chunks) == 1 else jnp.concatenate(chunks, axis=1)


def _rope_tables(pos, inv):
    ang = inv * pos.astype(F32)
    c = jnp.cos(ang)
    s = jnp.sin(ang)
    c4 = jnp.concatenate([c, c, c, c], axis=0)
    s4 = jnp.concatenate([-s, s, -s, s], axis=0)
    return c4.T, s4.T


def _inproj0_kernel(x_ref, w_ref, pos_ref, inv_ref, u_ref, q_ref, k_ref, v_ref, cos_ref, sin_ref):
    xb = x_ref[...].astype(BF16)
    cos, sin = _rope_tables(pos_ref[...], inv_ref[...])
    cos_ref[...] = cos
    sin_ref[...] = sin
    q0 = POOL_WIDTH
    k0 = q0 + SWA_Q_DIM
    v0 = k0 + SWA_KV_DIM
    u_ref[...] = jnp.dot(xb, w_ref[:, :q0], preferred_element_type=F32)
    q = jnp.dot(xb, w_ref[:, q0:k0], preferred_element_type=F32)
    q_ref[...] = (_rope(q, cos, sin) * QK_SCALE).astype(BF16)
    k = jnp.dot(xb, w_ref[:, k0:v0], preferred_element_type=F32)
    k_ref[...] = _rope(k, cos, sin).astype(BF16)
    v_ref[...] = jnp.dot(xb, w_ref[:, v0:], preferred_element_type=F32).astype(BF16)


def _inproj0(x2d, w_bf, positions, tm=512):
    t, d = x2d.shape
    n_in = w_bf.shape[1]
    half = HEAD_DIM // 2
    inv = ROPE_THETA ** (-jnp.arange(half, dtype=F32) / half)
    row = lambda i: (i, 0)
    return pl.pallas_call(
        _inproj0_kernel,
        grid=(t // tm,),
        in_specs=[pl.BlockSpec((tm, d), row),
                  pl.BlockSpec((d, n_in), lambda i: (0, 0)),
                  pl.BlockSpec((1, tm), lambda i: (0, i)),
                  pl.BlockSpec((half, 1), lambda i: (0, 0))],
        out_specs=[pl.BlockSpec((tm, POOL_WIDTH), row),
                   pl.BlockSpec((tm, SWA_Q_DIM), row),
                   pl.BlockSpec((tm, SWA_KV_DIM), row),
                   pl.BlockSpec((tm, SWA_KV_DIM), row),
                   pl.BlockSpec((tm, LANES), row),
                   pl.BlockSpec((tm, LANES), row)],
        out_shape=[jax.ShapeDtypeStruct((t, POOL_WIDTH), F32),
                   jax.ShapeDtypeStruct((t, SWA_Q_DIM), BF16),
                   jax.ShapeDtypeStruct((t, SWA_KV_DIM), BF16),
                   jax.ShapeDtypeStruct((t, SWA_KV_DIM), BF16),
                   jax.ShapeDtypeStruct((t, LANES), F32),
                   jax.ShapeDtypeStruct((t, LANES), F32)],
        compiler_params=_params(("parallel",)),
        name="inproj0",
    )(x2d, w_bf, positions.reshape(1, t), inv.reshape(half, 1))


def _band_bias(rows, max_dist, key_lo):
    qi = lax.broadcasted_iota(I32, (rows, 2 * ATT_BLOCK), 0) & (ATT_BLOCK - 1)
    sj = lax.broadcasted_iota(I32, (rows, 2 * ATT_BLOCK), 1)
    dist = ATT_BLOCK + qi - sj
    return jnp.where((dist >= 0) & (dist <= max_dist) & (sj >= key_lo), 0.0, NEG_BIG)


def _band_mask(max_dist, key_lo):
    qi = lax.broadcasted_iota(I32, (ATT_BLOCK, 2 * ATT_BLOCK), 0)
    sj = lax.broadcasted_iota(I32, (ATT_BLOCK, 2 * ATT_BLOCK), 1)
    dist = ATT_BLOCK + qi - sj
    return (dist >= 0) & (dist <= max_dist) & (sj >= key_lo)


def _attn_block(q, kwin, vwin, valid, sink):
    s = lax.dot_general(q, kwin, (((1,), (1,)), ((), ())), preferred_element_type=F32)
    s = jnp.where(valid, s, NEG_BIG)
    m = jnp.maximum(jnp.max(s, axis=1, keepdims=True), sink)
    p = jnp.exp(s - m)
    den = jnp.sum(p, axis=1, keepdims=True) + jnp.exp(sink - m)
    o = jnp.dot(p.astype(BF16), vwin, preferred_element_type=F32)
    return o * (1.0 / den)


def _attn_stack(qs, kpart, vpart, bias):
    s = lax.dot_general(qs, kpart, (((1,), (1,)), ((), ())), preferred_element_type=F32) + bias
    m = jnp.max(s, axis=1, keepdims=True)
    p = jnp.exp(s - m)
    den = jnp.sum(p, axis=1, keepdims=True)
    return jnp.dot(p.astype(BF16), vpart, preferred_element_type=F32), m, den


def _kv_parts(kvx, h=0):
    base = h * KV_PARTS * LANES
    return [kvx[:, base + part * LANES:base + (part + 1) * LANES] for part in range(KV_PARTS)]


def _swa_kernel(sink_ref, q_ref, kp_ref, kc_ref, vp_ref, vc_ref, *rest, tq, n_cast):
    o_ref = rest[n_cast]
    for src, dst in zip(rest[:n_cast], rest[n_cast + 1:]):
        dst[...] = src[...].astype(BF16)
    i = pl.program_id(1)
    kfull = jnp.concatenate([kp_ref[...], kc_ref[...]], axis=0)
    vfull = jnp.concatenate([vp_ref[...], vc_ref[...]], axis=0)
    g = SWA_Q_HEADS // SWA_KV_HEADS
    for j in range(tq // ATT_BLOCK):
        key_lo = jnp.where(i == 0, ATT_BLOCK, 0) if j == 0 else 0
        valid = _band_mask(SWA_WINDOW - 1, key_lo)
        r0 = j * ATT_BLOCK
        outs = []
        for h in range(SWA_Q_HEADS):
            kv = h // g
            q = q_ref[r0:r0 + ATT_BLOCK, h * HEAD_DIM:(h + 1) * HEAD_DIM]
            kwin = kfull[r0:r0 + 2 * ATT_BLOCK, kv * HEAD_DIM:(kv + 1) * HEAD_DIM]
            vwin = vfull[r0:r0 + 2 * ATT_BLOCK, kv * HEAD_DIM:(kv + 1) * HEAD_DIM]
            outs.append(_attn_block(q, kwin, vwin, valid, sink_ref[h]))
        o_ref[r0:r0 + ATT_BLOCK, :] = jnp.concatenate(outs, axis=1).astype(BF16)


def _swa(q, k, v, sinks, b, s, cast=(), tq=256):
    per = tq // ATT_BLOCK
    steps = s // tq
    cur = lambda bi, i: (bi, i, 0)
    prev = lambda bi, i: (bi, jnp.maximum(i * per - 1, 0), 0)
    q3 = q.reshape(b, s, SWA_Q_DIM)
    k3 = k.reshape(b, s, SWA_KV_DIM)
    v3 = v.reshape(b, s, SWA_KV_DIM)
    cast_specs = []
    for w in cast:
        rows = w.shape[0] // (b * steps)
        assert rows * b * steps == w.shape[0] and rows % 16 == 0, w.shape
        cast_specs.append(pl.BlockSpec((rows, w.shape[1]), lambda bi, i: (bi * steps + i, 0)))
    outs = pl.pallas_call(
        functools.partial(_swa_kernel, tq=tq, n_cast=len(cast)),
        grid=(b, steps),
        in_specs=[pl.BlockSpec(memory_space=pltpu.SMEM),
                  pl.BlockSpec((None, tq, SWA_Q_DIM), cur),
                  pl.BlockSpec((None, ATT_BLOCK, SWA_KV_DIM), prev),
                  pl.BlockSpec((None, tq, SWA_KV_DIM), cur),
                  pl.BlockSpec((None, ATT_BLOCK, SWA_KV_DIM), prev),
                  pl.BlockSpec((None, tq, SWA_KV_DIM), cur)] + cast_specs,
        out_specs=[pl.BlockSpec((None, tq, SWA_Q_DIM), cur)] + cast_specs,
        out_shape=[jax.ShapeDtypeStruct((b, s, SWA_Q_DIM), BF16)]
                  + [jax.ShapeDtypeStruct(w.shape, BF16) for w in cast],
        compiler_params=_params(("parallel", "parallel")),
        name="swa_attention",
    )(sinks, q3, k3, k3, v3, v3, *cast)
    return outs[0].reshape(b * s, SWA_Q_DIM), outs[1:]


def _pool_mixer(u, halo, seq_row0):
    tm = u.shape[0]
    full = jnp.concatenate([halo, u], axis=0)
    sums = [full]
    for shift in (1, 2, 4, 8):
        prev = sums[-1]
        sums.append(prev + pltpu.roll(prev, shift, 0))
    lane = lax.broadcasted_iota(I32, (tm, POOL_WIDTH), 1)
    row = lax.broadcasted_iota(I32, (tm, POOL_WIDTH), 0)
    grp = lane // POOL_GROUP
    win = sums[4][POOL_HALO:]
    width = jnp.full((tm, POOL_WIDTH), POOL_WINDOWS[3], I32)
    for gi in (2, 1, 0):
        win = jnp.where(grp == gi, sums[gi + 1][POOL_HALO:], win)
        width = jnp.where(grp == gi, POOL_WINDOWS[gi], width)
    count = jnp.minimum(seq_row0 + row + 1, width).astype(F32)
    return win / count - u


def _swiglu_tile(xb, wg, wu, wd):
    g = jnp.dot(xb, wg, preferred_element_type=F32)
    u = jnp.dot(xb, wu, preferred_element_type=F32)
    h = (g * jax.nn.sigmoid(g)) * u
    return jnp.dot(h.astype(BF16), wd, preferred_element_type=F32)


def _layer0_tail_kernel(u_ref, uh_ref, o_ref, x_ref, pw_ref, ps_ref, wa_ref, wb_ref, lw0_ref, lb0_ref,
                        wg_ref, wu_ref, wd_ref, lw1_ref, lb1_ref, out_ref, *, tm):
    i = pl.program_id(1)
    halo = jnp.where(i == 0, 0.0, uh_ref[...])
    d = _pool_mixer(u_ref[...], halo, i * tm)
    a = jnp.dot(d.astype(BF16), pw_ref[...], preferred_element_type=F32) * ps_ref[...]
    mix = jnp.dot(a.astype(BF16), wa_ref[...], preferred_element_type=F32)
    mix = mix + jnp.dot(o_ref[...], wb_ref[...], preferred_element_type=F32)
    x1 = _layer_norm(ALPHA * x_ref[...] + mix, lw0_ref[...], lb0_ref[...])
    ffn = _swiglu_tile(x1.astype(BF16), wg_ref[...], wu_ref[...], wd_ref[...])
    out_ref[...] = _layer_norm(ALPHA * x1 + ffn, lw1_ref[...], lb1_ref[...])


def _layer0_tail(u, attn, x2d, pool_bd, pool_scale, wa, wb, ln0, wg, wu, wd, ln1, b, s, tm=512):
    d = x2d.shape[1]
    f = wg.shape[1]
    per = tm // POOL_HALO
    cur = lambda bi, i: (bi, i, 0)
    prev = lambda bi, i: (bi, jnp.maximum(i * per - 1, 0), 0)
    const = lambda bi, i: (0, 0)
    out = pl.pallas_call(
        functools.partial(_layer0_tail_kernel, tm=tm),
        grid=(b, s // tm),
        in_specs=[pl.BlockSpec((None, tm, POOL_WIDTH), cur),
                  pl.BlockSpec((None, POOL_HALO, POOL_WIDTH), prev),
                  pl.BlockSpec((None, tm, SWA_Q_DIM), cur),
                  pl.BlockSpec((None, tm, d), cur),
                  pl.BlockSpec((POOL_WIDTH, POOL_WIDTH), const),
                  pl.BlockSpec((1, POOL_WIDTH), const),
                  pl.BlockSpec((POOL_WIDTH, d), const),
                  pl.BlockSpec((SWA_Q_DIM, d), const),
                  pl.BlockSpec((1, d), const),
                  pl.BlockSpec((1, d), const),
                  pl.BlockSpec((d, f), const),
                  pl.BlockSpec((d, f), const),
                  pl.BlockSpec((f, d), const),
                  pl.BlockSpec((1, d), const),
                  pl.BlockSpec((1, d), const)],
        out_specs=pl.BlockSpec((None, tm, d), cur),
        out_shape=jax.ShapeDtypeStruct((b, s, d), F32),
        compiler_params=_params(("parallel", "parallel"), vmem=VMEM_LIMIT_BIG),
        name="layer0_tail",
    )(u.reshape(b, s, POOL_WIDTH), u.reshape(b, s, POOL_WIDTH), attn.reshape(b, s, SWA_Q_DIM),
      x2d.reshape(b, s, d), pool_bd, pool_scale, wa, wb, *ln0, wg, wu, wd, *ln1)
    return out.reshape(b * s, d)


def _kv_operands(kv, low):
    swapped = pltpu.roll(kv, HEAD_DIM, 1)
    return jnp.where(low, kv, swapped), jnp.where(low, swapped, kv)


def _inproj1_kernel(x_ref, wq_ref, wkv_ref, wc_ref, cw_ref, cos_ref, sin_ref,
                    q0_ref, q1_ref, q2_ref, kv0_ref, kv1_ref, kv2_ref, d_ref,
                    zc_ref, sq_ref, skv_ref, *, tm):
    i = pl.program_id(1)
    xb = x_ref[...].astype(BF16)
    cos = cos_ref[...]
    sin = sin_ref[...]
    lane = lax.broadcasted_iota(I32, (tm, LANES), 1)
    low = lane < HEAD_DIM
    cos_k = jnp.where(low, cos, 1.0)
    sin_k = jnp.where(low, sin, 0.0)
    q_refs = (q0_ref, q1_ref, q2_ref)
    kv_refs = (kv0_ref, kv1_ref, kv2_ref)
    for gi, (_, dil) in enumerate(DIL_PAIRS):
        q = jnp.dot(xb, wq_ref[gi], preferred_element_type=F32)
        q = _rope(q, cos, sin) * QK_SCALE
        kv = jnp.dot(xb, wkv_ref[gi], preferred_element_type=F32)
        kv = _rope(kv, cos_k, sin_k)
        kvx = [_kv_operands(kv[:, h * LANES:(h + 1) * LANES], low) for h in range(DIL_KV_HEADS)]
        if dil == 1:
            q_refs[gi][0] = q.astype(BF16)
            for h in range(DIL_KV_HEADS):
                kv_refs[gi][h, 0] = jnp.concatenate(kvx[h], axis=1).astype(BF16)
        else:
            n = tm // dil
            for c in range(DIL_Q_DIM // LANES):
                sq_ref[c] = q[:, c * LANES:(c + 1) * LANES]
            for h in range(DIL_KV_HEADS):
                for part in range(KV_PARTS):
                    skv_ref[h * KV_PARTS + part] = kvx[h][part]
            for r in range(dil):
                rows = pl.ds(r, n, stride=dil)
                q_refs[gi][r] = jnp.concatenate(
                    [sq_ref[c, rows, :] for c in range(DIL_Q_DIM // LANES)], axis=1).astype(BF16)
                for h in range(DIL_KV_HEADS):
                    kv_refs[gi][h, r] = jnp.concatenate(
                        [skv_ref[h * KV_PARTS + part, rows, :] for part in range(KV_PARTS)], axis=1).astype(BF16)

    hc = jnp.dot(xb, wc_ref[...], preferred_element_type=F32)
    z = hc[:, 2 * CONV_WIDTH:] * hc[:, :CONV_WIDTH]
    zprev = jnp.where(i == 0, 0.0, zc_ref[...])
    zfull = jnp.concatenate([zprev, z], axis=0)
    z1 = pltpu.roll(zfull, 1, 0)[CONV_HALO:]
    z2 = pltpu.roll(zfull, 2, 0)[CONV_HALO:]
    cw = cw_ref[...]
    y = cw[0:1] * z2 + cw[1:2] * z1 + cw[2:3] * z
    d_ref[...] = (hc[:, CONV_WIDTH:2 * CONV_WIDTH] * y).astype(BF16)
    zc_ref[...] = z[tm - CONV_HALO:]


def _inproj1(x2d, wq, wkv, wc, conv_w, cos_t, sin_t, b, s, tm=512):
    d = x2d.shape[1]
    cur = lambda bi, i: (bi, i, 0)
    row = lambda bi, i: (bi * (s // tm) + i, 0)
    in_specs = [pl.BlockSpec((None, tm, d), cur),
                pl.BlockSpec(wq.shape, lambda bi, i: (0, 0, 0)),
                pl.BlockSpec(wkv.shape, lambda bi, i: (0, 0, 0)),
                pl.BlockSpec(wc.shape, lambda bi, i: (0, 0)),
                pl.BlockSpec(conv_w.shape, lambda bi, i: (0, 0)),
                pl.BlockSpec((tm, LANES), row),
                pl.BlockSpec((tm, LANES), row)]
    out_specs, out_shape = [], []
    for _, dil in DIL_PAIRS:
        out_specs.append(pl.BlockSpec((None, dil, tm // dil, DIL_Q_DIM), lambda bi, i: (bi, 0, i, 0)))
        out_shape.append(jax.ShapeDtypeStruct((b, dil, s // dil, DIL_Q_DIM), BF16))
    for _, dil in DIL_PAIRS:
        out_specs.append(pl.BlockSpec((None, DIL_KV_HEADS, dil, tm // dil, KV_PARTS * LANES),
                                      lambda bi, i: (bi, 0, 0, i, 0)))
        out_shape.append(jax.ShapeDtypeStruct((b, DIL_KV_HEADS, dil, s // dil, KV_PARTS * LANES), BF16))
    out_specs.append(pl.BlockSpec((None, tm, CONV_WIDTH), cur))
    out_shape.append(jax.ShapeDtypeStruct((b, s, CONV_WIDTH), BF16))
    return pl.pallas_call(
        functools.partial(_inproj1_kernel, tm=tm),
        grid=(b, s // tm),
        in_specs=in_specs,
        out_specs=out_specs,
        out_shape=out_shape,
        scratch_shapes=[pltpu.VMEM((CONV_HALO, CONV_WIDTH), F32),
                        pltpu.VMEM((DIL_Q_DIM // LANES, tm, LANES), F32),
                        pltpu.VMEM((DIL_KV_HEADS * KV_PARTS, tm, LANES), F32)],
        compiler_params=_params(("parallel", "arbitrary")),
        name="inproj1",
    )(x2d.reshape(b, s, d), wq, wkv, wc, conv_w, cos_t, sin_t)


def _dil_kernel(q0_ref, q1_ref, q2_ref, kv0_ref, kv1_ref, kv2_ref, out_ref, o_run, l_run, bias_ref, *, s):
    q_refs = (q0_ref, q1_ref, q2_ref)
    kv_refs = (kv0_ref, kv1_ref, kv2_ref)
    n_blocks = s // ATT_BLOCK
    width = DIL_Q_PER_KV * HEAD_DIM
    n_chunks = width // LANES
    max_dist = DIL_PAIRS[0][0] // DIL_PAIRS[0][1]
    assert all(w // d == max_dist for w, d in DIL_PAIRS)
    stacked = n_chunks * ATT_BLOCK
    bias_ref[0] = _band_bias(stacked, max_dist, 0)
    bias_ref[1] = _band_bias(stacked, max_dist, ATT_BLOCK)
    for gi, (_, dil) in enumerate(DIL_PAIRS):
        q_ref, kv_ref = q_refs[gi], kv_refs[gi]
        blocks_per_sub = (s // dil) // ATT_BLOCK
        shift = blocks_per_sub.bit_length() - 1

        def body(n, carry, q_ref=q_ref, kv_ref=kv_ref, gi=gi, dil=dil, blocks_per_sub=blocks_per_sub, shift=shift):
            res = lax.shift_right_logical(n, shift)
            bi = n & (blocks_per_sub - 1)
            rows = pl.ds(pl.multiple_of(n * ATT_BLOCK, ATT_BLOCK), ATT_BLOCK)
            prow = pl.ds(pl.multiple_of(jnp.maximum(n - 1, 0) * ATT_BLOCK, ATT_BLOCK), ATT_BLOCK)
            kvx = jnp.concatenate([kv_ref[prow, :], kv_ref[rows, :]], axis=0)
            kk, vv = _kv_parts(kvx)
            q4 = q_ref[rows, :]
            qs = jnp.concatenate([q4[:, c * LANES:(c + 1) * LANES] for c in range(n_chunks)], axis=0)
            bias = bias_ref[jnp.where(bi == 0, 1, 0)]
            low = lax.broadcasted_iota(I32, (n_chunks * ATT_BLOCK, LANES), 1) < HEAD_DIM
            zero = jnp.zeros_like(qs)
            oa, ma, da = _attn_stack(jnp.where(low, qs, zero), kk, vv, bias)
            ob, mb, db = _attn_stack(jnp.where(low, zero, qs), kk, vv, bias)
            o_st = jnp.where(low, oa * (1.0 / da), ob * (1.0 / db))
            l_st = jnp.where(low, ma + jnp.log(da), mb + jnp.log(db))
            start = res + dil * ATT_BLOCK * bi
            if dil == 1:
                tok = pl.ds(pl.multiple_of(start, ATT_BLOCK), ATT_BLOCK)
            else:
                tok = pl.ds(start, ATT_BLOCK, stride=dil)
            for c in range(n_chunks):
                o_c = o_st[c * ATT_BLOCK:(c + 1) * ATT_BLOCK]
                l_c = l_st[c * ATT_BLOCK:(c + 1) * ATT_BLOCK]
                if gi == 0:
                    o_run[c, tok, :] = o_c
                    l_run[c, tok, :] = l_c
                else:
                    o_old = o_run[c, tok, :]
                    l_old = l_run[c, tok, :]
                    m = jnp.maximum(l_old, l_c)
                    a = jnp.exp(l_old - m)
                    bb = jnp.exp(l_c - m)
                    tot = a + bb
                    o_run[c, tok, :] = (o_old * a + o_c * bb) * (1.0 / tot)
                    if gi < len(DIL_PAIRS) - 1:
                        l_run[c, tok, :] = m + jnp.log(tot)
            return carry

        lax.fori_loop(0, n_blocks, body, 0, unroll=8)
    for c in range(n_chunks):
        out_ref[:, c * LANES:(c + 1) * LANES] = o_run[c].astype(BF16)


def _dilated(qs, kvs, b, s):
    width = DIL_Q_PER_KV * HEAD_DIM
    in_specs = [pl.BlockSpec((None, s, width), lambda bi, h: (bi, 0, h)) for _ in DIL_PAIRS]
    in_specs += [pl.BlockSpec((None, None, s, KV_PARTS * LANES), lambda bi, h: (bi, h, 0, 0)) for _ in DIL_PAIRS]
    qs = [q.reshape(b, s, DIL_Q_DIM) for q in qs]
    kvs = [kv.reshape(b, DIL_KV_HEADS, s, KV_PARTS * LANES) for kv in kvs]
    out = pl.pallas_call(
        functools.partial(_dil_kernel, s=s),
        grid=(b, DIL_KV_HEADS),
        in_specs=in_specs,
        out_specs=pl.BlockSpec((None, s, width), lambda bi, h: (bi, 0, h)),
        out_shape=jax.ShapeDtypeStruct((b, s, DIL_Q_DIM), BF16),
        scratch_shapes=[pltpu.VMEM((width // LANES, s, LANES), F32),
                        pltpu.VMEM((width // LANES, s, LANES), F32),
                        pltpu.VMEM((2, (width // LANES) * ATT_BLOCK, 2 * ATT_BLOCK), F32)],
        compiler_params=_params(("parallel", "parallel"), vmem=VMEM_LIMIT_BIG),
        name="dilated_attention",
    )(*qs, *kvs)
    return out.reshape(b * s, DIL_Q_DIM)


def _to_slabs(ref, val, base=0):
    rows, d = val.shape
    per = d // LANES
    for c in range(per):
        ref[pl.ds(base + c, rows, stride=per), :] = val[:, c * LANES:(c + 1) * LANES]


def _from_slabs(ref, rows, base=0, per=ROW_SLAB):
    return jnp.concatenate([ref[pl.ds(base + c, rows, stride=per), :] for c in range(per)], axis=1)


def _route_tile(x, wr, e_ref, g_ref, r_ref, cnt_ref, run_ref, tm, col0):
    cols = slice(col0, col0 + tm)
    logits = lax.dot_general(wr, x, (((1,), (1,)), ((), ())),
                             precision=lax.Precision.HIGHEST, preferred_element_type=F32)
    eid = lax.broadcasted_iota(I32, (N_EXPERTS, tm), 0)
    m1 = jnp.max(logits, axis=0, keepdims=True)
    i1 = jnp.min(jnp.where(logits == m1, eid, N_EXPERTS), axis=0, keepdims=True)
    rest = jnp.where(eid == i1, -jnp.inf, logits)
    m2 = jnp.max(rest, axis=0, keepdims=True)
    i2 = jnp.min(jnp.where(rest == m2, eid, N_EXPERTS), axis=0, keepdims=True)
    t2 = jnp.exp(m2 - m1)
    g1 = 1.0 / (1.0 + t2)
    g2 = t2 / (1.0 + t2)
    oh1 = eid == i1
    oh2 = eid == i2
    oh = jnp.where(oh1 | oh2, 1.0, 0.0)
    ri = lax.broadcasted_iota(I32, (tm, tm), 0)
    ci = lax.broadcasted_iota(I32, (tm, tm), 1)
    tri = jnp.where(ri <= ci, 1.0, 0.0).astype(BF16)
    incl = jnp.dot(oh.astype(BF16), tri, preferred_element_type=F32)
    rank = run_ref[:, 0:1] + incl - oh
    r1 = jnp.sum(jnp.where(oh1, rank, 0.0), axis=0, keepdims=True)
    r2 = jnp.sum(jnp.where(oh2, rank, 0.0), axis=0, keepdims=True)
    e_ref[0:1, cols] = i1
    e_ref[1:2, cols] = i2
    g_ref[0:1, cols] = g1
    g_ref[1:2, cols] = g2
    r_ref[0:1, cols] = r1.astype(I32)
    r_ref[1:2, cols] = r2.astype(I32)
    run_ref[...] = run_ref[...] + incl[:, tm - 1:tm]
    cnt_ref[...] = run_ref[...]


def _outproj1_kernel(c_ref, d_ref, x_ref, wa_ref, wb_ref, lw_ref, lb_ref, wr_ref,
                     slab_ref, e_ref, g_ref, r_ref, cnt_ref, run_ref, *, tm, sub):
    @pl.when(pl.program_id(0) == 0)
    def _():
        run_ref[...] = jnp.zeros_like(run_ref)

    for h in range(tm // sub):
        rows = slice(h * sub, (h + 1) * sub)
        mix = jnp.dot(c_ref[rows, :], wa_ref[...], preferred_element_type=F32)
        mix = mix + jnp.dot(d_ref[rows, :], wb_ref[...], preferred_element_type=F32)
        y = _layer_norm(ALPHA * x_ref[rows, :] + mix, lw_ref[...], lb_ref[...])
        _to_slabs(slab_ref, y, base=h * sub * ROW_SLAB)
        _route_tile(y, wr_ref[...], e_ref, g_ref, r_ref, cnt_ref, run_ref, sub, h * sub)


def _outproj1(c, dconv, x2d, wa, wb, ln_w, ln_b, wr_t, tm=1024, sub=512):
    t, d = x2d.shape
    row = lambda i: (i, 0)
    col = lambda i: (0, i)
    const = lambda i: (0, 0)
    return pl.pallas_call(
        functools.partial(_outproj1_kernel, tm=tm, sub=sub),
        grid=(t // tm,),
        in_specs=[pl.BlockSpec((tm, DIL_Q_DIM), row),
                  pl.BlockSpec((tm, CONV_WIDTH), row),
                  pl.BlockSpec((tm, d), row),
                  pl.BlockSpec((DIL_Q_DIM, d), const),
                  pl.BlockSpec((CONV_WIDTH, d), const),
                  pl.BlockSpec((1, d), const),
                  pl.BlockSpec((1, d), const),
                  pl.BlockSpec((N_EXPERTS, d), const)],
        out_specs=[pl.BlockSpec((tm * (d // LANES), LANES), row),
                   pl.BlockSpec((2, tm), col), pl.BlockSpec((2, tm), col), pl.BlockSpec((2, tm), col),
                   pl.BlockSpec((N_EXPERTS, LANES), const)],
        out_shape=[jax.ShapeDtypeStruct((t * (d // LANES), LANES), F32),
                   jax.ShapeDtypeStruct((2, t), I32), jax.ShapeDtypeStruct((2, t), F32),
                   jax.ShapeDtypeStruct((2, t), I32), jax.ShapeDtypeStruct((N_EXPERTS, LANES), F32)],
        scratch_shapes=[pltpu.VMEM((N_EXPERTS, LANES), F32)],
        compiler_params=_params(("arbitrary",)),
        name="outproj1_route",
    )(c, dconv, x2d, wa, wb, ln_w, ln_b, wr_t)


def _slab_copy(src_ref, dst_ref, sem, src_row, dst_row, n=1):
    src = src_ref.at[pl.ds(pl.multiple_of(src_row * ROW_SLAB, ROW_SLAB), n * ROW_SLAB)]
    dst = dst_ref.at[pl.ds(pl.multiple_of(dst_row * ROW_SLAB, ROW_SLAB), n * ROW_SLAB)]
    return pltpu.make_async_copy(src, dst, sem)


def _moe_kernel(be_ref, nv_ref, tok_ref, tokn_ref, dstp_ref, dstc_ref, x_hbm, wg_ref, wu_ref, wd_ref, y_hbm,
                xs_ref, stage_ref, sem_in, sem_out, *, tm, dump_row0):
    i = pl.program_id(0)
    n_valid = nv_ref[0]
    slot = i & 1

    def row_in(tok, dst_slot, r):
        return _slab_copy(x_hbm, xs_ref, sem_in.at[dst_slot], tok, dst_slot * tm + r)

    def row_out(src_slot, r, dst):
        return _slab_copy(stage_ref, y_hbm, sem_out.at[src_slot], src_slot * tm + r, dst)

    @pl.when(i < n_valid)
    def _():
        @pl.when(i == 0)
        def _():
            def start(r, c):
                row_in(tok_ref[0, 0, r], 0, r).start()
                return c
            lax.fori_loop(0, tm, start, 0)
            stage_ref[...] = jnp.zeros_like(stage_ref)
            fill = _slab_copy(stage_ref, y_hbm, sem_out.at[0], 0, dump_row0, n=2 * tm)
            fill.start()
            fill.wait()

        _slab_copy(x_hbm, xs_ref, sem_in.at[slot], 0, slot * tm, n=tm).wait()
        xb = _from_slabs(xs_ref, tm, base=slot * (tm * ROW_SLAB)).astype(BF16)

        for r in range(tm):
            row_in(tokn_ref[0, 0, r], 1 - slot, r).start(priority=1)
            row_out(1 - slot, r, dstp_ref[0, 0, r]).start(priority=1)

        part = _swiglu_tile(xb, wg_ref[...], wu_ref[...], wd_ref[...])
        _slab_copy(stage_ref, y_hbm, sem_out.at[1 - slot], (1 - slot) * tm, 0, n=tm).wait()
        _to_slabs(stage_ref, part, base=slot * (tm * ROW_SLAB))

        @pl.when(i + 1 >= n_valid)
        def _():
            _slab_copy(x_hbm, xs_ref, sem_in.at[1 - slot], 0, (1 - slot) * tm, n=tm).wait()

            def start(r, c):
                row_out(slot, r, dstc_ref[0, 0, r]).start()
                return c
            lax.fori_loop(0, tm, start, 0)
            _slab_copy(stage_ref, y_hbm, sem_out.at[slot], slot * tm, 0, n=tm).wait()


def _moe_experts(x_slabs, row_tok, row_dst, dump_row0, block_e, n_valid, wg, wu, wd, tm):
    d = wg.shape[1]
    n_rows = row_tok.shape[0]
    n_blocks = n_rows // tm
    f = wg.shape[2]
    y_rows = dump_row0 + 2 * tm

    def w_map(i, be, nv):
        return be[jnp.minimum(i, nv[0] - 1)], 0, 0

    def smem(index_map):
        return pl.BlockSpec((1, 1, tm), index_map, memory_space=pltpu.SMEM)

    resident = dict(pipeline_mode=pl.Buffered(1))
    grid_spec = pltpu.PrefetchScalarGridSpec(
        num_scalar_prefetch=2,
        grid=(n_blocks,),
        in_specs=[smem(lambda i, be, nv: (i, 0, 0)),
                  smem(lambda i, be, nv: (jnp.minimum(i + 1, n_blocks - 1), 0, 0)),
                  smem(lambda i, be, nv: (jnp.where(i == 0, n_blocks, i - 1), 0, 0)),
                  smem(lambda i, be, nv: (i, 0, 0)),
                  pl.BlockSpec(memory_space=pl.ANY),
                  pl.BlockSpec((None, d, f), w_map, **resident),
                  pl.BlockSpec((None, d, f), w_map, **resident),
                  pl.BlockSpec((None, f, d), w_map, **resident)],
        out_specs=pl.BlockSpec(memory_space=pl.ANY),
        scratch_shapes=[pltpu.VMEM((2 * tm * ROW_SLAB, LANES), F32),
                        pltpu.VMEM((2 * tm * ROW_SLAB, LANES), F32),
                        pltpu.SemaphoreType.DMA((2,)),
                        pltpu.SemaphoreType.DMA((2,))],
    )
    tok3 = row_tok.reshape(n_blocks, 1, tm)
    dst3 = row_dst.reshape(n_blocks + 1, 1, tm)
    return pl.pallas_call(
        functools.partial(_moe_kernel, tm=tm, dump_row0=dump_row0),
        grid_spec=grid_spec,
        out_shape=jax.ShapeDtypeStruct((y_rows * ROW_SLAB, LANES), F32),
        compiler_params=_params(("arbitrary",), vmem=VMEM_LIMIT_BIG),
        name="moe_experts",
    )(block_e, n_valid, tok3, tok3, dst3, dst3, x_slabs, wg, wu, wd)


def _combine_kernel(ya_ref, yb_ref, x_ref, g1_ref, g2_ref, lw_ref, lb_ref, out_ref, *, tm):
    ffn = _from_slabs(ya_ref, tm) * g1_ref[...] + _from_slabs(yb_ref, tm) * g2_ref[...]
    out_ref[...] = _layer_norm(ALPHA * _from_slabs(x_ref, tm) + ffn, lw_ref[...], lb_ref[...])


def _combine(y, x_slabs, g1, g2, ln_w, ln_b, tm=512):
    t = g1.shape[0]
    d = ln_w.shape[1]
    row = lambda i: (i, 0)
    return pl.pallas_call(
        functools.partial(_combine_kernel, tm=tm),
        grid=(t // tm,),
        in_specs=[pl.BlockSpec((tm * ROW_SLAB, LANES), row),
                  pl.BlockSpec((tm * ROW_SLAB, LANES), lambda i: (t // tm + i, 0)),
                  pl.BlockSpec((tm * ROW_SLAB, LANES), row),
                  pl.BlockSpec((tm, 1), row),
                  pl.BlockSpec((tm, 1), row),
                  pl.BlockSpec((1, d), lambda i: (0, 0)),
                  pl.BlockSpec((1, d), lambda i: (0, 0))],
        out_specs=pl.BlockSpec((tm, d), row),
        out_shape=jax.ShapeDtypeStruct((t, d), F32),
        compiler_params=_params(("parallel",)),
        name="moe_combine",
    )(y, y, x_slabs, g1.reshape(t, 1), g2.reshape(t, 1), ln_w, ln_b)


SC_CORES = 2
SC_SUBCORES = 16
SC_LANES = 16
SC_CHUNK = 8192


def _invert_rows(idx, vals, default):
    n_pairs = idx.shape[0]
    n_out = default.shape[0]
    n_workers = SC_CORES * SC_SUBCORES
    per = n_out // n_workers
    assert per * n_workers == n_out and per % 8 == 0 and n_pairs % SC_CHUNK == 0
    mesh = plsc.VectorSubcoreMesh(core_axis_name="c", subcore_axis_name="s",
                                  num_cores=SC_CORES, num_subcores=SC_SUBCORES)

    def body(idx_hbm, val_hbm, dflt_hbm, out_hbm, idx_v, val_v, loc_v):
        wid = lax.axis_index("s") * SC_CORES + lax.axis_index("c")
        lo = wid * per
        pltpu.sync_copy(dflt_hbm.at[pl.ds(lo, per)], loc_v)

        def chunk(c, carry):
            pltpu.sync_copy(idx_hbm.at[pl.ds(c * SC_CHUNK, SC_CHUNK)], idx_v)
            pltpu.sync_copy(val_hbm.at[pl.ds(c * SC_CHUNK, SC_CHUNK)], val_v)

            def step(j, carry):
                rel = idx_v[pl.ds(j * SC_LANES, SC_LANES)] - lo
                mine = (rel >= 0) & (rel < per)
                plsc.store_scatter(loc_v, [jnp.where(mine, rel, 0)], val_v[pl.ds(j * SC_LANES, SC_LANES)],
                                   mask=mine)
                return carry

            return lax.fori_loop(0, SC_CHUNK // SC_LANES, step, carry)

        lax.fori_loop(0, n_pairs // SC_CHUNK, chunk, 0)
        pltpu.sync_copy(loc_v, out_hbm.at[pl.ds(lo, per)])

    return pl.kernel(
        body,
        out_type=jax.ShapeDtypeStruct((n_out,), I32),
        mesh=mesh,
        scratch_types=[pltpu.VMEM((SC_CHUNK,), I32), pltpu.VMEM((SC_CHUNK,), I32), pltpu.VMEM((per,), I32)],
        compiler_params=pltpu.CompilerParams(needs_layout_passes=False),
        name="moe_invert_rows",
    )(idx, vals, default)


def _moe_layer(x_slabs, e_sel, gates, ranks, counts, wg, wu, wd, ln_w, ln_b, tmoe=512):
    t = e_sel.shape[1]
    counts = counts[:, 0].astype(I32)
    padded = (counts + tmoe - 1) // tmoe * tmoe
    pend = jnp.cumsum(padded)
    pstart = pend - padded
    n_blocks = -(-(2 * t + N_EXPERTS * (tmoe - 1)) // tmoe)
    n_rows = n_blocks * tmoe
    first_row = sum(jnp.where(e_sel == e, pstart[e], 0) for e in range(N_EXPERTS))
    dest = first_row + ranks
    dump_row0 = 2 * t
    r = jnp.arange(n_rows, dtype=I32)
    dump = dump_row0 + ((r // tmoe) & 1) * tmoe + r % tmoe
    tok = jnp.arange(t, dtype=I32)
    row_dst = _invert_rows(dest.reshape(-1), jnp.concatenate([tok, t + tok]), dump)
    row_tok = jnp.where(row_dst < dump_row0, row_dst % t, 0)
    row_dst = jnp.concatenate([row_dst, dump_row0 + tmoe + jnp.arange(tmoe, dtype=I32)])
    block_row0 = jnp.arange(n_blocks, dtype=I32) * tmoe
    block_e = jnp.minimum(jnp.sum((block_row0[:, None] >= pend[None, :]).astype(I32), axis=1), N_EXPERTS - 1)
    n_valid = (pend[-1:] // tmoe).astype(I32)
    y = _moe_experts(x_slabs, row_tok, row_dst, dump_row0, block_e, n_valid, wg, wu, wd, tmoe)
    return _combine(y, x_slabs, gates[0], gates[1], ln_w, ln_b)


def kernel(x, positions, ln_w, ln_b, even_w_in, pool_w, pool_scale, swa_sinks, even_w_out, ffn_w_gate, ffn_w_up, ffn_w_down, odd_w_in, conv_w, odd_w_out, router_w, moe_w_gate, moe_w_up, moe_w_down):
    b, s, d = x.shape
    t = b * s
    x2d = x.reshape(t, d)
    ln = lambda layer, k: (ln_w[layer, k].reshape(1, d), ln_b[layer, k].reshape(1, d))

    u, q, k, v, cos_t, sin_t = _inproj0(x2d, even_w_in[0].astype(BF16), positions)
    moe_w = (moe_w_gate[0], moe_w_up[0], moe_w_down[0])
    attn, moe_bf = _swa(q, k, v, swa_sinks[0], b, s, cast=[w.reshape(-1, w.shape[-1]) for w in moe_w])
    moe_bf = [wb.reshape(w.shape) for wb, w in zip(moe_bf, moe_w)]
    pool_bd = jax.scipy.linalg.block_diag(*[pool_w[0, gi] for gi in range(len(POOL_WINDOWS))]).astype(BF16)
    w_out0 = even_w_out[0].astype(BF16)
    x2d = _layer0_tail(u, attn, x2d, pool_bd, pool_scale[0].reshape(1, POOL_WIDTH),
                       w_out0[:POOL_WIDTH], w_out0[POOL_WIDTH:], ln(0, 0),
                       ffn_w_gate[0].astype(BF16), ffn_w_up[0].astype(BF16), ffn_w_down[0].astype(BF16),
                       ln(0, 1), b, s)

    w_in1 = odd_w_in[0].astype(BF16)
    c_in = len(DIL_PAIRS) * DIL_GROUP_IN
    wq, wkv = [], []
    for gi in range(len(DIL_PAIRS)):
        g0 = gi * DIL_GROUP_IN
        wq.append(w_in1[:, g0:g0 + DIL_Q_DIM])
        kcol = g0 + DIL_Q_DIM
        vcol = kcol + DIL_KV_DIM
        parts = []
        for h in range(DIL_KV_HEADS):
            parts += [w_in1[:, kcol + h * HEAD_DIM:kcol + (h + 1) * HEAD_DIM],
                      w_in1[:, vcol + h * HEAD_DIM:vcol + (h + 1) * HEAD_DIM]]
        wkv.append(jnp.concatenate(parts, axis=1))
    outs = _inproj1(x2d, jnp.stack(wq), jnp.stack(wkv), w_in1[:, c_in:], conv_w[0], cos_t, sin_t, b, s)
    c_out = _dilated(outs[0:3], outs[3:6], b, s)
    w_out1 = odd_w_out[0].astype(BF16)
    x_slabs, e_sel, gates, ranks, counts = _outproj1(
        c_out, outs[6].reshape(t, CONV_WIDTH), x2d, w_out1[:DIL_Q_DIM], w_out1[DIL_Q_DIM:], *ln(1, 0),
        router_w[0].T)
    x2d = _moe_layer(x_slabs, e_sel, gates, ranks, counts, *moe_bf, *ln(1, 1))
    return x2d.reshape(b, s, d)
```

```python
import functools
import math

import jax
import jax.numpy as jnp
from jax import lax
from jax.experimental import pallas as pl
from jax.experimental.pallas import tpu as pltpu
from jax.experimental.pallas import tpu_sc as plsc

F32 = jnp.float32
BF16 = jnp.bfloat16
I32 = jnp.int32

HEAD_DIM = 64
ROPE_THETA = 10000.0
ATT_BLOCK = 128
LN_EPS = 1e-5
POOL_WINDOWS = (2, 4, 8, 16)
POOL_GROUP = 64
POOL_WIDTH = 256
POOL_HALO = 16
SWA_WINDOW = 128
SWA_Q_HEADS = 12
SWA_KV_HEADS = 4
SWA_Q_DIM = SWA_Q_HEADS * HEAD_DIM
SWA_KV_DIM = SWA_KV_HEADS * HEAD_DIM
DIL_PAIRS = ((128, 1), (512, 4), (2048, 16))
DIL_Q_HEADS = 8
DIL_KV_HEADS = 2
DIL_Q_DIM = DIL_Q_HEADS * HEAD_DIM
DIL_KV_DIM = DIL_KV_HEADS * HEAD_DIM
DIL_GROUP_IN = DIL_Q_DIM + 2 * DIL_KV_DIM
DIL_Q_PER_KV = DIL_Q_HEADS // DIL_KV_HEADS
CONV_WIDTH = 512
CONV_K = 3
CONV_HALO = 8
N_EXPERTS = 8
DEPTH = 2
ALPHA = (2 * DEPTH) ** 0.25
QK_SCALE = 1.0 / math.sqrt(HEAD_DIM)
NEG_BIG = -1e30
KV_PARTS = 2

LANES = 128
ROW_SLAB = 8
VMEM_LIMIT = 48 * 1024 * 1024
VMEM_LIMIT_BIG = 56 * 1024 * 1024


def _params(sem, vmem=VMEM_LIMIT):
    return pltpu.CompilerParams(dimension_semantics=sem, vmem_limit_bytes=vmem)


def _layer_norm(y, w, b):
    mu = jnp.mean(y, axis=-1, keepdims=True)
    yc = y - mu
    var = jnp.mean(yc * yc, axis=-1, keepdims=True)
    return yc * lax.rsqrt(var + LN_EPS) * w + b


def _rope_chunk(xc, cos, sin_signed, first_half):
    rot = jnp.where(first_half, pltpu.roll(xc, 96, 1), pltpu.roll(xc, 32, 1))
    return xc * cos + rot * sin_signed


def _rope(x, cos, sin_signed):
    tm, c = x.shape
    lane = lax.broadcasted_iota(I32, (tm, LANES), 1)
    first_half = (lane & 32) == 0
    chunks = [_rope_chunk(x[:, i * LANES:(i + 1) * LANES], cos, sin_signed, first_half)
              for i in range(c // LANES)]
    return chunks[0] if len(chunks) == 1 else jnp.concatenate(chunks, axis=1)


def _rope_tables(pos, inv):
    ang = inv * pos.astype(F32)
    c = jnp.cos(ang)
    s = jnp.sin(ang)
    c4 = jnp.concatenate([c, c, c, c], axis=0)
    s4 = jnp.concatenate([-s, s, -s, s], axis=0)
    return c4.T, s4.T


def _inproj0_kernel(x_ref, w_ref, pos_ref, inv_ref, u_ref, q_ref, k_ref, v_ref, cos_ref, sin_ref):
    xb = x_ref[...].astype(BF16)
    cos, sin = _rope_tables(pos_ref[...], inv_ref[...])
    cos_ref[...] = cos
    sin_ref[...] = sin
    q0 = POOL_WIDTH
    k0 = q0 + SWA_Q_DIM
    v0 = k0 + SWA_KV_DIM
    u_ref[...] = jnp.dot(xb, w_ref[:, :q0], preferred_element_type=F32)
    q = jnp.dot(xb, w_ref[:, q0:k0], preferred_element_type=F32)
    q_ref[...] = (_rope(q, cos, sin) * QK_SCALE).astype(BF16)
    k = _rope(jnp.dot(xb, w_ref[:, k0:v0], preferred_element_type=F32), cos, sin)
    v = jnp.dot(xb, w_ref[:, v0:], preferred_element_type=F32)
    low = lax.broadcasted_iota(I32, (xb.shape[0], LANES), 1) < HEAD_DIM

    def per_head(pairs):
        out = []
        for c in range(pairs.shape[1] // LANES):
            chunk = pairs[:, c * LANES:(c + 1) * LANES]
            swapped = pltpu.roll(chunk, HEAD_DIM, 1)
            out += [jnp.where(low, chunk, swapped), jnp.where(low, swapped, chunk)]
        return jnp.concatenate(out, axis=1).astype(BF16)

    k_ref[...] = per_head(k)
    v_ref[...] = per_head(v)


def _inproj0(x2d, w_bf, positions, tm=512):
    t, d = x2d.shape
    n_in = w_bf.shape[1]
    half = HEAD_DIM // 2
    inv = ROPE_THETA ** (-jnp.arange(half, dtype=F32) / half)
    row = lambda i: (i, 0)
    return pl.pallas_call(
        _inproj0_kernel,
        grid=(t // tm,),
        in_specs=[pl.BlockSpec((tm, d), row),
                  pl.BlockSpec((d, n_in), lambda i: (0, 0)),
                  pl.BlockSpec((1, tm), lambda i: (0, i)),
                  pl.BlockSpec((half, 1), lambda i: (0, 0))],
        out_specs=[pl.BlockSpec((tm, POOL_WIDTH), row),
                   pl.BlockSpec((tm, SWA_Q_DIM), row),
                   pl.BlockSpec((tm, SWA_KV_HEADS * LANES), row),
                   pl.BlockSpec((tm, SWA_KV_HEADS * LANES), row),
                   pl.BlockSpec((tm, LANES), row),
                   pl.BlockSpec((tm, LANES), row)],
        out_shape=[jax.ShapeDtypeStruct((t, POOL_WIDTH), F32),
                   jax.ShapeDtypeStruct((t, SWA_Q_DIM), BF16),
                   jax.ShapeDtypeStruct((t, SWA_KV_HEADS * LANES), BF16),
                   jax.ShapeDtypeStruct((t, SWA_KV_HEADS * LANES), BF16),
                   jax.ShapeDtypeStruct((t, LANES), F32),
                   jax.ShapeDtypeStruct((t, LANES), F32)],
        compiler_params=_params(("parallel",)),
        name="inproj0",
    )(x2d, w_bf, positions.reshape(1, t), inv.reshape(half, 1))


def _band_bias(rows, max_dist, key_lo):
    qi = lax.broadcasted_iota(I32, (rows, 2 * ATT_BLOCK), 0) & (ATT_BLOCK - 1)
    sj = lax.broadcasted_iota(I32, (rows, 2 * ATT_BLOCK), 1)
    dist = ATT_BLOCK + qi - sj
    return jnp.where((dist >= 0) & (dist <= max_dist) & (sj >= key_lo), 0.0, NEG_BIG)


def _band_mask(max_dist, key_lo):
    qi = lax.broadcasted_iota(I32, (ATT_BLOCK, 2 * ATT_BLOCK), 0)
    sj = lax.broadcasted_iota(I32, (ATT_BLOCK, 2 * ATT_BLOCK), 1)
    dist = ATT_BLOCK + qi - sj
    return (dist >= 0) & (dist <= max_dist) & (sj >= key_lo)


def _attn_block(q, kwin, vwin, valid, sink):
    s = lax.dot_general(q, kwin, (((1,), (1,)), ((), ())), preferred_element_type=F32)
    s = jnp.where(valid, s, NEG_BIG)
    m = jnp.maximum(jnp.max(s, axis=1, keepdims=True), sink)
    p = jnp.exp(s - m)
    den = jnp.sum(p, axis=1, keepdims=True) + jnp.exp(sink - m)
    o = jnp.dot(p.astype(BF16), vwin, preferred_element_type=F32)
    return o * (1.0 / den)


def _attn_stack(qs, kpart, vpart, bias):
    s = lax.dot_general(qs, kpart, (((1,), (1,)), ((), ())), preferred_element_type=F32) + bias
    m = jnp.max(s, axis=1, keepdims=True)
    p = jnp.exp(s - m)
    den = jnp.sum(p, axis=1, keepdims=True)
    return jnp.dot(p.astype(BF16), vpart, preferred_element_type=F32), m, den


def _kv_parts(kvx, h=0):
    base = h * KV_PARTS * LANES
    return [kvx[:, base + part * LANES:base + (part + 1) * LANES] for part in range(KV_PARTS)]


def _swa_kernel(sink_ref, q_ref, kp_ref, kc_ref, vp_ref, vc_ref, *rest, tq, n_cast):
    o_ref = rest[n_cast]
    for src, dst in zip(rest[:n_cast], rest[n_cast + 1:]):
        dst[...] = src[...].astype(BF16)
    i = pl.program_id(1)
    kfull = jnp.concatenate([kp_ref[...], kc_ref[...]], axis=0)
    vfull = jnp.concatenate([vp_ref[...], vc_ref[...]], axis=0)
    g = SWA_Q_HEADS // SWA_KV_HEADS
    low = lax.broadcasted_iota(I32, (ATT_BLOCK, LANES), 1) < HEAD_DIM
    for j in range(tq // ATT_BLOCK):
        key_lo = jnp.where(i == 0, ATT_BLOCK, 0) if j == 0 else 0
        valid = _band_mask(SWA_WINDOW - 1, key_lo)
        r0 = j * ATT_BLOCK
        heads = []
        for h in range(SWA_Q_HEADS):
            kv = h // g
            chunk = q_ref[r0:r0 + ATT_BLOCK, (h // 2) * LANES:(h // 2 + 1) * LANES]
            zero = jnp.zeros_like(chunk)
            q = jnp.where(low, chunk, zero) if h % 2 == 0 else jnp.where(low, zero, chunk)
            kwin = kfull[r0:r0 + 2 * ATT_BLOCK, kv * LANES:(kv + 1) * LANES]
            vwin = vfull[r0:r0 + 2 * ATT_BLOCK, kv * LANES:(kv + 1) * LANES]
            heads.append(_attn_block(q, kwin, vwin, valid, sink_ref[h]))
        outs = [jnp.where(low, heads[2 * c], heads[2 * c + 1]) for c in range(SWA_Q_HEADS // 2)]
        o_ref[r0:r0 + ATT_BLOCK, :] = jnp.concatenate(outs, axis=1).astype(BF16)


def _swa(q, k, v, sinks, b, s, cast=(), tq=256):
    per = tq // ATT_BLOCK
    steps = s // tq
    cur = lambda bi, i: (bi, i, 0)
    prev = lambda bi, i: (bi, jnp.maximum(i * per - 1, 0), 0)
    kvw = SWA_KV_HEADS * LANES
    q3 = q.reshape(b, s, SWA_Q_DIM)
    k3 = k.reshape(b, s, kvw)
    v3 = v.reshape(b, s, kvw)
    cast_specs = []
    for w in cast:
        rows = w.shape[0] // (b * steps)
        assert rows * b * steps == w.shape[0] and rows % 16 == 0, w.shape
        cast_specs.append(pl.BlockSpec((rows, w.shape[1]), lambda bi, i: (bi * steps + i, 0)))
    outs = pl.pallas_call(
        functools.partial(_swa_kernel, tq=tq, n_cast=len(cast)),
        grid=(b, steps),
        in_specs=[pl.BlockSpec(memory_space=pltpu.SMEM),
                  pl.BlockSpec((None, tq, SWA_Q_DIM), cur),
                  pl.BlockSpec((None, ATT_BLOCK, kvw), prev),
                  pl.BlockSpec((None, tq, kvw), cur),
                  pl.BlockSpec((None, ATT_BLOCK, kvw), prev),
                  pl.BlockSpec((None, tq, kvw), cur)] + cast_specs,
        out_specs=[pl.BlockSpec((None, tq, SWA_Q_DIM), cur)] + cast_specs,
        out_shape=[jax.ShapeDtypeStruct((b, s, SWA_Q_DIM), BF16)]
                  + [jax.ShapeDtypeStruct(w.shape, BF16) for w in cast],
        compiler_params=_params(("parallel", "parallel")),
        name="swa_attention",
    )(sinks, q3, k3, k3, v3, v3, *cast)
    return outs[0].reshape(b * s, SWA_Q_DIM), outs[1:]


def _pool_mixer(u, halo, seq_row0):
    tm = u.shape[0]
    full = jnp.concatenate([halo, u], axis=0)
    sums = [full]
    for shift in (1, 2, 4, 8):
        prev = sums[-1]
        sums.append(prev + pltpu.roll(prev, shift, 0))
    lane = lax.broadcasted_iota(I32, (tm, POOL_WIDTH), 1)
    row = lax.broadcasted_iota(I32, (tm, POOL_WIDTH), 0)
    grp = lane // POOL_GROUP
    win = sums[4][POOL_HALO:]
    width = jnp.full((tm, POOL_WIDTH), POOL_WINDOWS[3], I32)
    for gi in (2, 1, 0):
        win = jnp.where(grp == gi, sums[gi + 1][POOL_HALO:], win)
        width = jnp.where(grp == gi, POOL_WINDOWS[gi], width)
    count = jnp.minimum(seq_row0 + row + 1, width).astype(F32)
    return win / count - u


def _swiglu_tile(xb, wg, wu, wd):
    g = jnp.dot(xb, wg, preferred_element_type=F32)
    u = jnp.dot(xb, wu, preferred_element_type=F32)
    h = (g * jax.nn.sigmoid(g)) * u
    return jnp.dot(h.astype(BF16), wd, preferred_element_type=F32)


def _layer0_tail_kernel(u_ref, uh_ref, o_ref, x_ref, pw_ref, ps_ref, wa_ref, wb_ref, lw0_ref, lb0_ref,
                        wg_ref, wu_ref, wd_ref, lw1_ref, lb1_ref, out_ref, *, tm):
    i = pl.program_id(1)
    halo = jnp.where(i == 0, 0.0, uh_ref[...])
    d = _pool_mixer(u_ref[...], halo, i * tm)
    a = jnp.dot(d.astype(BF16), pw_ref[...], preferred_element_type=F32) * ps_ref[...]
    mix = jnp.dot(a.astype(BF16), wa_ref[...], preferred_element_type=F32)
    mix = mix + jnp.dot(o_ref[...], wb_ref[...], preferred_element_type=F32)
    x1 = _layer_norm(ALPHA * x_ref[...] + mix, lw0_ref[...], lb0_ref[...])
    ffn = _swiglu_tile(x1.astype(BF16), wg_ref[...], wu_ref[...], wd_ref[...])
    out_ref[...] = _layer_norm(ALPHA * x1 + ffn, lw1_ref[...], lb1_ref[...])


def _layer0_tail(u, attn, x2d, pool_bd, pool_scale, wa, wb, ln0, wg, wu, wd, ln1, b, s, tm=512):
    d = x2d.shape[1]
    f = wg.shape[1]
    per = tm // POOL_HALO
    cur = lambda bi, i: (bi, i, 0)
    prev = lambda bi, i: (bi, jnp.maximum(i * per - 1, 0), 0)
    const = lambda bi, i: (0, 0)
    out = pl.pallas_call(
        functools.partial(_layer0_tail_kernel, tm=tm),
        grid=(b, s // tm),
        in_specs=[pl.BlockSpec((None, tm, POOL_WIDTH), cur),
                  pl.BlockSpec((None, POOL_HALO, POOL_WIDTH), prev),
                  pl.BlockSpec((None, tm, SWA_Q_DIM), cur),
                  pl.BlockSpec((None, tm, d), cur),
                  pl.BlockSpec((POOL_WIDTH, POOL_WIDTH), const),
                  pl.BlockSpec((1, POOL_WIDTH), const),
                  pl.BlockSpec((POOL_WIDTH, d), const),
                  pl.BlockSpec((SWA_Q_DIM, d), const),
                  pl.BlockSpec((1, d), const),
                  pl.BlockSpec((1, d), const),
                  pl.BlockSpec((d, f), const),
                  pl.BlockSpec((d, f), const),
                  pl.BlockSpec((f, d), const),
                  pl.BlockSpec((1, d), const),
                  pl.BlockSpec((1, d), const)],
        out_specs=pl.BlockSpec((None, tm, d), cur),
        out_shape=jax.ShapeDtypeStruct((b, s, d), F32),
        compiler_params=_params(("parallel", "parallel"), vmem=VMEM_LIMIT_BIG),
        name="layer0_tail",
    )(u.reshape(b, s, POOL_WIDTH), u.reshape(b, s, POOL_WIDTH), attn.reshape(b, s, SWA_Q_DIM),
      x2d.reshape(b, s, d), pool_bd, pool_scale, wa, wb, *ln0, wg, wu, wd, *ln1)
    return out.reshape(b * s, d)


def _kv_operands(kv, low):
    swapped = pltpu.roll(kv, HEAD_DIM, 1)
    return jnp.where(low, kv, swapped), jnp.where(low, swapped, kv)


def _inproj1_kernel(x_ref, wq_ref, wkv_ref, wc_ref, cw_ref, cos_ref, sin_ref,
                    q0_ref, q1_ref, q2_ref, kv0_ref, kv1_ref, kv2_ref, d_ref,
                    zc_ref, sq_ref, skv_ref, *, tm):
    i = pl.program_id(1)
    xb = x_ref[...].astype(BF16)
    cos = cos_ref[...]
    sin = sin_ref[...]
    lane = lax.broadcasted_iota(I32, (tm, LANES), 1)
    low = lane < HEAD_DIM
    cos_k = jnp.where(low, cos, 1.0)
    sin_k = jnp.where(low, sin, 0.0)
    q_refs = (q0_ref, q1_ref, q2_ref)
    kv_refs = (kv0_ref, kv1_ref, kv2_ref)
    for gi, (_, dil) in enumerate(DIL_PAIRS):
        q = jnp.dot(xb, wq_ref[gi], preferred_element_type=F32)
        q = _rope(q, cos, sin) * QK_SCALE
        kv = jnp.dot(xb, wkv_ref[gi], preferred_element_type=F32)
        kv = _rope(kv, cos_k, sin_k)
        kvx = [_kv_operands(kv[:, h * LANES:(h + 1) * LANES], low) for h in range(DIL_KV_HEADS)]
        if dil == 1:
            q_refs[gi][0] = q.astype(BF16)
            for h in range(DIL_KV_HEADS):
                kv_refs[gi][h, 0] = jnp.concatenate(kvx[h], axis=1).astype(BF16)
        else:
            n = tm // dil
            for c in range(DIL_Q_DIM // LANES):
                sq_ref[c] = q[:, c * LANES:(c + 1) * LANES]
            for h in range(DIL_KV_HEADS):
                for part in range(KV_PARTS):
                    skv_ref[h * KV_PARTS + part] = kvx[h][part]
            for r in range(dil):
                rows = pl.ds(r, n, stride=dil)
                q_refs[gi][r] = jnp.concatenate(
                    [sq_ref[c, rows, :] for c in range(DIL_Q_DIM // LANES)], axis=1).astype(BF16)
                for h in range(DIL_KV_HEADS):
                    kv_refs[gi][h, r] = jnp.concatenate(
                        [skv_ref[h * KV_PARTS + part, rows, :] for part in range(KV_PARTS)], axis=1).astype(BF16)

    hc = jnp.dot(xb, wc_ref[...], preferred_element_type=F32)
    z = hc[:, 2 * CONV_WIDTH:] * hc[:, :CONV_WIDTH]
    zprev = jnp.where(i == 0, 0.0, zc_ref[...])
    zfull = jnp.concatenate([zprev, z], axis=0)
    z1 = pltpu.roll(zfull, 1, 0)[CONV_HALO:]
    z2 = pltpu.roll(zfull, 2, 0)[CONV_HALO:]
    cw = cw_ref[...]
    y = cw[0:1] * z2 + cw[1:2] * z1 + cw[2:3] * z
    d_ref[...] = (hc[:, CONV_WIDTH:2 * CONV_WIDTH] * y).astype(BF16)
    zc_ref[...] = z[tm - CONV_HALO:]


def _inproj1(x2d, wq, wkv, wc, conv_w, cos_t, sin_t, b, s, tm=512):
    d = x2d.shape[1]
    cur = lambda bi, i: (bi, i, 0)
    row = lambda bi, i: (bi * (s // tm) + i, 0)
    in_specs = [pl.BlockSpec((None, tm, d), cur),
                pl.BlockSpec(wq.shape, lambda bi, i: (0, 0, 0)),
                pl.BlockSpec(wkv.shape, lambda bi, i: (0, 0, 0)),
                pl.BlockSpec(wc.shape, lambda bi, i: (0, 0)),
                pl.BlockSpec(conv_w.shape, lambda bi, i: (0, 0)),
                pl.BlockSpec((tm, LANES), row),
                pl.BlockSpec((tm, LANES), row)]
    out_specs, out_shape = [], []
    for _, dil in DIL_PAIRS:
        out_specs.append(pl.BlockSpec((None, dil, tm // dil, DIL_Q_DIM), lambda bi, i: (bi, 0, i, 0)))
        out_shape.append(jax.ShapeDtypeStruct((b, dil, s // dil, DIL_Q_DIM), BF16))
    for _, dil in DIL_PAIRS:
        out_specs.append(pl.BlockSpec((None, DIL_KV_HEADS, dil, tm // dil, KV_PARTS * LANES),
                                      lambda bi, i: (bi, 0, 0, i, 0)))
        out_shape.append(jax.ShapeDtypeStruct((b, DIL_KV_HEADS, dil, s // dil, KV_PARTS * LANES), BF16))
    out_specs.append(pl.BlockSpec((None, tm, CONV_WIDTH), cur))
    out_shape.append(jax.ShapeDtypeStruct((b, s, CONV_WIDTH), BF16))
    return pl.pallas_call(
        functools.partial(_inproj1_kernel, tm=tm),
        grid=(b, s // tm),
        in_specs=in_specs,
        out_specs=out_specs,
        out_shape=out_shape,
        scratch_shapes=[pltpu.VMEM((CONV_HALO, CONV_WIDTH), F32),
                        pltpu.VMEM((DIL_Q_DIM // LANES, tm, LANES), F32),
                        pltpu.VMEM((DIL_KV_HEADS * KV_PARTS, tm, LANES), F32)],
        compiler_params=_params(("parallel", "arbitrary")),
        name="inproj1",
    )(x2d.reshape(b, s, d), wq, wkv, wc, conv_w, cos_t, sin_t)


def _dil_kernel(q0_ref, q1_ref, q2_ref, kv0_ref, kv1_ref, kv2_ref, out_ref, o_run, l_run, bias_ref, *, s):
    q_refs = (q0_ref, q1_ref, q2_ref)
    kv_refs = (kv0_ref, kv1_ref, kv2_ref)
    n_blocks = s // ATT_BLOCK
    width = DIL_Q_PER_KV * HEAD_DIM
    n_chunks = width // LANES
    max_dist = DIL_PAIRS[0][0] // DIL_PAIRS[0][1]
    assert all(w // d == max_dist for w, d in DIL_PAIRS)
    stacked = n_chunks * ATT_BLOCK
    bias_ref[0] = _band_bias(stacked, max_dist, 0)
    bias_ref[1] = _band_bias(stacked, max_dist, ATT_BLOCK)
    for gi, (_, dil) in enumerate(DIL_PAIRS):
        q_ref, kv_ref = q_refs[gi], kv_refs[gi]
        blocks_per_sub = (s // dil) // ATT_BLOCK
        shift = blocks_per_sub.bit_length() - 1

        def body(n, carry, q_ref=q_ref, kv_ref=kv_ref, gi=gi, dil=dil, blocks_per_sub=blocks_per_sub, shift=shift):
            res = lax.shift_right_logical(n, shift)
            bi = n & (blocks_per_sub - 1)
            rows = pl.ds(pl.multiple_of(n * ATT_BLOCK, ATT_BLOCK), ATT_BLOCK)
            prow = pl.ds(pl.multiple_of(jnp.maximum(n - 1, 0) * ATT_BLOCK, ATT_BLOCK), ATT_BLOCK)
            kvx = jnp.concatenate([kv_ref[prow, :], kv_ref[rows, :]], axis=0)
            kk, vv = _kv_parts(kvx)
            q4 = q_ref[rows, :]
            qs = jnp.concatenate([q4[:, c * LANES:(c + 1) * LANES] for c in range(n_chunks)], axis=0)
            bias = bias_ref[jnp.where(bi == 0, 1, 0)]
            low = lax.broadcasted_iota(I32, (n_chunks * ATT_BLOCK, LANES), 1) < HEAD_DIM
            zero = jnp.zeros_like(qs)
            oa, ma, da = _attn_stack(jnp.where(low, qs, zero), kk, vv, bias)
            ob, mb, db = _attn_stack(jnp.where(low, zero, qs), kk, vv, bias)
            o_st = jnp.where(low, oa * (1.0 / da), ob * (1.0 / db))
            l_st = jnp.where(low, ma + jnp.log(da), mb + jnp.log(db))
            start = res + dil * ATT_BLOCK * bi
            if dil == 1:
                tok = pl.ds(pl.multiple_of(start, ATT_BLOCK), ATT_BLOCK)
            else:
                tok = pl.ds(start, ATT_BLOCK, stride=dil)
            for c in range(n_chunks):
                o_c = o_st[c * ATT_BLOCK:(c + 1) * ATT_BLOCK]
                l_c = l_st[c * ATT_BLOCK:(c + 1) * ATT_BLOCK]
                if gi == 0:
                    o_run[c, tok, :] = o_c
                    l_run[c, tok, :] = l_c
                else:
                    o_old = o_run[c, tok, :]
                    l_old = l_run[c, tok, :]
                    m = jnp.maximum(l_old, l_c)
                    a = jnp.exp(l_old - m)
                    bb = jnp.exp(l_c - m)
                    tot = a + bb
                    o_run[c, tok, :] = (o_old * a + o_c * bb) * (1.0 / tot)
                    if gi < len(DIL_PAIRS) - 1:
                        l_run[c, tok, :] = m + jnp.log(tot)
            return carry

        lax.fori_loop(0, n_blocks, body, 0, unroll=8)
    for c in range(n_chunks):
        out_ref[:, c * LANES:(c + 1) * LANES] = o_run[c].astype(BF16)


def _dilated(qs, kvs, b, s):
    width = DIL_Q_PER_KV * HEAD_DIM
    in_specs = [pl.BlockSpec((None, s, width), lambda bi, h: (bi, 0, h)) for _ in DIL_PAIRS]
    in_specs += [pl.BlockSpec((None, None, s, KV_PARTS * LANES), lambda bi, h: (bi, h, 0, 0)) for _ in DIL_PAIRS]
    qs = [q.reshape(b, s, DIL_Q_DIM) for q in qs]
    kvs = [kv.reshape(b, DIL_KV_HEADS, s, KV_PARTS * LANES) for kv in kvs]
    out = pl.pallas_call(
        functools.partial(_dil_kernel, s=s),
        grid=(b, DIL_KV_HEADS),
        in_specs=in_specs,
        out_specs=pl.BlockSpec((None, s, width), lambda bi, h: (bi, 0, h)),
        out_shape=jax.ShapeDtypeStruct((b, s, DIL_Q_DIM), BF16),
        scratch_shapes=[pltpu.VMEM((width // LANES, s, LANES), F32),
                        pltpu.VMEM((width // LANES, s, LANES), F32),
                        pltpu.VMEM((2, (width // LANES) * ATT_BLOCK, 2 * ATT_BLOCK), F32)],
        compiler_params=_params(("parallel", "parallel"), vmem=VMEM_LIMIT_BIG),
        name="dilated_attention",
    )(*qs, *kvs)
    return out.reshape(b * s, DIL_Q_DIM)


def _to_slabs(ref, val, base=0):
    rows, d = val.shape
    per = d // LANES
    for c in range(per):
        ref[pl.ds(base + c, rows, stride=per), :] = val[:, c * LANES:(c + 1) * LANES]


def _from_slabs(ref, rows, base=0, per=ROW_SLAB):
    return jnp.concatenate([ref[pl.ds(base + c, rows, stride=per), :] for c in range(per)], axis=1)


def _route_tile(x, wr, e_ref, g_ref, r_ref, cnt_ref, run_ref, tm, col0):
    cols = slice(col0, col0 + tm)
    logits = lax.dot_general(wr, x, (((1,), (1,)), ((), ())),
                             precision=lax.Precision.HIGHEST, preferred_element_type=F32)
    eid = lax.broadcasted_iota(I32, (N_EXPERTS, tm), 0)
    m1 = jnp.max(logits, axis=0, keepdims=True)
    i1 = jnp.min(jnp.where(logits == m1, eid, N_EXPERTS), axis=0, keepdims=True)
    rest = jnp.where(eid == i1, -jnp.inf, logits)
    m2 = jnp.max(rest, axis=0, keepdims=True)
    i2 = jnp.min(jnp.where(rest == m2, eid, N_EXPERTS), axis=0, keepdims=True)
    t2 = jnp.exp(m2 - m1)
    g1 = 1.0 / (1.0 + t2)
    g2 = t2 / (1.0 + t2)
    oh1 = eid == i1
    oh2 = eid == i2
    oh = jnp.where(oh1 | oh2, 1.0, 0.0)
    ri = lax.broadcasted_iota(I32, (tm, tm), 0)
    ci = lax.broadcasted_iota(I32, (tm, tm), 1)
    tri = jnp.where(ri <= ci, 1.0, 0.0).astype(BF16)
    incl = jnp.dot(oh.astype(BF16), tri, preferred_element_type=F32)
    rank = run_ref[:, 0:1] + incl - oh
    r1 = jnp.sum(jnp.where(oh1, rank, 0.0), axis=0, keepdims=True)
    r2 = jnp.sum(jnp.where(oh2, rank, 0.0), axis=0, keepdims=True)
    e_ref[0:1, cols] = i1
    e_ref[1:2, cols] = i2
    g_ref[0:1, cols] = g1
    g_ref[1:2, cols] = g2
    r_ref[0:1, cols] = r1.astype(I32)
    r_ref[1:2, cols] = r2.astype(I32)
    run_ref[...] = run_ref[...] + incl[:, tm - 1:tm]
    cnt_ref[...] = run_ref[...]


def _outproj1_kernel(c_ref, d_ref, x_ref, wa_ref, wb_ref, lw_ref, lb_ref, wr_ref,
                     slab_ref, e_ref, g_ref, r_ref, cnt_ref, run_ref, *, tm, sub):
    @pl.when(pl.program_id(0) == 0)
    def _():
        run_ref[...] = jnp.zeros_like(run_ref)

    for h in range(tm // sub):
        rows = slice(h * sub, (h + 1) * sub)
        mix = jnp.dot(c_ref[rows, :], wa_ref[...], preferred_element_type=F32)
        mix = mix + jnp.dot(d_ref[rows, :], wb_ref[...], preferred_element_type=F32)
        y = _layer_norm(ALPHA * x_ref[rows, :] + mix, lw_ref[...], lb_ref[...])
        _to_slabs(slab_ref, y, base=h * sub * ROW_SLAB)
        _route_tile(y, wr_ref[...], e_ref, g_ref, r_ref, cnt_ref, run_ref, sub, h * sub)


def _outproj1(c, dconv, x2d, wa, wb, ln_w, ln_b, wr_t, tm=1024, sub=512):
    t, d = x2d.shape
    row = lambda i: (i, 0)
    col = lambda i: (0, i)
    const = lambda i: (0, 0)
    return pl.pallas_call(
        functools.partial(_outproj1_kernel, tm=tm, sub=sub),
        grid=(t // tm,),
        in_specs=[pl.BlockSpec((tm, DIL_Q_DIM), row),
                  pl.BlockSpec((tm, CONV_WIDTH), row),
                  pl.BlockSpec((tm, d), row),
                  pl.BlockSpec((DIL_Q_DIM, d), const),
                  pl.BlockSpec((CONV_WIDTH, d), const),
                  pl.BlockSpec((1, d), const),
                  pl.BlockSpec((1, d), const),
                  pl.BlockSpec((N_EXPERTS, d), const)],
        out_specs=[pl.BlockSpec((tm * (d // LANES), LANES), row),
                   pl.BlockSpec((2, tm), col), pl.BlockSpec((2, tm), col), pl.BlockSpec((2, tm), col),
                   pl.BlockSpec((N_EXPERTS, LANES), const)],
        out_shape=[jax.ShapeDtypeStruct((t * (d // LANES), LANES), F32),
                   jax.ShapeDtypeStruct((2, t), I32), jax.ShapeDtypeStruct((2, t), F32),
                   jax.ShapeDtypeStruct((2, t), I32), jax.ShapeDtypeStruct((N_EXPERTS, LANES), F32)],
        scratch_shapes=[pltpu.VMEM((N_EXPERTS, LANES), F32)],
        compiler_params=_params(("arbitrary",)),
        name="outproj1_route",
    )(c, dconv, x2d, wa, wb, ln_w, ln_b, wr_t)


def _slab_copy(src_ref, dst_ref, sem, src_row, dst_row, n=1):
    src = src_ref.at[pl.ds(pl.multiple_of(src_row * ROW_SLAB, ROW_SLAB), n * ROW_SLAB)]
    dst = dst_ref.at[pl.ds(pl.multiple_of(dst_row * ROW_SLAB, ROW_SLAB), n * ROW_SLAB)]
    return pltpu.make_async_copy(src, dst, sem)


def _moe_kernel(be_ref, nv_ref, tok_ref, tokn_ref, dstp_ref, dstc_ref, x_hbm, wg_ref, wu_ref, wd_ref, y_hbm,
                xs_ref, stage_ref, sem_in, sem_out, *, tm, dump_row0):
    i = pl.program_id(0)
    n_valid = nv_ref[0]
    slot = i & 1

    def row_in(tok, dst_slot, r):
        return _slab_copy(x_hbm, xs_ref, sem_in.at[dst_slot], tok, dst_slot * tm + r)

    def row_out(src_slot, r, dst):
        return _slab_copy(stage_ref, y_hbm, sem_out.at[src_slot], src_slot * tm + r, dst)

    @pl.when(i < n_valid)
    def _():
        @pl.when(i == 0)
        def _():
            def start(r, c):
                row_in(tok_ref[0, 0, r], 0, r).start()
                return c
            lax.fori_loop(0, tm, start, 0)
            stage_ref[...] = jnp.zeros_like(stage_ref)
            fill = _slab_copy(stage_ref, y_hbm, sem_out.at[0], 0, dump_row0, n=2 * tm)
            fill.start()
            fill.wait()

        _slab_copy(x_hbm, xs_ref, sem_in.at[slot], 0, slot * tm, n=tm).wait()
        xb = _from_slabs(xs_ref, tm, base=slot * (tm * ROW_SLAB)).astype(BF16)

        for r in range(tm):
            row_in(tokn_ref[0, 0, r], 1 - slot, r).start(priority=1)
            row_out(1 - slot, r, dstp_ref[0, 0, r]).start(priority=1)

        part = _swiglu_tile(xb, wg_ref[...], wu_ref[...], wd_ref[...])
        _slab_copy(stage_ref, y_hbm, sem_out.at[1 - slot], (1 - slot) * tm, 0, n=tm).wait()
        _to_slabs(stage_ref, part, base=slot * (tm * ROW_SLAB))

        @pl.when(i + 1 >= n_valid)
        def _():
            _slab_copy(x_hbm, xs_ref, sem_in.at[1 - slot], 0, (1 - slot) * tm, n=tm).wait()

            def start(r, c):
                row_out(slot, r, dstc_ref[0, 0, r]).start()
                return c
            lax.fori_loop(0, tm, start, 0)
            _slab_copy(stage_ref, y_hbm, sem_out.at[slot], slot * tm, 0, n=tm).wait()


def _moe_experts(x_slabs, row_tok, row_dst, dump_row0, block_e, n_valid, wg, wu, wd, tm):
    d = wg.shape[1]
    n_rows = row_tok.shape[0]
    n_blocks = n_rows // tm
    f = wg.shape[2]
    y_rows = dump_row0 + 2 * tm

    def w_map(i, be, nv):
        return be[jnp.minimum(i, nv[0] - 1)], 0, 0

    def smem(index_map):
        return pl.BlockSpec((1, 1, tm), index_map, memory_space=pltpu.SMEM)

    resident = dict(pipeline_mode=pl.Buffered(1))
    grid_spec = pltpu.PrefetchScalarGridSpec(
        num_scalar_prefetch=2,
        grid=(n_blocks,),
        in_specs=[smem(lambda i, be, nv: (i, 0, 0)),
                  smem(lambda i, be, nv: (jnp.minimum(i + 1, n_blocks - 1), 0, 0)),
                  smem(lambda i, be, nv: (jnp.where(i == 0, n_blocks, i - 1), 0, 0)),
                  smem(lambda i, be, nv: (i, 0, 0)),
                  pl.BlockSpec(memory_space=pl.ANY),
                  pl.BlockSpec((None, d, f), w_map, **resident),
                  pl.BlockSpec((None, d, f), w_map, **resident),
                  pl.BlockSpec((None, f, d), w_map, **resident)],
        out_specs=pl.BlockSpec(memory_space=pl.ANY),
        scratch_shapes=[pltpu.VMEM((2 * tm * ROW_SLAB, LANES), F32),
                        pltpu.VMEM((2 * tm * ROW_SLAB, LANES), F32),
                        pltpu.SemaphoreType.DMA((2,)),
                        pltpu.SemaphoreType.DMA((2,))],
    )
    tok3 = row_tok.reshape(n_blocks, 1, tm)
    dst3 = row_dst.reshape(n_blocks + 1, 1, tm)
    return pl.pallas_call(
        functools.partial(_moe_kernel, tm=tm, dump_row0=dump_row0),
        grid_spec=grid_spec,
        out_shape=jax.ShapeDtypeStruct((y_rows * ROW_SLAB, LANES), F32),
        compiler_params=_params(("arbitrary",), vmem=VMEM_LIMIT_BIG),
        name="moe_experts",
    )(block_e, n_valid, tok3, tok3, dst3, dst3, x_slabs, wg, wu, wd)


def _combine_kernel(ya_ref, yb_ref, x_ref, g1_ref, g2_ref, lw_ref, lb_ref, out_ref, *, tm):
    ffn = _from_slabs(ya_ref, tm) * g1_ref[...] + _from_slabs(yb_ref, tm) * g2_ref[...]
    out_ref[...] = _layer_norm(ALPHA * _from_slabs(x_ref, tm) + ffn, lw_ref[...], lb_ref[...])


def _combine(y, x_slabs, g1, g2, ln_w, ln_b, tm=512):
    t = g1.shape[0]
    d = ln_w.shape[1]
    row = lambda i: (i, 0)
    return pl.pallas_call(
        functools.partial(_combine_kernel, tm=tm),
        grid=(t // tm,),
        in_specs=[pl.BlockSpec((tm * ROW_SLAB, LANES), row),
                  pl.BlockSpec((tm * ROW_SLAB, LANES), lambda i: (t // tm + i, 0)),
                  pl.BlockSpec((tm * ROW_SLAB, LANES), row),
                  pl.BlockSpec((tm, 1), row),
                  pl.BlockSpec((tm, 1), row),
                  pl.BlockSpec((1, d), lambda i: (0, 0)),
                  pl.BlockSpec((1, d), lambda i: (0, 0))],
        out_specs=pl.BlockSpec((tm, d), row),
        out_shape=jax.ShapeDtypeStruct((t, d), F32),
        compiler_params=_params(("parallel",)),
        name="moe_combine",
    )(y, y, x_slabs, g1.reshape(t, 1), g2.reshape(t, 1), ln_w, ln_b)


SC_CORES = 2
SC_SUBCORES = 16
SC_LANES = 16
SC_CHUNK = 8192


def _invert_rows(idx, vals, default):
    n_pairs = idx.shape[0]
    n_out = default.shape[0]
    n_workers = SC_CORES * SC_SUBCORES
    per = n_out // n_workers
    assert per * n_workers == n_out and per % 8 == 0 and n_pairs % SC_CHUNK == 0
    mesh = plsc.VectorSubcoreMesh(core_axis_name="c", subcore_axis_name="s",
                                  num_cores=SC_CORES, num_subcores=SC_SUBCORES)

    def body(idx_hbm, val_hbm, dflt_hbm, out_hbm, idx_v, val_v, loc_v):
        wid = lax.axis_index("s") * SC_CORES + lax.axis_index("c")
        lo = wid * per
        pltpu.sync_copy(dflt_hbm.at[pl.ds(lo, per)], loc_v)

        def chunk(c, carry):
            pltpu.sync_copy(idx_hbm.at[pl.ds(c * SC_CHUNK, SC_CHUNK)], idx_v)
            pltpu.sync_copy(val_hbm.at[pl.ds(c * SC_CHUNK, SC_CHUNK)], val_v)

            def step(j, carry):
                rel = idx_v[pl.ds(j * SC_LANES, SC_LANES)] - lo
                mine = (rel >= 0) & (rel < per)
                plsc.store_scatter(loc_v, [jnp.where(mine, rel, 0)], val_v[pl.ds(j * SC_LANES, SC_LANES)],
                                   mask=mine)
                return carry

            return lax.fori_loop(0, SC_CHUNK // SC_LANES, step, carry)

        lax.fori_loop(0, n_pairs // SC_CHUNK, chunk, 0)
        pltpu.sync_copy(loc_v, out_hbm.at[pl.ds(lo, per)])

    return pl.kernel(
        body,
        out_type=jax.ShapeDtypeStruct((n_out,), I32),
        mesh=mesh,
        scratch_types=[pltpu.VMEM((SC_CHUNK,), I32), pltpu.VMEM((SC_CHUNK,), I32), pltpu.VMEM((per,), I32)],
        compiler_params=pltpu.CompilerParams(needs_layout_passes=False),
        name="moe_invert_rows",
    )(idx, vals, default)


def _moe_layer(x_slabs, e_sel, gates, ranks, counts, wg, wu, wd, ln_w, ln_b, tmoe=512):
    t = e_sel.shape[1]
    counts = counts[:, 0].astype(I32)
    padded = (counts + tmoe - 1) // tmoe * tmoe
    pend = jnp.cumsum(padded)
    pstart = pend - padded
    n_blocks = -(-(2 * t + N_EXPERTS * (tmoe - 1)) // tmoe)
    n_rows = n_blocks * tmoe
    first_row = sum(jnp.where(e_sel == e, pstart[e], 0) for e in range(N_EXPERTS))
    dest = first_row + ranks
    dump_row0 = 2 * t
    r = jnp.arange(n_rows, dtype=I32)
    dump = dump_row0 + ((r // tmoe) & 1) * tmoe + r % tmoe
    tok = jnp.arange(t, dtype=I32)
    row_dst = _invert_rows(dest.reshape(-1), jnp.concatenate([tok, t + tok]), dump)
    row_tok = jnp.where(row_dst < dump_row0, row_dst % t, 0)
    row_dst = jnp.concatenate([row_dst, dump_row0 + tmoe + jnp.arange(tmoe, dtype=I32)])
    block_row0 = jnp.arange(n_blocks, dtype=I32) * tmoe
    block_e = jnp.minimum(jnp.sum((block_row0[:, None] >= pend[None, :]).astype(I32), axis=1), N_EXPERTS - 1)
    n_valid = (pend[-1:] // tmoe).astype(I32)
    y = _moe_experts(x_slabs, row_tok, row_dst, dump_row0, block_e, n_valid, wg, wu, wd, tmoe)
    return _combine(y, x_slabs, gates[0], gates[1], ln_w, ln_b)


def kernel(x, positions, ln_w, ln_b, even_w_in, pool_w, pool_scale, swa_sinks, even_w_out, ffn_w_gate, ffn_w_up, ffn_w_down, odd_w_in, conv_w, odd_w_out, router_w, moe_w_gate, moe_w_up, moe_w_down):
    b, s, d = x.shape
    t = b * s
    x2d = x.reshape(t, d)
    ln = lambda layer, k: (ln_w[layer, k].reshape(1, d), ln_b[layer, k].reshape(1, d))

    u, q, k, v, cos_t, sin_t = _inproj0(x2d, even_w_in[0].astype(BF16), positions)
    moe_w = (moe_w_gate[0], moe_w_up[0], moe_w_down[0])
    attn, moe_bf = _swa(q, k, v, swa_sinks[0], b, s, cast=[w.reshape(-1, w.shape[-1]) for w in moe_w])
    moe_bf = [wb.reshape(w.shape) for wb, w in zip(moe_bf, moe_w)]
    pool_bd = jax.scipy.linalg.block_diag(*[pool_w[0, gi] for gi in range(len(POOL_WINDOWS))]).astype(BF16)
    w_out0 = even_w_out[0].astype(BF16)
    x2d = _layer0_tail(u, attn, x2d, pool_bd, pool_scale[0].reshape(1, POOL_WIDTH),
                       w_out0[:POOL_WIDTH], w_out0[POOL_WIDTH:], ln(0, 0),
                       ffn_w_gate[0].astype(BF16), ffn_w_up[0].astype(BF16), ffn_w_down[0].astype(BF16),
                       ln(0, 1), b, s)

    w_in1 = odd_w_in[0].astype(BF16)
    c_in = len(DIL_PAIRS) * DIL_GROUP_IN
    wq, wkv = [], []
    for gi in range(len(DIL_PAIRS)):
        g0 = gi * DIL_GROUP_IN
        wq.append(w_in1[:, g0:g0 + DIL_Q_DIM])
        kcol = g0 + DIL_Q_DIM
        vcol = kcol + DIL_KV_DIM
        parts = []
        for h in range(DIL_KV_HEADS):
            parts += [w_in1[:, kcol + h * HEAD_DIM:kcol + (h + 1) * HEAD_DIM],
                      w_in1[:, vcol + h * HEAD_DIM:vcol + (h + 1) * HEAD_DIM]]
        wkv.append(jnp.concatenate(parts, axis=1))
    outs = _inproj1(x2d, jnp.stack(wq), jnp.stack(wkv), w_in1[:, c_in:], conv_w[0], cos_t, sin_t, b, s)
    c_out = _dilated(outs[0:3], outs[3:6], b, s)
    w_out1 = odd_w_out[0].astype(BF16)
    x_slabs, e_sel, gates, ranks, counts = _outproj1(
        c_out, outs[6].reshape(t, CONV_WIDTH), x2d, w_out1[:DIL_Q_DIM], w_out1[DIL_Q_DIM:], *ln(1, 0),
        router_w[0].T)
    x2d = _moe_layer(x_slabs, e_sel, gates, ranks, counts, *moe_bf, *ln(1, 1))
    return x2d.reshape(b, s, d)
```
